```python
import math
import jax, jax.numpy as jnp
from jax import lax
import numpy as np

D_MODEL = 2048
BATCH = 4
SEQ = 4096
DEPTH = 1

HEAD_DIM = 64
D_RWKV = D_MODEL // 2
D_SB = D_MODEL - D_RWKV
N_RWKV_HEADS = D_RWKV // HEAD_DIM
N_SB_HEADS = D_SB // HEAD_DIM
D_IN_PROJ = 3 * D_RWKV + 3 * D_SB
DECAY_LORA = 64
AAA_LORA = 64
GATE_LORA = 160
D_FF = ((8 * D_MODEL + 3 * 256 - 1) // (3 * 256)) * 256
SB_BLOCK = 128
RMS_EPS = 1e-6
GN_EPS = 64e-5
L2_EPS = 1e-12

kernel_name = "hymba_rwkv7_stickbreaking_block"


def rms_norm(x, gain):
    xf = x.astype(jnp.float32)
    y = xf * lax.rsqrt(jnp.mean(xf * xf, axis=-1, keepdims=True) + RMS_EPS)
    return (y * gain.astype(jnp.float32)).astype(x.dtype)


def token_shift(x):
    return jnp.pad(x[:, :-1], ((0, 0), (1, 0), (0, 0)))


def rwkv7_time_mix(h, p_rkv, mu_rkv, mu_w, mu_a, mu_g, w0, w1, w2, a0, a1, a2,
                   g1, g2, k_k, k_a, r_k, ln_x_gain, ln_x_bias):
    B, T, _ = h.shape
    H, N = N_RWKV_HEADS, HEAD_DIM
    dh = token_shift(h) - h
    xw = h + dh * mu_w
    xa = h + dh * mu_a
    xg = h + dh * mu_g
    p = p_rkv + (token_shift(p_rkv) - p_rkv) * mu_rkv
    r, k, v = jnp.split(p, 3, axis=-1)
    w_log = -jax.nn.softplus(-(w0 + jnp.tanh(xw @ w1) @ w2)) - 0.5
    decay = jnp.exp(-jnp.exp(w_log.astype(jnp.float32)))
    a = jax.nn.sigmoid(a0 + (xa @ a1) @ a2)
    g = jax.nn.sigmoid(xg @ g1) @ g2
    kk = (k * k_k).reshape(B, T, H, N).astype(jnp.float32)
    kk = kk * lax.rsqrt(jnp.sum(kk * kk, axis=-1, keepdims=True) + L2_EPS)
    k = k * (1.0 + (a - 1.0) * k_a)
    rh = r.reshape(B, T, H, N)
    kh = k.reshape(B, T, H, N)
    vh = v.reshape(B, T, H, N)
    ah = a.reshape(B, T, H, N).astype(jnp.float32)
    wh = decay.reshape(B, T, H, N)

    def step(S, inp):
        r_t, w_t, k_t, v_t, rem_t, wr_t = inp
        sa = jnp.einsum('bhvk,bhk->bhv', S, rem_t)
        S = (S * w_t[:, :, None, :] + sa[..., None] * wr_t[:, :, None, :]
             + v_t[..., None] * k_t[:, :, None, :])
        y = jnp.einsum('bhvk,bhk->bhv', S, r_t)
        return S, y

    xs = tuple(jnp.moveaxis(t.astype(jnp.float32), 1, 0)
               for t in (rh, wh, kh, vh, -kk, kk * ah))
    S0 = jnp.zeros((B, H, N, N), jnp.float32)
    _, y = lax.scan(step, S0, xs)
    y = jnp.moveaxis(y, 0, 1)
    mean = jnp.mean(y, axis=-1, keepdims=True)
    var = jnp.mean(jnp.square(y - mean), axis=-1, keepdims=True)
    y = ((y - mean) * lax.rsqrt(var + GN_EPS)).reshape(B, T, D_RWKV)
    y = (y * ln_x_gain + ln_x_bias).astype(h.dtype)
    bonus = jnp.sum(rh * kh * r_k, axis=-1, keepdims=True) * vh
    return (y + bonus.reshape(B, T, D_RWKV)) * g


def stick_breaking_attention(q, k, v):
    B, H, T, d = q.shape
    nblk = T // SB_BLOCK
    qb = q.reshape(B, H, nblk, SB_BLOCK, d).transpose(2, 0, 1, 3, 4)
    kpos = jnp.arange(T)
    inv_sqrt_d = 1.0 / math.sqrt(d)

    def block(args):
        qi, i = args
        z = jnp.einsum('bhqd,bhkd->bhqk', qi, k).astype(jnp.float32) * inv_sqrt_d
        qpos = i * SB_BLOCK + jnp.arange(SB_BLOCK)
        causal = kpos[None, :] < qpos[:, None]
        log_1m_beta = jnp.where(causal, -jax.nn.softplus(z), 0.0)
        tail = lax.cumsum(log_1m_beta, axis=3, reverse=True) - log_1m_beta
        log_a = jax.nn.log_sigmoid(z) + tail
        attn = jnp.where(causal, jnp.exp(log_a), 0.0)
        return jnp.einsum('bhqk,bhkd->bhqd', attn.astype(v.dtype), v)

    out = lax.map(block, (qb, jnp.arange(nblk)))
    return out.transpose(1, 2, 0, 3, 4).reshape(B, H, T, d)


def setup_inputs(seed: int = 0) -> dict:
    key = jax.random.key(seed)
    ks = jax.random.split(key, 32)
    L = DEPTH

    def nrm(k, shape, scale):
        return jax.random.normal(k, shape, jnp.float32) * scale

    def uni(k, shape, lo=0.0, hi=1.0):
        return jax.random.uniform(k, shape, jnp.float32, minval=lo, maxval=hi)

    Dm = D_MODEL
    return {
        "x": nrm(ks[0], (BATCH, SEQ, Dm), 1.0),
        "c": nrm(ks[1], (BATCH, Dm), 1.0),
        "w_ada": nrm(ks[2], (L, Dm, 6 * Dm), 0.5 * Dm ** -0.5),
        "b_ada": nrm(ks[3], (L, 6 * Dm), 0.02),
        "norm1_gain": 1.0 + nrm(ks[4], (L, Dm), 0.05),
        "norm2_gain": 1.0 + nrm(ks[5], (L, Dm), 0.05),
        "w_in": nrm(ks[6], (L, Dm, D_IN_PROJ), Dm ** -0.5),
        "mu_rkv": uni(ks[7], (L, 3 * D_RWKV)),
        "mu_w": uni(ks[8], (L, Dm)),
        "mu_a": uni(ks[9], (L, Dm)),
        "mu_g": uni(ks[10], (L, Dm)),
        "w0": uni(ks[11], (L, D_RWKV), -5.5, -0.5),
        "w1": nrm(ks[12], (L, Dm, DECAY_LORA), Dm ** -0.5),
        "w2": nrm(ks[13], (L, DECAY_LORA, D_RWKV), 0.3 * DECAY_LORA ** -0.5),
        "a0": nrm(ks[14], (L, D_RWKV), 0.1),
        "a1": nrm(ks[15], (L, Dm, AAA_LORA), Dm ** -0.5),
        "a2": nrm(ks[16], (L, AAA_LORA, D_RWKV), 0.3 * AAA_LORA ** -0.5),
        "g1": nrm(ks[17], (L, Dm, GATE_LORA), Dm ** -0.5),
        "g2": nrm(ks[18], (L, GATE_LORA, D_RWKV), GATE_LORA ** -0.5),
        "k_k": 0.85 + nrm(ks[19], (L, D_RWKV), 0.05),
        "k_a": 1.0 + nrm(ks[20], (L, D_RWKV), 0.05),
        "r_k": nrm(ks[21], (L, N_RWKV_HEADS, HEAD_DIM), 0.1),
        "ln_x_gain": 1.0 + nrm(ks[22], (L, D_RWKV), 0.05),
        "ln_x_bias": nrm(ks[23], (L, D_RWKV), 0.02),
        "q_norm_gain": 1.0 + nrm(ks[24], (L, HEAD_DIM), 0.05),
        "k_norm_gain": 1.0 + nrm(ks[25], (L, HEAD_DIM), 0.05),
        "w_out": nrm(ks[26], (L, Dm, Dm), Dm ** -0.5),
        "w_gate_up": nrm(ks[27], (L, Dm, 2 * D_FF), Dm ** -0.5),
        "w_down": nrm(ks[28], (L, D_FF, Dm), D_FF ** -0.5),
    }


def reference(x, c, w_ada, b_ada, norm1_gain, norm2_gain, w_in, mu_rkv, mu_w, mu_a,
              mu_g, w0, w1, w2, a0, a1, a2, g1, g2, k_k, k_a, r_k, ln_x_gain,
              ln_x_bias, q_norm_gain, k_norm_gain, w_out, w_gate_up, w_down):
    B, T, _ = x.shape
    c_act = jax.nn.silu(c)
    for l in range(DEPTH):
        mod = c_act @ w_ada[l] + b_ada[l]
        sh1, sc1, gt1, sh2, sc2, gt2 = [m[:, None, :] for m in jnp.split(mod, 6, axis=-1)]

        h = rms_norm(x, norm1_gain[l]) * (1.0 + sc1) + sh1
        p = h @ w_in[l]
        p_rkv, p_sb = p[..., :3 * D_RWKV], p[..., 3 * D_RWKV:]

        y_rwkv = rwkv7_time_mix(h, p_rkv, mu_rkv[l], mu_w[l], mu_a[l], mu_g[l],
                                w0[l], w1[l], w2[l], a0[l], a1[l], a2[l], g1[l], g2[l],
                                k_k[l], k_a[l], r_k[l], ln_x_gain[l], ln_x_bias[l])

        q, k, v = jnp.split(p_sb, 3, axis=-1)
        q = rms_norm(q.reshape(B, T, N_SB_HEADS, HEAD_DIM), q_norm_gain[l])
        k = rms_norm(k.reshape(B, T, N_SB_HEADS, HEAD_DIM), k_norm_gain[l])
        v = v.reshape(B, T, N_SB_HEADS, HEAD_DIM)
        y_sb = stick_breaking_attention(q.transpose(0, 2, 1, 3), k.transpose(0, 2, 1, 3),
                                        v.transpose(0, 2, 1, 3))
        y_sb = y_sb.transpose(0, 2, 1, 3).reshape(B, T, D_SB)

        mix = jnp.concatenate([y_rwkv, y_sb], axis=-1) @ w_out[l]
        x = x + gt1 * mix

        h2 = rms_norm(x, norm2_gain[l]) * (1.0 + sc2) + sh2
        gate, up = jnp.split(h2 @ w_gate_up[l], 2, axis=-1)
        x = x + gt2 * ((jax.nn.silu(gate) * up) @ w_down[l])
    return x
```

```python
import functools
import math

import jax
import jax.numpy as jnp
from jax import lax
from jax.experimental import pallas as pl
from jax.experimental.pallas import tpu as pltpu

F32 = jnp.float32
BF16 = jnp.bfloat16

HEAD_DIM = 64
RMS_EPS = 1e-6
GN_EPS = 64e-5
L2_EPS = 1e-12
LANES = 128
CHUNK = 64
VMEM_LIMIT = 48 * 1024 * 1024


def _cparams(n_axes):
    return pltpu.CompilerParams(dimension_semantics=("arbitrary",) * n_axes,
                                vmem_limit_bytes=VMEM_LIMIT)


def _sigmoid(x):
    return 1.0 / (1.0 + jnp.exp(-x))


def _softplus(x):
    return jnp.maximum(x, 0.0) + jnp.log(1.0 + jnp.exp(-jnp.abs(x)))


def _dot(x, y):
    return jnp.dot(x, y, preferred_element_type=F32)


def _dot_nt(x, y):
    return lax.dot_general(x, y, (((1,), (1,)), ((), ())), preferred_element_type=F32)


def _split(x):
    hi = x.astype(BF16)
    lo = (x - hi.astype(F32)).astype(BF16)
    return hi, lo


def _mm3(x, y, nt=False):
    d = _dot_nt if nt else _dot
    xh, xl = _split(x)
    yh, yl = _split(y)
    return d(xh, yh) + (d(xh, yl) + d(xl, yh))


def _mm2_exact_rhs(x, y_bf16):
    xh, xl = _split(x)
    return _dot(xh, y_bf16) + _dot(xl, y_bf16)


def _ada_kernel(c_ref, w_ref, b_ref, o_ref):
    c = c_ref[...]
    ca = c * _sigmoid(c)
    o_ref[...] = _dot(ca.astype(BF16), w_ref[...].astype(BF16)) + b_ref[...]


def _ada(c_pad, w_ada, b_ada, tn=1024):
    m, d = c_pad.shape
    n = w_ada.shape[1]
    return pl.pallas_call(
        _ada_kernel,
        out_shape=jax.ShapeDtypeStruct((m, n), F32),
        grid=(n // tn,),
        in_specs=[pl.BlockSpec((m, d), lambda j: (0, 0)),
                  pl.BlockSpec((d, tn), lambda j: (0, j)),
                  pl.BlockSpec((1, tn), lambda j: (0, j))],
        out_specs=pl.BlockSpec((m, tn), lambda j: (0, j)),
        compiler_params=_cparams(1),
        name="ada",
    )(c_pad, w_ada, b_ada)


def _fold_kernel(w_ref, mu_ref, o_ref):
    w = w_ref[...]
    mu = mu_ref[...]
    half = w.shape[1]
    o_ref[:, :half] = (w * (1.0 - mu)).astype(BF16)
    o_ref[:, half:] = (w * mu).astype(BF16)


def _fold(w_l, mu_l):
    d, half = w_l.shape
    return pl.pallas_call(
        _fold_kernel,
        out_shape=jax.ShapeDtypeStruct((d, 2 * half), BF16),
        compiler_params=pltpu.CompilerParams(vmem_limit_bytes=VMEM_LIMIT),
        name="fold",
    )(w_l, mu_l)


def _inproj_kernel(x_ref, g_ref, sc_ref, sh_ref, w_ref, o_ref, h_scr):
    @pl.when(pl.program_id(1) == 0)
    def _():
        x = x_ref[...]
        ms = jnp.mean(x * x, axis=-1, keepdims=True)
        y = x * lax.rsqrt(ms + RMS_EPS) * g_ref[...]
        h_scr[...] = (y * (1.0 + sc_ref[0]) + sh_ref[0]).astype(BF16)

    o_ref[...] = _dot(h_scr[...], w_ref[...])


def _inproj(x2, gain, sc, sh, w_ext, seq, tm=512, tn=1024):
    m, d = x2.shape
    n = w_ext.shape[1]
    per_seq = seq // tm
    return pl.pallas_call(
        _inproj_kernel,
        out_shape=jax.ShapeDtypeStruct((m, n), F32),
        grid=(m // tm, n // tn),
        in_specs=[pl.BlockSpec((tm, d), lambda i, j: (i, 0)),
                  pl.BlockSpec((1, d), lambda i, j: (0, 0)),
                  pl.BlockSpec((1, 1, d), lambda i, j: (i // per_seq, 0, 0)),
                  pl.BlockSpec((1, 1, d), lambda i, j: (i // per_seq, 0, 0)),
                  pl.BlockSpec((d, tn), lambda i, j: (0, j))],
        out_specs=pl.BlockSpec((tm, tn), lambda i, j: (i, j)),
        scratch_shapes=[pltpu.VMEM((tm, d), BF16)],
        compiler_params=_cparams(2),
        name="inproj",
    )(x2, gain, sc, sh, w_ext)


def _shift_rows(cur, prev_row):
    rolled = pltpu.roll(cur, 1, axis=0)
    row = lax.broadcasted_iota(jnp.int32, cur.shape, 0)
    return jnp.where(row == 0, jnp.broadcast_to(prev_row, cur.shape), rolled)


def _rwkv_kernel(p_ref, l_ref, mu_ref, w0_ref, w2_ref, a0_ref, a2_ref, g2_ref, kk_ref, ka_ref,
                 rk_ref, lng_ref, lnb_ref, hs_ref, o_ref, s_scr, prevp_scr, prevl_scr, y_scr,
                 *, n_heads):
    dr = n_heads * HEAD_DIM
    c = CHUNK

    @pl.when(pl.program_id(1) == 0)
    def _():
        s_scr[...] = jnp.zeros_like(s_scr)
        prevp_scr[...] = jnp.zeros_like(prevp_scr)
        prevl_scr[...] = jnp.zeros_like(prevl_scr)

    hs = hs_ref[...]

    def headsum(t):
        return _mm2_exact_rhs(t, hs)

    p = p_ref[0]
    pshift = _shift_rows(p, prevp_scr[...])
    prevp_scr[...] = p[c - 1:c, :]
    pm = p + (pshift - p) * mu_ref[...]
    r = pm[:, :dr]
    k = pm[:, dr:2 * dr]
    v = pm[:, 2 * dr:]

    l = l_ref[0]
    half = l.shape[1] // 2
    lb = l[:, half:]
    lin = l[:, :half] + _shift_rows(lb, prevl_scr[...])
    prevl_scr[...] = lb[c - 1:c, :]
    nw = w2_ref.shape[0]
    na = a2_ref.shape[0]
    lw = jnp.tanh(lin[:, :nw])
    la = lin[:, nw:nw + na]
    lg = _sigmoid(lin[:, nw + na:])

    w_pre = w0_ref[...] + _dot(lw.astype(BF16), w2_ref[...])
    w_log = -_softplus(-w_pre) - 0.5
    logw = -jnp.exp(w_log)
    a_gate = _sigmoid(a0_ref[...] + _dot(la.astype(BF16), a2_ref[...]))
    g = _dot(lg.astype(BF16), g2_ref[...])

    kk = k * kk_ref[...]
    kk = kk * lax.rsqrt(headsum(kk * kk) + L2_EPS)
    k2 = k * (1.0 + (a_gate - 1.0) * ka_ref[...])
    avec = -kk
    bvec = kk * a_gate
    bonus = headsum(r * k2 * rk_ref[...]) * v

    ti = lax.broadcasted_iota(jnp.int32, (c, c), 0)
    si = lax.broadcasted_iota(jnp.int32, (c, c), 1)
    strict = ti > si
    incl = ti >= si
    tri = jnp.where(incl, 1.0, 0.0).astype(BF16)
    l1, l2 = _split(logw)
    l3 = (logw - l1.astype(F32) - l2.astype(F32)).astype(BF16)
    lgc = _dot(tri, l1) + (_dot(tri, l2) + _dot(tri, l3))
    lg_end = lgc[c - 1:c, :]
    at = avec * jnp.exp(lgc - logw)
    ginv = jnp.exp(-lgc)
    bt = bvec * ginv
    kt = k2 * ginv
    rt = r * jnp.exp(lgc)
    gend = jnp.exp(lg_end - lgc)
    bh = bvec * gend
    kh = k2 * gend
    g_end = jnp.exp(lg_end)
    eye = jnp.where(ti == si, 1.0, 0.0)

    for h in range(n_heads):
        sl = slice(h * HEAD_DIM, (h + 1) * HEAD_DIM)
        a_, r_, b_, k_, v_ = at[:, sl], rt[:, sl], bt[:, sl], kt[:, sl], v[:, sl]
        a_ab = jnp.where(strict, _mm3(a_, b_, nt=True), 0.0)
        a_ak = jnp.where(strict, _mm3(a_, k_, nt=True), 0.0)
        a_rb = jnp.where(incl, _mm3(r_, b_, nt=True), 0.0)
        a_rk = jnp.where(incl, _mm3(r_, k_, nt=True), 0.0)
        x = eye + a_ab
        pw = a_ab
        for _ in range(5):
            pw = _mm3(pw, pw)
            x = x + _mm3(pw, x)
        pmat = _mm3(x, a_)
        qmat = _mm3(x, _mm3(a_ak, v_))
        yk = _mm3(a_rk, v_)
        s = s_scr[h]
        u = _mm3(pmat, s, nt=True) + qmat
        y = _mm3(r_, s, nt=True) + _mm3(a_rb, u) + yk
        y_scr[:, sl] = y
        s_scr[h] = (s * g_end[:, sl] + _mm3(u.T, bh[:, sl]) + _mm3(v_.T, kh[:, sl]))

    y = y_scr[...]
    inv_n = 1.0 / HEAD_DIM
    mean = headsum(y) * inv_n
    yc = y - mean
    var = headsum(yc * yc) * inv_n
    yn = yc * lax.rsqrt(var + GN_EPS) * lng_ref[...] + lnb_ref[...]
    o_ref[0] = ((yn + bonus) * g).astype(o_ref.dtype)


def _rwkv(p3, mu_rkv, w0, w2p, a0, a2p, g2p, k_k, k_a, r_k, ln_g, ln_b, hsum, n_heads, lora_w):
    b, t, _ = p3.shape
    dr = n_heads * HEAD_DIM
    c = CHUNK
    lora_blk = (6 * dr) // lora_w
    row = lambda n: pl.BlockSpec((1, n), lambda i, j: (0, 0))
    full = lambda a: pl.BlockSpec(a.shape, lambda i, j: (0, 0))
    return pl.pallas_call(
        functools.partial(_rwkv_kernel, n_heads=n_heads),
        out_shape=jax.ShapeDtypeStruct((b, t, dr), BF16),
        grid=(b, t // c),
        in_specs=[pl.BlockSpec((1, c, 3 * dr), lambda i, j: (i, j, 0)),
                  pl.BlockSpec((1, c, lora_w), lambda i, j: (i, j, lora_blk)),
                  row(3 * dr), row(dr), full(w2p), row(dr), full(a2p), full(g2p),
                  row(dr), row(dr), row(dr), row(dr), row(dr), full(hsum)],
        out_specs=pl.BlockSpec((1, c, dr), lambda i, j: (i, j, 0)),
        scratch_shapes=[pltpu.VMEM((n_heads, HEAD_DIM, HEAD_DIM), F32),
                        pltpu.VMEM((1, 3 * dr), F32),
                        pltpu.VMEM((1, lora_w // 2), F32),
                        pltpu.VMEM((c, dr), F32)],
        compiler_params=_cparams(2),
        name="rwkv",
    )(p3, p3, mu_rkv, w0, w2p, a0, a2p, g2p, k_k, k_a, r_k, ln_g, ln_b, hsum)


def _sbprep_kernel(q_ref, k_ref, v_ref, qg_ref, kg_ref, hs_ref, qo_ref, ko_ref, vo_ref):
    hs = hs_ref[...]
    inv_n = 1.0 / HEAD_DIM

    def norm(t, gain):
        ms = _mm2_exact_rhs(t * t, hs) * inv_n
        return t * lax.rsqrt(ms + RMS_EPS) * gain

    qo_ref[0] = (norm(q_ref[0], qg_ref[...]) * (1.0 / math.sqrt(HEAD_DIM))).astype(BF16)
    ko_ref[0] = norm(k_ref[0], kg_ref[...]).astype(BF16)
    vo_ref[0] = v_ref[0].astype(BF16)


def _sbprep(p3, qg, kg, hsum, ds, col0, tt=512):
    b, t, _ = p3.shape
    cb = col0 // ds
    spec = lambda off: pl.BlockSpec((1, tt, ds), lambda i, j: (i, j, cb + off))
    out = jax.ShapeDtypeStruct((b, t, ds), BF16)
    ospec = pl.BlockSpec((1, tt, ds), lambda i, j: (i, j, 0))
    return pl.pallas_call(
        _sbprep_kernel,
        out_shape=(out, out, out),
        grid=(b, t // tt),
        in_specs=[spec(0), spec(1), spec(2),
                  pl.BlockSpec((1, ds), lambda i, j: (0, 0)),
                  pl.BlockSpec((1, ds), lambda i, j: (0, 0)),
                  pl.BlockSpec(hsum.shape, lambda i, j: (0, 0))],
        out_specs=(ospec, ospec, ospec),
        compiler_params=_cparams(2),
        name="sbprep",
    )(p3, p3, p3, qg, kg, hsum)


def _sbattn_kernel(q_ref, k_ref, v_ref, o_ref, *, tq, tk):
    qi = pl.program_id(2)
    q2 = q_ref[0]
    lane = lax.broadcasted_iota(jnp.int32, (tq, LANES), 1)
    first = lane < HEAD_DIM
    zero = jnp.zeros_like(q2)
    qs = (jnp.where(first, q2, zero), jnp.where(first, zero, q2))

    kj = lax.broadcasted_iota(jnp.int32, (tk, 2 * tk), 0)
    ks = lax.broadcasted_iota(jnp.int32, (tk, 2 * tk), 1)
    tri = jnp.where((kj >= ks) | (ks >= tk), 1.0, 0.0).astype(BF16)

    def tile(q, kblk, vblk, carry, mask):
        z = _dot_nt(q, kblk)
        sp = _softplus(z)
        if mask is not None:
            sp = jnp.where(mask, sp, 0.0)
        hi, lo = _split(sp)
        cs = _dot(hi, tri) + _dot(lo, tri)
        log_a = z - cs[:, :tk] - carry
        attn = jnp.exp(log_a)
        if mask is not None:
            attn = jnp.where(mask, attn, 0.0)
        return _dot(attn.astype(BF16), vblk), carry + cs[:, tk:]

    def step(j, state, masks):
        new = []
        for q, (carry, acc) in zip(qs, state):
            for sub in reversed(range(tq // tk)):
                start = pl.multiple_of(j * tq + sub * tk, tk)
                kblk = k_ref[0, pl.ds(start, tk), :]
                vblk = v_ref[0, pl.ds(start, tk), :]
                pv, carry = tile(q, kblk, vblk, carry, None if masks is None else masks[sub])
                acc = acc + pv
            new.append((carry, acc))
        return tuple(new)

    qpos = lax.broadcasted_iota(jnp.int32, (tq, tk), 0)
    kpos = lax.broadcasted_iota(jnp.int32, (tq, tk), 1)
    diag_masks = [kpos + sub * tk < qpos for sub in range(tq // tk)]
    init = tuple((jnp.zeros((tq, tk), F32), jnp.zeros((tq, LANES), F32)) for _ in range(2))
    state = step(qi, init, diag_masks)
    state = lax.fori_loop(0, qi, lambda i, st: step(qi - 1 - i, st, None), state)
    o_ref[0] = jnp.where(first, state[0][1], state[1][1]).astype(o_ref.dtype)


def _sbattn(qn, kn, vn, tq=256, tk=128):
    b, t, ds = qn.shape
    npair = ds // LANES
    return pl.pallas_call(
        functools.partial(_sbattn_kernel, tq=tq, tk=tk),
        out_shape=jax.ShapeDtypeStruct((b, t, ds), BF16),
        grid=(b, npair, t // tq),
        in_specs=[pl.BlockSpec((1, tq, LANES), lambda i, h, j: (i, j, h)),
                  pl.BlockSpec((1, t, LANES), lambda i, h, j: (i, 0, h)),
                  pl.BlockSpec((1, t, LANES), lambda i, h, j: (i, 0, h))],
        out_specs=pl.BlockSpec((1, tq, LANES), lambda i, h, j: (i, j, h)),
        compiler_params=_cparams(3),
        name="sbattn",
    )(qn, kn, vn)


def _outproj_kernel(yr_ref, ys_ref, w_ref, x_ref, gt_ref, g2_ref, sc_ref, sh_ref, x1_ref, h2_ref):
    dr = yr_ref.shape[1]
    mix = _dot(yr_ref[...], w_ref[:dr, :]) + _dot(ys_ref[...], w_ref[dr:, :])
    x1 = x_ref[...] + gt_ref[0] * mix
    x1_ref[...] = x1
    ms = jnp.mean(x1 * x1, axis=-1, keepdims=True)
    y = x1 * lax.rsqrt(ms + RMS_EPS) * g2_ref[...]
    h2_ref[...] = (y * (1.0 + sc_ref[0]) + sh_ref[0]).astype(BF16)


def _outproj(yr, ys, w_out, x2, gt1, gain2, sc2, sh2, seq, tm=512):
    m, d = x2.shape
    dr = yr.shape[1]
    ds = ys.shape[1]
    per_seq = seq // tm
    mod = pl.BlockSpec((1, 1, d), lambda i: (i // per_seq, 0, 0))
    return pl.pallas_call(
        _outproj_kernel,
        out_shape=(jax.ShapeDtypeStruct((m, d), F32), jax.ShapeDtypeStruct((m, d), BF16)),
        grid=(m // tm,),
        in_specs=[pl.BlockSpec((tm, dr), lambda i: (i, 0)),
                  pl.BlockSpec((tm, ds), lambda i: (i, 0)),
                  pl.BlockSpec(w_out.shape, lambda i: (0, 0)),
                  pl.BlockSpec((tm, d), lambda i: (i, 0)),
                  mod,
                  pl.BlockSpec((1, d), lambda i: (0, 0)),
                  mod, mod],
        out_specs=(pl.BlockSpec((tm, d), lambda i: (i, 0)), pl.BlockSpec((tm, d), lambda i: (i, 0))),
        compiler_params=_cparams(1),
        name="outproj",
    )(yr, ys, w_out, x2, gt1, gain2, sc2, sh2)


def _ffn_kernel(h_ref, wg_ref, wu_ref, wd_ref, x_ref, gt_ref, o_ref, acc_scr):
    j = pl.program_id(1)

    @pl.when(j == 0)
    def _():
        acc_scr[...] = jnp.zeros_like(acc_scr)

    h = h_ref[...]
    gate = _dot(h, wg_ref[...])
    up = _dot(h, wu_ref[...])
    act = (gate * _sigmoid(gate) * up).astype(BF16)
    acc_scr[...] += _dot(act, wd_ref[...])

    @pl.when(j == pl.num_programs(1) - 1)
    def _():
        o_ref[...] = x_ref[...] + gt_ref[0] * acc_scr[...]


def _ffn(h2, w_gu, w_down, x1, gt2, seq, tm=512, tf=512):
    m, d = h2.shape
    dff = w_down.shape[0]
    nf = dff // tf
    per_seq = seq // tm
    return pl.pallas_call(
        _ffn_kernel,
        out_shape=jax.ShapeDtypeStruct((m, d), F32),
        grid=(m // tm, nf),
        in_specs=[pl.BlockSpec((tm, d), lambda i, j: (i, 0)),
                  pl.BlockSpec((d, tf), lambda i, j: (0, j)),
                  pl.BlockSpec((d, tf), lambda i, j: (0, j + nf)),
                  pl.BlockSpec((tf, d), lambda i, j: (j, 0)),
                  pl.BlockSpec((tm, d), lambda i, j: (i, 0)),
                  pl.BlockSpec((1, 1, d), lambda i, j: (i // per_seq, 0, 0))],
        out_specs=pl.BlockSpec((tm, d), lambda i, j: (i, 0)),
        scratch_shapes=[pltpu.VMEM((tm, d), F32)],
        compiler_params=_cparams(2),
        name="ffn",
    )(h2, w_gu, w_gu, w_down, x1, gt2)


def _pad_to(a, n, axis):
    pad = [(0, 0)] * a.ndim
    pad[axis] = (0, n - a.shape[axis])
    return jnp.pad(a, pad)


def _layer(x, c_pad, w_ada, b_ada, norm1_gain, norm2_gain, w_in, mu_rkv, mu_w, mu_a, mu_g, w0, w1,
           w2, a0, a1, a2, g1, g2, k_k, k_a, r_k, ln_x_gain, ln_x_bias, q_norm_gain, k_norm_gain,
           w_out, w_gate_up, w_down):
    b, t, d = x.shape
    dr = w0.shape[0]
    ds = d - dr
    n_rwkv = dr // HEAD_DIM
    n_sb = ds // HEAD_DIM
    row = lambda a: a.reshape(1, -1)

    mod = _ada(c_pad, w_ada, row(b_ada))[:b]
    sh1, sc1, gt1, sh2, sc2, gt2 = [m.reshape(b, 1, d) for m in jnp.split(mod, 6, axis=-1)]

    nw = -(-w1.shape[1] // LANES) * LANES
    na = -(-a1.shape[1] // LANES) * LANES
    ng = -(-g1.shape[1] // LANES) * LANES
    w_l = jnp.concatenate([_pad_to(w1, nw, 1), _pad_to(a1, na, 1), _pad_to(g1, ng, 1)], axis=1)
    mu_l = jnp.concatenate([jnp.broadcast_to(mu_w[:, None], (d, nw)),
                            jnp.broadcast_to(mu_a[:, None], (d, na)),
                            jnp.broadcast_to(mu_g[:, None], (d, ng))], axis=1)
    w_lora = _fold(w_l, mu_l)
    lora_w = w_lora.shape[1]
    w_ext = jnp.concatenate([w_in.astype(BF16), w_lora], axis=1)

    x2 = x.reshape(b * t, d)
    p = _inproj(x2, row(norm1_gain), sc1, sh1, w_ext, t)
    p3 = p.reshape(b, t, -1)

    hsum_r = jnp.kron(jnp.eye(n_rwkv, dtype=F32), jnp.ones((HEAD_DIM, HEAD_DIM), F32)).astype(BF16)
    y_rwkv = _rwkv(p3, row(mu_rkv), row(w0), _pad_to(w2, nw, 0).astype(BF16), row(a0),
                   _pad_to(a2, na, 0).astype(BF16), _pad_to(g2, ng, 0).astype(BF16),
                   row(k_k), row(k_a), row(r_k), row(ln_x_gain), row(ln_x_bias), hsum_r,
                   n_rwkv, lora_w)

    hsum_s = jnp.kron(jnp.eye(n_sb, dtype=F32), jnp.ones((HEAD_DIM, HEAD_DIM), F32)).astype(BF16)
    qn, kn, vn = _sbprep(p3, row(jnp.tile(q_norm_gain, n_sb)), row(jnp.tile(k_norm_gain, n_sb)),
                         hsum_s, ds, 3 * dr)
    y_sb = _sbattn(qn, kn, vn)

    x1, h2 = _outproj(y_rwkv.reshape(b * t, dr), y_sb.reshape(b * t, ds), w_out.astype(BF16), x2,
                      gt1, row(norm2_gain), sc2, sh2, t)
    out = _ffn(h2, w_gate_up.astype(BF16), w_down.astype(BF16), x1, gt2, t)
    return out.reshape(b, t, d)


def kernel(x, c, w_ada, b_ada, norm1_gain, norm2_gain, w_in, mu_rkv, mu_w, mu_a, mu_g, w0, w1, w2,
           a0, a1, a2, g1, g2, k_k, k_a, r_k, ln_x_gain, ln_x_bias, q_norm_gain, k_norm_gain, w_out,
           w_gate_up, w_down):
    depth = w_ada.shape[0]
    c_pad = _pad_to(c, 8, 0)
    for l in range(depth):
        x = _layer(x, c_pad, w_ada[l], b_ada[l], norm1_gain[l], norm2_gain[l], w_in[l], mu_rkv[l],
                   mu_w[l], mu_a[l], mu_g[l], w0[l], w1[l], w2[l], a0[l], a1[l], a2[l], g1[l],
                   g2[l], k_k[l], k_a[l], r_k[l].reshape(-1), ln_x_gain[l], ln_x_bias[l],
                   q_norm_gain[l], k_norm_gain[l], w_out[l], w_gate_up[l], w_down[l])
    return x
```

```python
import functools
import math

import jax
import jax.numpy as jnp
from jax import lax
from jax.experimental import pallas as pl
from jax.experimental.pallas import tpu as pltpu

F32 = jnp.float32
BF16 = jnp.bfloat16

HEAD_DIM = 64
RMS_EPS = 1e-6
GN_EPS = 64e-5
L2_EPS = 1e-12
LANES = 128
MXU_N = 256
CHUNK = 64
VMEM_LIMIT = 48 * 1024 * 1024


def _cparams(n_axes):
    return pltpu.CompilerParams(dimension_semantics=("arbitrary",) * n_axes,
                                vmem_limit_bytes=VMEM_LIMIT)


def _sigmoid(x):
    return 1.0 / (1.0 + jnp.exp(-x))


def _softplus(x):
    return jnp.maximum(x, 0.0) + jnp.log(1.0 + jnp.exp(-jnp.abs(x)))


def _dot(x, y):
    return jnp.dot(x, y, preferred_element_type=F32)


def _dot_nt(x, y):
    return lax.dot_general(x, y, (((1,), (1,)), ((), ())), preferred_element_type=F32)


def _split(x):
    hi = x.astype(BF16)
    lo = (x - hi.astype(F32)).astype(BF16)
    return hi, lo


def _mm3(x, y, nt=False):
    d = _dot_nt if nt else _dot
    xh, xl = _split(x)
    yh, yl = _split(y)
    return d(xh, yh) + (d(xh, yl) + d(xl, yh))


def _mm(x, y, passes, nt=False):
    d = _dot_nt if nt else _dot
    if passes == 1:
        return d(x.astype(BF16), y.astype(BF16))
    xh, xl = _split(x)
    if passes == 2:
        yh = y.astype(BF16)
        return d(xh, yh) + d(xl, yh)
    yh, yl = _split(y)
    return d(xh, yh) + (d(xh, yl) + d(xl, yh))


_PASSES = {"pair": 1, "akv": 1, "solve": 1, "state_read": 1, "rbu": 1, "state_write": 1}


def _headsum(t, hs):
    g = hs.shape[0]
    return jnp.concatenate([_mm2_exact_rhs(t[:, i:i + g], hs) for i in range(0, t.shape[1], g)],
                           axis=1)


def _mm2_exact_rhs(x, y_bf16):
    xh, xl = _split(x)
    return _dot(xh, y_bf16) + _dot(xl, y_bf16)


def _ada_kernel(c_ref, w_ref, b_ref, o_ref):
    c = c_ref[...]
    ca = c * _sigmoid(c)
    o_ref[...] = _dot(ca.astype(BF16), w_ref[...].astype(BF16)) + b_ref[...]


def _ada(c_pad, w_ada, b_ada, tn=1024):
    m, d = c_pad.shape
    n = w_ada.shape[1]
    return pl.pallas_call(
        _ada_kernel,
        out_shape=jax.ShapeDtypeStruct((m, n), F32),
        grid=(n // tn,),
        in_specs=[pl.BlockSpec((m, d), lambda j: (0, 0)),
                  pl.BlockSpec((d, tn), lambda j: (0, j)),
                  pl.BlockSpec((1, tn), lambda j: (0, j))],
        out_specs=pl.BlockSpec((m, tn), lambda j: (0, j)),
        compiler_params=_cparams(1),
        name="ada",
    )(c_pad, w_ada, b_ada)


def _fold_kernel(w_ref, mu_ref, o_ref):
    w = w_ref[...]
    mu = mu_ref[...]
    half = w.shape[1]
    o_ref[:, :half] = (w * (1.0 - mu)).astype(BF16)
    o_ref[:, half:] = (w * mu).astype(BF16)


def _fold(w_l, mu_l):
    d, half = w_l.shape
    return pl.pallas_call(
        _fold_kernel,
        out_shape=jax.ShapeDtypeStruct((d, 2 * half), BF16),
        compiler_params=pltpu.CompilerParams(vmem_limit_bytes=VMEM_LIMIT),
        name="fold",
    )(w_l, mu_l)


def _inproj_kernel(x_ref, g_ref, sc_ref, sh_ref, w_ref, o_ref, h_scr):
    @pl.when(pl.program_id(1) == 0)
    def _():
        x = x_ref[...]
        ms = jnp.mean(x * x, axis=-1, keepdims=True)
        y = x * lax.rsqrt(ms + RMS_EPS) * g_ref[...]
        h_scr[...] = (y * (1.0 + sc_ref[0]) + sh_ref[0]).astype(BF16)

    o_ref[...] = _dot(h_scr[...], w_ref[...])


def _inproj(x2, gain, sc, sh, w_ext, seq, tm=512, tn=1024):
    m, d = x2.shape
    n = w_ext.shape[1]
    per_seq = seq // tm
    return pl.pallas_call(
        _inproj_kernel,
        out_shape=jax.ShapeDtypeStruct((m, n), F32),
        grid=(m // tm, n // tn),
        in_specs=[pl.BlockSpec((tm, d), lambda i, j: (i, 0)),
                  pl.BlockSpec((1, d), lambda i, j: (0, 0)),
                  pl.BlockSpec((1, 1, d), lambda i, j: (i // per_seq, 0, 0)),
                  pl.BlockSpec((1, 1, d), lambda i, j: (i // per_seq, 0, 0)),
                  pl.BlockSpec((d, tn), lambda i, j: (0, j))],
        out_specs=pl.BlockSpec((tm, tn), lambda i, j: (i, j)),
        scratch_shapes=[pltpu.VMEM((tm, d), BF16)],
        compiler_params=_cparams(2),
        name="inproj",
    )(x2, gain, sc, sh, w_ext)


def _shift_rows(cur, prev_row):
    rolled = pltpu.roll(cur, 1, axis=0)
    row = lax.broadcasted_iota(jnp.int32, cur.shape, 0)
    return jnp.where(row == 0, jnp.broadcast_to(prev_row, cur.shape), rolled)


def _rwkv_kernel(p_ref, l_ref, mu_ref, w0_ref, w2_ref, a0_ref, a2_ref, g2_ref, kk_ref, ka_ref,
                 rk_ref, lng_ref, lnb_ref, hs_ref, o_ref, s_scr, prevp_scr, prevl_scr, y_scr,
                 *, n_heads):
    dr = n_heads * HEAD_DIM
    c = CHUNK

    @pl.when(pl.program_id(1) == 0)
    def _():
        s_scr[...] = jnp.zeros_like(s_scr)
        prevp_scr[...] = jnp.zeros_like(prevp_scr)
        prevl_scr[...] = jnp.zeros_like(prevl_scr)

    headsum = functools.partial(_headsum, hs=hs_ref[...])

    p = p_ref[0]
    pshift = _shift_rows(p, prevp_scr[...])
    prevp_scr[...] = p[c - 1:c, :]
    pm = p + (pshift - p) * mu_ref[...]
    r = pm[:, :dr]
    k = pm[:, dr:2 * dr]
    v = pm[:, 2 * dr:]

    l = l_ref[0]
    half = l.shape[1] // 2
    lb = l[:, half:]
    lin = l[:, :half] + _shift_rows(lb, prevl_scr[...])
    prevl_scr[...] = lb[c - 1:c, :]
    nw = w2_ref.shape[0]
    na = a2_ref.shape[0]
    lw = jnp.tanh(lin[:, :nw])
    la = lin[:, nw:nw + na]
    lg = _sigmoid(lin[:, nw + na:])

    w_pre = w0_ref[...] + _dot(lw.astype(BF16), w2_ref[...])
    w_log = -_softplus(-w_pre) - 0.5
    logw = -jnp.exp(w_log)
    a_gate = _sigmoid(a0_ref[...] + _dot(la.astype(BF16), a2_ref[...]))
    g = _dot(lg.astype(BF16), g2_ref[...])

    kk = k * kk_ref[...]
    kk = kk * lax.rsqrt(headsum(kk * kk) + L2_EPS)
    k2 = k * (1.0 + (a_gate - 1.0) * ka_ref[...])
    avec = -kk
    bvec = kk * a_gate
    bonus = headsum(r * k2 * rk_ref[...]) * v

    ti = lax.broadcasted_iota(jnp.int32, (c, c), 0)
    si = lax.broadcasted_iota(jnp.int32, (c, c), 1)
    strict = ti > si
    incl = ti >= si
    tri = jnp.where(incl, 1.0, 0.0).astype(BF16)
    l1, l2 = _split(logw)
    l3 = (logw - l1.astype(F32) - l2.astype(F32)).astype(BF16)
    lgc = _dot(tri, l1) + (_dot(tri, l2) + _dot(tri, l3))
    lg_end = lgc[c - 1:c, :]
    at = avec * jnp.exp(lgc - logw)
    ginv = jnp.exp(-lgc)
    bt = bvec * ginv
    kt = k2 * ginv
    rt = r * jnp.exp(lgc)
    gend = jnp.exp(lg_end - lgc)
    bh = bvec * gend
    kh = k2 * gend
    g_end = jnp.exp(lg_end)

    d = HEAD_DIM
    heads = range(n_heads)
    sls = [slice(h * d, (h + 1) * d) for h in heads]
    rows2 = lax.broadcasted_iota(jnp.int32, (2 * c, 2 * c), 0)
    cols2 = lax.broadcasted_iota(jnp.int32, (2 * c, 2 * c), 1)
    tt = jnp.where(rows2 >= c, rows2 - c, rows2)
    ii = jnp.where(cols2 >= c, cols2 - c, cols2)
    quad_mask = tt + jnp.where(rows2 >= c, 1, 0) > ii
    keep_z = lax.broadcasted_iota(jnp.int32, (c, 4 * d), 1) >= 2 * d

    ar = [jnp.concatenate([at[:, s], rt[:, s]], axis=0) for s in sls]
    bk = [jnp.concatenate([bt[:, s], kt[:, s]], axis=0) for s in sls]
    m4 = [jnp.where(quad_mask, _mm(ar[h], bk[h], _PASSES["pair"], nt=True), 0.0) for h in heads]
    akv_yk = [_mm(m4[h][:, c:], v[:, sls[h]], _PASSES["akv"]) for h in heads]
    zpad = jnp.zeros((c, d), F32)
    w = [jnp.concatenate([m4[h][:c, :c], zpad, at[:, sls[h]], akv_yk[h][:c]], axis=1)
         for h in heads]
    for lvl in range(6):
        rhs = w if lvl < 5 else [wh[:, 2 * d:] for wh in w]
        prod = [_mm(w[h][:, :d], rhs[h], _PASSES["solve"]) for h in heads]
        if lvl < 5:
            w = [prod[h] + jnp.where(keep_z, w[h], 0.0) for h in heads]
        else:
            w = [prod[h] + w[h][:, 2 * d:] for h in heads]
    s_old = [s_scr[h] for h in heads]
    su = [_mm(jnp.concatenate([w[h][:, :d], rt[:, sls[h]]], axis=0), s_old[h], _PASSES["state_read"],
              nt=True) for h in heads]
    u = [su[h][:c] + w[h][:, d:] for h in heads]
    rbu = [_mm(m4[h][c:, :c], u[h], _PASSES["rbu"]) for h in heads]
    upd = [_mm(jnp.concatenate([u[h], v[:, sls[h]]], axis=0).T,
               jnp.concatenate([bh[:, sls[h]], kh[:, sls[h]]], axis=0), _PASSES["state_write"])
           for h in heads]
    for h in heads:
        y_scr[:, sls[h]] = su[h][c:] + rbu[h] + akv_yk[h][c:]
        s_scr[h] = s_old[h] * g_end[:, sls[h]] + upd[h]

    y = y_scr[...]
    inv_n = 1.0 / HEAD_DIM
    mean = headsum(y) * inv_n
    yc = y - mean
    var = headsum(yc * yc) * inv_n
    yn = yc * lax.rsqrt(var + GN_EPS) * lng_ref[...] + lnb_ref[...]
    o_ref[0] = ((yn + bonus) * g).astype(o_ref.dtype)


def _rwkv(p3, mu_rkv, w0, w2p, a0, a2p, g2p, k_k, k_a, r_k, ln_g, ln_b, hsum, n_heads, lora_w):
    b, t, _ = p3.shape
    dr = n_heads * HEAD_DIM
    c = CHUNK
    lora_blk = (6 * dr) // lora_w
    row = lambda n: pl.BlockSpec((1, n), lambda i, j: (0, 0))
    full = lambda a: pl.BlockSpec(a.shape, lambda i, j: (0, 0))
    return pl.pallas_call(
        functools.partial(_rwkv_kernel, n_heads=n_heads),
        out_shape=jax.ShapeDtypeStruct((b, t, dr), BF16),
        grid=(b, t // c),
        in_specs=[pl.BlockSpec((1, c, 3 * dr), lambda i, j: (i, j, 0)),
                  pl.BlockSpec((1, c, lora_w), lambda i, j: (i, j, lora_blk)),
                  row(3 * dr), row(dr), full(w2p), row(dr), full(a2p), full(g2p),
                  row(dr), row(dr), row(dr), row(dr), row(dr), full(hsum)],
        out_specs=pl.BlockSpec((1, c, dr), lambda i, j: (i, j, 0)),
        scratch_shapes=[pltpu.VMEM((n_heads, HEAD_DIM, HEAD_DIM), F32),
                        pltpu.VMEM((1, 3 * dr), F32),
                        pltpu.VMEM((1, lora_w // 2), F32),
                        pltpu.VMEM((c, dr), F32)],
        compiler_params=_cparams(2),
        name="rwkv",
    )(p3, p3, mu_rkv, w0, w2p, a0, a2p, g2p, k_k, k_a, r_k, ln_g, ln_b, hsum)


def _sbprep_kernel(q_ref, k_ref, v_ref, qg_ref, kg_ref, hs_ref, qo_ref, ko_ref, vo_ref):
    hs = hs_ref[...]
    inv_n = 1.0 / HEAD_DIM

    def norm(t, gain):
        ms = _headsum(t * t, hs) * inv_n
        return t * lax.rsqrt(ms + RMS_EPS) * gain

    qo_ref[0] = (norm(q_ref[0], qg_ref[...]) * (1.0 / math.sqrt(HEAD_DIM))).astype(BF16)
    ko_ref[0] = norm(k_ref[0], kg_ref[...]).astype(BF16)
    vo_ref[0] = v_ref[0].astype(BF16)


def _sbprep(p3, qg, kg, hsum, ds, col0, tt=512):
    b, t, _ = p3.shape
    cb = col0 // ds
    spec = lambda off: pl.BlockSpec((1, tt, ds), lambda i, j: (i, j, cb + off))
    out = jax.ShapeDtypeStruct((b, t, ds), BF16)
    ospec = pl.BlockSpec((1, tt, ds), lambda i, j: (i, j, 0))
    return pl.pallas_call(
        _sbprep_kernel,
        out_shape=(out, out, out),
        grid=(b, t // tt),
        in_specs=[spec(0), spec(1), spec(2),
                  pl.BlockSpec((1, ds), lambda i, j: (0, 0)),
                  pl.BlockSpec((1, ds), lambda i, j: (0, 0)),
                  pl.BlockSpec(hsum.shape, lambda i, j: (0, 0))],
        out_specs=(ospec, ospec, ospec),
        compiler_params=_cparams(2),
        name="sbprep",
    )(p3, p3, p3, qg, kg, hsum)


def _sbattn_kernel(q_ref, k_ref, v_ref, o_ref, *, tq, tk, nb):
    qi = pl.program_id(2)
    nsub = tq // tk
    q2 = q_ref[0]
    first = lax.broadcasted_iota(jnp.int32, (tk, LANES), 1) < HEAD_DIM
    rr = lax.broadcasted_iota(jnp.int32, (2 * tk, 2 * tk), 0)
    cc = lax.broadcasted_iota(jnp.int32, (2 * tk, 2 * tk), 1)
    tri2 = jnp.where((rr >= cc) & ((rr < tk) == (cc < tk)), 1.0, 0.0).astype(BF16)

    def per_head_rows(blk):
        zero = jnp.zeros_like(blk)
        return jnp.concatenate([jnp.where(first, blk, zero), jnp.where(first, zero, blk)], axis=0)

    def blocks(starts, carry, acc, masks):
        kcs = [per_head_rows(k_ref[0, pl.ds(s, tk), :]) for s in starts]
        vcs = [per_head_rows(v_ref[0, pl.ds(s, tk), :]) for s in starts]
        zs = [_dot_nt(q2, kc) for kc in kcs]
        sps = []
        for z, m in zip(zs, masks):
            sp = _softplus(z)
            if m is not None:
                sp = jnp.where(m, sp, 0.0)
            sps.append(sp.astype(BF16))
        css = [_dot(sp, tri2) for sp in sps]
        pvs = []
        for z, cs, m, vc in zip(zs, css, masks, vcs):
            attn = jnp.exp(z - cs - carry)
            if m is not None:
                attn = jnp.where(m, attn, 0.0)
            pvs.append(_dot(attn.astype(BF16), vc))
            tot = jnp.concatenate([jnp.broadcast_to(cs[:, 0:1], (tq, tk)),
                                   jnp.broadcast_to(cs[:, tk:tk + 1], (tq, tk))], axis=1)
            carry = carry + tot
        for pv in pvs:
            acc = acc + pv
        return carry, acc

    qpos = lax.broadcasted_iota(jnp.int32, (tq, 2 * tk), 0)
    col = lax.broadcasted_iota(jnp.int32, (tq, 2 * tk), 1)
    kpos = jnp.where(col >= tk, col - tk, col)
    carry = jnp.zeros((tq, 2 * tk), F32)
    acc = jnp.zeros((tq, LANES), F32)
    for g in range(nsub // nb):
        subs = [nsub - 1 - g * nb - u for u in range(nb)]
        starts = [pl.multiple_of(qi * tq + sub * tk, tk) for sub in subs]
        carry, acc = blocks(starts, carry, acc, [kpos + sub * tk < qpos for sub in subs])

    def body(i, st):
        base = qi * tq - (i + 1) * (nb * tk)
        starts = [pl.multiple_of(base + (nb - 1 - u) * tk, tk) for u in range(nb)]
        return blocks(starts, st[0], st[1], [None] * nb)

    carry, acc = lax.fori_loop(0, qi * (nsub // nb), body, (carry, acc))
    o_ref[0] = acc.astype(o_ref.dtype)


def _sbattn(qn, kn, vn, tq=512, tk=128, nb=4):
    b, t, ds = qn.shape
    npair = ds // LANES
    return pl.pallas_call(
        functools.partial(_sbattn_kernel, tq=tq, tk=tk, nb=nb),
        out_shape=jax.ShapeDtypeStruct((b, t, ds), BF16),
        grid=(b, npair, t // tq),
        in_specs=[pl.BlockSpec((1, tq, LANES), lambda i, h, j: (i, j, h)),
                  pl.BlockSpec((1, t, LANES), lambda i, h, j: (i, 0, h)),
                  pl.BlockSpec((1, t, LANES), lambda i, h, j: (i, 0, h))],
        out_specs=pl.BlockSpec((1, tq, LANES), lambda i, h, j: (i, j, h)),
        compiler_params=_cparams(3),
        name="sbattn",
    )(qn, kn, vn)


def _outproj_kernel(yr_ref, ys_ref, w_ref, x_ref, gt_ref, g2_ref, sc_ref, sh_ref, x1_ref, h2_ref):
    dr = yr_ref.shape[1]
    mix = _dot(yr_ref[...], w_ref[:dr, :]) + _dot(ys_ref[...], w_ref[dr:, :])
    x1 = x_ref[...] + gt_ref[0] * mix
    x1_ref[...] = x1
    ms = jnp.mean(x1 * x1, axis=-1, keepdims=True)
    y = x1 * lax.rsqrt(ms + RMS_EPS) * g2_ref[...]
    h2_ref[...] = (y * (1.0 + sc_ref[0]) + sh_ref[0]).astype(BF16)


def _outproj(yr, ys, w_out, x2, gt1, gain2, sc2, sh2, seq, tm=512):
    m, d = x2.shape
    dr = yr.shape[1]
    ds = ys.shape[1]
    per_seq = seq // tm
    mod = pl.BlockSpec((1, 1, d), lambda i: (i // per_seq, 0, 0))
    return pl.pallas_call(
        _outproj_kernel,
        out_shape=(jax.ShapeDtypeStruct((m, d), F32), jax.ShapeDtypeStruct((m, d), BF16)),
        grid=(m // tm,),
        in_specs=[pl.BlockSpec((tm, dr), lambda i: (i, 0)),
                  pl.BlockSpec((tm, ds), lambda i: (i, 0)),
                  pl.BlockSpec(w_out.shape, lambda i: (0, 0)),
                  pl.BlockSpec((tm, d), lambda i: (i, 0)),
                  mod,
                  pl.BlockSpec((1, d), lambda i: (0, 0)),
                  mod, mod],
        out_specs=(pl.BlockSpec((tm, d), lambda i: (i, 0)), pl.BlockSpec((tm, d), lambda i: (i, 0))),
        compiler_params=_cparams(1),
        name="outproj",
    )(yr, ys, w_out, x2, gt1, gain2, sc2, sh2)


def _ffn_kernel(h_ref, wg_ref, wu_ref, wd_ref, x_ref, gt_ref, o_ref, acc_scr):
    j = pl.program_id(1)

    @pl.when(j == 0)
    def _():
        acc_scr[...] = jnp.zeros_like(acc_scr)

    h = h_ref[...]
    gate = _dot(h, wg_ref[...])
    up = _dot(h, wu_ref[...])
    act = (gate * _sigmoid(gate) * up).astype(BF16)
    acc_scr[...] += _dot(act, wd_ref[...])

    @pl.when(j == pl.num_programs(1) - 1)
    def _():
        o_ref[...] = x_ref[...] + gt_ref[0] * acc_scr[...]


def _ffn(h2, w_gu, w_down, x1, gt2, seq, tm=512, tf=512):
    m, d = h2.shape
    dff = w_down.shape[0]
    nf = dff // tf
    per_seq = seq // tm
    return pl.pallas_call(
        _ffn_kernel,
        out_shape=jax.ShapeDtypeStruct((m, d), F32),
        grid=(m // tm, nf),
        in_specs=[pl.BlockSpec((tm, d), lambda i, j: (i, 0)),
                  pl.BlockSpec((d, tf), lambda i, j: (0, j)),
                  pl.BlockSpec((d, tf), lambda i, j: (0, j + nf)),
                  pl.BlockSpec((tf, d), lambda i, j: (j, 0)),
                  pl.BlockSpec((tm, d), lambda i, j: (i, 0)),
                  pl.BlockSpec((1, 1, d), lambda i, j: (i // per_seq, 0, 0))],
        out_specs=pl.BlockSpec((tm, d), lambda i, j: (i, 0)),
        scratch_shapes=[pltpu.VMEM((tm, d), F32)],
        compiler_params=_cparams(2),
        name="ffn",
    )(h2, w_gu, w_gu, w_down, x1, gt2)


def _pad_to(a, n, axis):
    pad = [(0, 0)] * a.ndim
    pad[axis] = (0, n - a.shape[axis])
    return jnp.pad(a, pad)


def _layer(x, c_pad, w_ada, b_ada, norm1_gain, norm2_gain, w_in, mu_rkv, mu_w, mu_a, mu_g, w0, w1,
           w2, a0, a1, a2, g1, g2, k_k, k_a, r_k, ln_x_gain, ln_x_bias, q_norm_gain, k_norm_gain,
           w_out, w_gate_up, w_down):
    b, t, d = x.shape
    dr = w0.shape[0]
    ds = d - dr
    n_rwkv = dr // HEAD_DIM
    n_sb = ds // HEAD_DIM
    row = lambda a: a.reshape(1, -1)

    mod = _ada(c_pad, w_ada, row(b_ada))[:b]
    sh1, sc1, gt1, sh2, sc2, gt2 = [m.reshape(b, 1, d) for m in jnp.split(mod, 6, axis=-1)]

    nw = -(-w1.shape[1] // LANES) * LANES
    na = -(-a1.shape[1] // LANES) * LANES
    ng = -(-g1.shape[1] // LANES) * LANES
    w_l = jnp.concatenate([_pad_to(w1, nw, 1), _pad_to(a1, na, 1), _pad_to(g1, ng, 1)], axis=1)
    mu_l = jnp.concatenate([jnp.broadcast_to(mu_w[:, None], (d, nw)),
                            jnp.broadcast_to(mu_a[:, None], (d, na)),
                            jnp.broadcast_to(mu_g[:, None], (d, ng))], axis=1)
    w_lora = _fold(w_l, mu_l)
    lora_w = w_lora.shape[1]
    w_ext = jnp.concatenate([w_in.astype(BF16), w_lora], axis=1)

    x2 = x.reshape(b * t, d)
    p = _inproj(x2, row(norm1_gain), sc1, sh1, w_ext, t)
    p3 = p.reshape(b, t, -1)

    hsum = jnp.kron(jnp.eye(MXU_N // HEAD_DIM, dtype=F32),
                    jnp.ones((HEAD_DIM, HEAD_DIM), F32)).astype(BF16)
    y_rwkv = _rwkv(p3, row(mu_rkv), row(w0), _pad_to(w2, nw, 0).astype(BF16), row(a0),
                   _pad_to(a2, na, 0).astype(BF16), _pad_to(g2, ng, 0).astype(BF16),
                   row(k_k), row(k_a), row(r_k), row(ln_x_gain), row(ln_x_bias), hsum,
                   n_rwkv, lora_w)

    qn, kn, vn = _sbprep(p3, row(jnp.tile(q_norm_gain, n_sb)), row(jnp.tile(k_norm_gain, n_sb)),
                         hsum, ds, 3 * dr)
    y_sb = _sbattn(qn, kn, vn)

    x1, h2 = _outproj(y_rwkv.reshape(b * t, dr), y_sb.reshape(b * t, ds), w_out.astype(BF16), x2,
                      gt1, row(norm2_gain), sc2, sh2, t)
    out = _ffn(h2, w_gate_up.astype(BF16), w_down.astype(BF16), x1, gt2, t)
    return out.reshape(b, t, d)


def kernel(x, c, w_ada, b_ada, norm1_gain, norm2_gain, w_in, mu_rkv, mu_w, mu_a, mu_g, w0, w1, w2,
           a0, a1, a2, g1, g2, k_k, k_a, r_k, ln_x_gain, ln_x_bias, q_norm_gain, k_norm_gain, w_out,
           w_gate_up, w_down):
    depth = w_ada.shape[0]
    c_pad = _pad_to(c, 8, 0)
    for l in range(depth):
        x = _layer(x, c_pad, w_ada[l], b_ada[l], norm1_gain[l], norm2_gain[l], w_in[l], mu_rkv[l],
                   mu_w[l], mu_a[l], mu_g[l], w0[l], w1[l], w2[l], a0[l], a1[l], a2[l], g1[l],
                   g2[l], k_k[l], k_a[l], r_k[l].reshape(-1), ln_x_gain[l], ln_x_bias[l],
                   q_norm_gain[l], k_norm_gain[l], w_out[l], w_gate_up[l], w_down[l])
    return x
```

```python
import functools
import math

import jax
import jax.numpy as jnp
from jax import lax
from jax.experimental import pallas as pl
from jax.experimental.pallas import tpu as pltpu

F32 = jnp.float32
BF16 = jnp.bfloat16

HEAD_DIM = 64
RMS_EPS = 1e-6
GN_EPS = 64e-5
L2_EPS = 1e-12
LOG2E = math.log2(math.e)
LANES = 128
MXU_N = 256
CHUNK = 64
VMEM_LIMIT = 48 * 1024 * 1024
VMEM_LIMIT_BIG = 56 * 1024 * 1024


def _cparams(n_axes, vmem_limit=VMEM_LIMIT):
    return pltpu.CompilerParams(dimension_semantics=("arbitrary",) * n_axes,
                                vmem_limit_bytes=vmem_limit)


def _sigmoid(x):
    return 1.0 / (1.0 + jnp.exp(-x))


def _softplus(x):
    return jnp.maximum(x, 0.0) + jnp.log(1.0 + jnp.exp(-jnp.abs(x)))


def _dot(x, y):
    return jnp.dot(x, y, preferred_element_type=F32)


def _dot_nt(x, y):
    return lax.dot_general(x, y, (((1,), (1,)), ((), ())), preferred_element_type=F32)


def _split(x):
    hi = x.astype(BF16)
    lo = (x - hi.astype(F32)).astype(BF16)
    return hi, lo


def _mm3(x, y, nt=False):
    d = _dot_nt if nt else _dot
    xh, xl = _split(x)
    yh, yl = _split(y)
    return d(xh, yh) + (d(xh, yl) + d(xl, yh))


def _mm(x, y, passes, nt=False):
    d = _dot_nt if nt else _dot
    if passes == 1:
        return d(x.astype(BF16), y.astype(BF16))
    xh, xl = _split(x)
    if passes == 2:
        yh = y.astype(BF16)
        return d(xh, yh) + d(xl, yh)
    yh, yl = _split(y)
    return d(xh, yh) + (d(xh, yl) + d(xl, yh))


_PASSES = {"pair": 1, "akv": 1, "solve": 1, "state_read": 1, "rbu": 1, "state_write": 1}


def _headsum(t, hs):
    g = hs.shape[0]
    return jnp.concatenate([_mm2_exact_rhs(t[:, i:i + g], hs) for i in range(0, t.shape[1], g)],
                           axis=1)


def _mm2_exact_rhs(x, y_bf16):
    xh, xl = _split(x)
    return _dot(xh, y_bf16) + _dot(xl, y_bf16)


def _ada_kernel(c_ref, w_ref, b_ref, o_ref):
    c = c_ref[...]
    ca = c * _sigmoid(c)
    o_ref[...] = _dot(ca.astype(BF16), w_ref[...].astype(BF16)) + b_ref[...]


def _ada(c_pad, w_ada, b_ada, tn=1024):
    m, d = c_pad.shape
    n = w_ada.shape[1]
    return pl.pallas_call(
        _ada_kernel,
        out_shape=jax.ShapeDtypeStruct((m, n), F32),
        grid=(n // tn,),
        in_specs=[pl.BlockSpec((m, d), lambda j: (0, 0)),
                  pl.BlockSpec((d, tn), lambda j: (0, j)),
                  pl.BlockSpec((1, tn), lambda j: (0, j))],
        out_specs=pl.BlockSpec((m, tn), lambda j: (0, j)),
        compiler_params=_cparams(1),
        name="ada",
    )(c_pad, w_ada, b_ada)


def _fold_kernel(w_ref, mu_ref, o_ref):
    w = w_ref[...]
    mu = mu_ref[...]
    half = w.shape[1]
    o_ref[:, :half] = (w * (1.0 - mu)).astype(BF16)
    o_ref[:, half:] = (w * mu).astype(BF16)


def _fold(w_l, mu_l):
    d, half = w_l.shape
    return pl.pallas_call(
        _fold_kernel,
        out_shape=jax.ShapeDtypeStruct((d, 2 * half), BF16),
        compiler_params=pltpu.CompilerParams(vmem_limit_bytes=VMEM_LIMIT),
        name="fold",
    )(w_l, mu_l)


def _inproj_kernel(x_ref, g_ref, sc_ref, sh_ref, w_ref, qg_ref, kg_ref, hs_ref, p_ref, s_ref, h_scr,
                   *, n_f32):
    j = pl.program_id(1)

    @pl.when(j == 0)
    def _():
        x = x_ref[...]
        ms = jnp.mean(x * x, axis=-1, keepdims=True)
        y = x * lax.rsqrt(ms + RMS_EPS) * g_ref[...]
        h_scr[...] = (y * (1.0 + sc_ref[0]) + sh_ref[0]).astype(BF16)

    acc = _dot(h_scr[...], w_ref[...])

    def norm(t, gain):
        ms = _headsum(t * t, hs_ref[...]) * (1.0 / HEAD_DIM)
        return t * lax.rsqrt(ms + RMS_EPS) * gain

    @pl.when(j < n_f32)
    def _():
        p_ref[...] = acc

    @pl.when(j == n_f32)
    def _():
        s_ref[...] = (norm(acc, qg_ref[...]) * (LOG2E / math.sqrt(HEAD_DIM))).astype(BF16)

    @pl.when(j == n_f32 + 1)
    def _():
        s_ref[...] = norm(acc, kg_ref[...]).astype(BF16)

    @pl.when(j == n_f32 + 2)
    def _():
        s_ref[...] = acc.astype(BF16)


def _inproj(x2, gain, sc, sh, w_ext, qg, kg, hsum, seq, n_f32, tm=1024):
    m, d = x2.shape
    tn = qg.shape[1]
    nt = w_ext.shape[1] // tn
    assert nt == n_f32 + 3
    per_seq = seq // tm
    const = lambda a: pl.BlockSpec(a.shape, lambda i, j: (0,) * a.ndim)
    return pl.pallas_call(
        functools.partial(_inproj_kernel, n_f32=n_f32),
        out_shape=(jax.ShapeDtypeStruct((m, n_f32 * tn), F32),
                   jax.ShapeDtypeStruct((m, 3 * tn), BF16)),
        grid=(m // tm, nt),
        in_specs=[pl.BlockSpec((tm, d), lambda i, j: (i, 0)),
                  const(gain),
                  pl.BlockSpec((1, 1, d), lambda i, j: (i // per_seq, 0, 0)),
                  pl.BlockSpec((1, 1, d), lambda i, j: (i // per_seq, 0, 0)),
                  pl.BlockSpec((d, tn), lambda i, j: (0, j)),
                  const(qg), const(kg), const(hsum)],
        out_specs=(pl.BlockSpec((tm, tn), lambda i, j: (i, jnp.minimum(j, n_f32 - 1))),
                   pl.BlockSpec((tm, tn), lambda i, j: (i, jnp.maximum(j - n_f32, 0)))),
        scratch_shapes=[pltpu.VMEM((tm, d), BF16)],
        compiler_params=_cparams(2, VMEM_LIMIT_BIG),
        name="inproj",
    )(x2, gain, sc, sh, w_ext, qg, kg, hsum)


def _shift_rows(cur, prev_row):
    rolled = pltpu.roll(cur, 1, axis=0)
    row = lax.broadcasted_iota(jnp.int32, cur.shape, 0)
    return jnp.where(row == 0, jnp.broadcast_to(prev_row, cur.shape), rolled)


def _rwkv_kernel(p_ref, l_ref, mu_ref, w0_ref, w2_ref, a0_ref, a2_ref, g2_ref, kk_ref, ka_ref,
                 rk_ref, lng_ref, lnb_ref, hs_ref, o_ref, s_scr, prevp_scr, prevl_scr, y_scr,
                 *, n_heads):
    dr = n_heads * HEAD_DIM
    c = CHUNK

    @pl.when(pl.program_id(1) == 0)
    def _():
        s_scr[...] = jnp.zeros_like(s_scr)
        prevp_scr[...] = jnp.zeros_like(prevp_scr)
        prevl_scr[...] = jnp.zeros_like(prevl_scr)

    headsum = functools.partial(_headsum, hs=hs_ref[...])

    p = p_ref[0]
    pshift = _shift_rows(p, prevp_scr[...])
    prevp_scr[...] = p[c - 1:c, :]
    pm = p + (pshift - p) * mu_ref[...]
    r = pm[:, :dr]
    k = pm[:, dr:2 * dr]
    v = pm[:, 2 * dr:]

    l = l_ref[0]
    half = l.shape[1] // 2
    lb = l[:, half:]
    lin = l[:, :half] + _shift_rows(lb, prevl_scr[...])
    prevl_scr[...] = lb[c - 1:c, :]
    nw = w2_ref.shape[0]
    na = a2_ref.shape[0]
    lw = jnp.tanh(lin[:, :nw])
    la = lin[:, nw:nw + na]
    lg = _sigmoid(lin[:, nw + na:])

    w_pre = w0_ref[...] + _dot(lw.astype(BF16), w2_ref[...])
    w_log = -_softplus(-w_pre) - 0.5
    logw = -jnp.exp(w_log)
    a_gate = _sigmoid(a0_ref[...] + _dot(la.astype(BF16), a2_ref[...]))
    g = _dot(lg.astype(BF16), g2_ref[...])

    kk = k * kk_ref[...]
    kk = kk * lax.rsqrt(headsum(kk * kk) + L2_EPS)
    k2 = k * (1.0 + (a_gate - 1.0) * ka_ref[...])
    avec = -kk
    bvec = kk * a_gate
    bonus = headsum(r * k2 * rk_ref[...]) * v

    ti = lax.broadcasted_iota(jnp.int32, (c, c), 0)
    si = lax.broadcasted_iota(jnp.int32, (c, c), 1)
    strict = ti > si
    incl = ti >= si
    tri = jnp.where(incl, 1.0, 0.0).astype(BF16)
    l1, l2 = _split(logw)
    l3 = (logw - l1.astype(F32) - l2.astype(F32)).astype(BF16)
    lgc = _dot(tri, l1) + (_dot(tri, l2) + _dot(tri, l3))
    lg_end = lgc[c - 1:c, :]
    at = avec * jnp.exp(lgc - logw)
    ginv = jnp.exp(-lgc)
    bt = bvec * ginv
    kt = k2 * ginv
    rt = r * jnp.exp(lgc)
    gend = jnp.exp(lg_end - lgc)
    bh = bvec * gend
    kh = k2 * gend
    g_end = jnp.exp(lg_end)

    d = HEAD_DIM
    heads = range(n_heads)
    sls = [slice(h * d, (h + 1) * d) for h in heads]
    rows2 = lax.broadcasted_iota(jnp.int32, (2 * c, 2 * c), 0)
    cols2 = lax.broadcasted_iota(jnp.int32, (2 * c, 2 * c), 1)
    tt = jnp.where(rows2 >= c, rows2 - c, rows2)
    ii = jnp.where(cols2 >= c, cols2 - c, cols2)
    quad_mask = tt + jnp.where(rows2 >= c, 1, 0) > ii
    keep_z = lax.broadcasted_iota(jnp.int32, (c, 4 * d), 1) >= 2 * d

    ar = [jnp.concatenate([at[:, s], rt[:, s]], axis=0) for s in sls]
    bk = [jnp.concatenate([bt[:, s], kt[:, s]], axis=0) for s in sls]
    m4 = [jnp.where(quad_mask, _mm(ar[h], bk[h], _PASSES["pair"], nt=True), 0.0) for h in heads]
    akv_yk = [_mm(m4[h][:, c:], v[:, sls[h]], _PASSES["akv"]) for h in heads]
    zpad = jnp.zeros((c, d), F32)
    w = [jnp.concatenate([m4[h][:c, :c], zpad, at[:, sls[h]], akv_yk[h][:c]], axis=1)
         for h in heads]
    for lvl in range(6):
        rhs = w if lvl < 5 else [wh[:, 2 * d:] for wh in w]
        prod = [_mm(w[h][:, :d], rhs[h], _PASSES["solve"]) for h in heads]
        if lvl < 5:
            w = [prod[h] + jnp.where(keep_z, w[h], 0.0) for h in heads]
        else:
            w = [prod[h] + w[h][:, 2 * d:] for h in heads]
    s_old = [s_scr[h] for h in heads]
    su = [_mm(jnp.concatenate([w[h][:, :d], rt[:, sls[h]]], axis=0), s_old[h], _PASSES["state_read"],
              nt=True) for h in heads]
    u = [su[h][:c] + w[h][:, d:] for h in heads]
    rbu = [_mm(m4[h][c:, :c], u[h], _PASSES["rbu"]) for h in heads]
    upd = [_mm(jnp.concatenate([u[h], v[:, sls[h]]], axis=0).T,
               jnp.concatenate([bh[:, sls[h]], kh[:, sls[h]]], axis=0), _PASSES["state_write"])
           for h in heads]
    for h in heads:
        y_scr[:, sls[h]] = su[h][c:] + rbu[h] + akv_yk[h][c:]
        s_scr[h] = s_old[h] * g_end[:, sls[h]] + upd[h]

    y = y_scr[...]
    inv_n = 1.0 / HEAD_DIM
    mean = headsum(y) * inv_n
    yc = y - mean
    var = headsum(yc * yc) * inv_n
    yn = yc * lax.rsqrt(var + GN_EPS) * lng_ref[...] + lnb_ref[...]
    o_ref[0] = ((yn + bonus) * g).astype(o_ref.dtype)


def _rwkv(p3, mu_rkv, w0, w2p, a0, a2p, g2p, k_k, k_a, r_k, ln_g, ln_b, hsum, n_heads, lora_w):
    b, t, _ = p3.shape
    dr = n_heads * HEAD_DIM
    c = CHUNK
    lora_blk = (3 * dr) // lora_w
    row = lambda n: pl.BlockSpec((1, n), lambda i, j: (0, 0))
    full = lambda a: pl.BlockSpec(a.shape, lambda i, j: (0, 0))
    return pl.pallas_call(
        functools.partial(_rwkv_kernel, n_heads=n_heads),
        out_shape=jax.ShapeDtypeStruct((b, t, dr), BF16),
        grid=(b, t // c),
        in_specs=[pl.BlockSpec((1, c, 3 * dr), lambda i, j: (i, j, 0)),
                  pl.BlockSpec((1, c, lora_w), lambda i, j: (i, j, lora_blk)),
                  row(3 * dr), row(dr), full(w2p), row(dr), full(a2p), full(g2p),
                  row(dr), row(dr), row(dr), row(dr), row(dr), full(hsum)],
        out_specs=pl.BlockSpec((1, c, dr), lambda i, j: (i, j, 0)),
        scratch_shapes=[pltpu.VMEM((n_heads, HEAD_DIM, HEAD_DIM), F32),
                        pltpu.VMEM((1, 3 * dr), F32),
                        pltpu.VMEM((1, lora_w // 2), F32),
                        pltpu.VMEM((c, dr), F32)],
        compiler_params=_cparams(2),
        name="rwkv",
    )(p3, p3, mu_rkv, w0, w2p, a0, a2p, g2p, k_k, k_a, r_k, ln_g, ln_b, hsum)


def _sbattn_kernel(q_ref, k_ref, v_ref, o_ref, *, tq, tk, nb, npairs):
    qi = pl.program_id(2)
    nsub = tq // tk
    first = lax.broadcasted_iota(jnp.int32, (tk, LANES), 1) < HEAD_DIM
    rr = lax.broadcasted_iota(jnp.int32, (2 * tk, 2 * tk), 0)
    cc = lax.broadcasted_iota(jnp.int32, (2 * tk, 2 * tk), 1)
    tri2 = jnp.where((rr >= cc) & ((rr < tk) == (cc < tk)), 1.0, 0.0).astype(BF16)

    def per_head_rows(blk):
        zero = jnp.zeros_like(blk)
        return jnp.concatenate([jnp.where(first, blk, zero), jnp.where(first, zero, blk)], axis=0)

    def add_rows(full, r0, delta):
        if r0 == 0:
            return full + delta
        return jnp.concatenate([full[:r0], full[r0:] + delta], axis=0)

    def blocks(starts, state, row0s, masks):
        jobs = [(p, u) for p in range(npairs) for u in range(len(starts))]
        lanes = [slice(p * LANES, (p + 1) * LANES) for p in range(npairs)]
        kcs = [per_head_rows(k_ref[0, pl.ds(starts[u], tk), lanes[p]]) for p, u in jobs]
        vcs = [per_head_rows(v_ref[0, pl.ds(starts[u], tk), lanes[p]]) for p, u in jobs]
        zs = [_dot_nt(q_ref[0, row0s[u]:, lanes[p]], kc) for (p, u), kc in zip(jobs, kcs)]
        sps = []
        for (p, u), z in zip(jobs, zs):
            neg_abs = pltpu.bitcast(pltpu.bitcast(z, jnp.uint32) | jnp.uint32(0x80000000), F32)
            sp = jnp.maximum(z, 0.0) + jnp.log(1.0 + jnp.exp2(neg_abs)) * LOG2E
            if masks[u] is not None:
                sp = jnp.where(masks[u], sp, 0.0)
            sps.append(sp.astype(BF16))
        css = [_dot(sp, tri2) for sp in sps]
        carries = [c for c, _ in state]
        pvs = []
        for (p, u), z, cs, vc in zip(jobs, zs, css, vcs):
            r0 = row0s[u]
            rows = tq - r0
            attn = jnp.exp2((z - carries[p][r0:]) - cs)
            if masks[u] is not None:
                attn = jnp.where(masks[u], attn, 0.0)
            pvs.append(_dot(attn.astype(BF16), vc))
            tot = jnp.concatenate([jnp.broadcast_to(cs[:, 0:1], (rows, tk)),
                                   jnp.broadcast_to(cs[:, tk:tk + 1], (rows, tk))], axis=1)
            carries[p] = add_rows(carries[p], r0, tot)
        accs = [a for _, a in state]
        for (p, u), pv in zip(jobs, pvs):
            accs[p] = add_rows(accs[p], row0s[u], pv)
        return tuple(zip(carries, accs))

    state = tuple((jnp.zeros((tq, 2 * tk), F32), jnp.zeros((tq, LANES), F32)) for _ in range(npairs))
    for g in range(nsub // nb):
        subs = [nsub - 1 - g * nb - u for u in range(nb)]
        starts = [pl.multiple_of(qi * tq + sub * tk, tk) for sub in subs]
        masks = []
        for sub in subs:
            rows = tq - sub * tk
            qpos = lax.broadcasted_iota(jnp.int32, (rows, 2 * tk), 0)
            col = lax.broadcasted_iota(jnp.int32, (rows, 2 * tk), 1)
            masks.append(jnp.where(col >= tk, col - tk, col) < qpos)
        state = blocks(starts, state, [sub * tk for sub in subs], masks)

    def body(i, st):
        base = qi * tq - (i + 1) * (nb * tk)
        starts = [pl.multiple_of(base + (nb - 1 - u) * tk, tk) for u in range(nb)]
        return blocks(starts, st, [0] * nb, [None] * nb)

    state = lax.fori_loop(0, qi * (nsub // nb), body, state)
    o_ref[0] = jnp.concatenate([a for _, a in state], axis=1).astype(o_ref.dtype)


def _sbattn(qkv, tq=512, tk=128, nb=4, npairs=2):
    b, t, ds3 = qkv.shape
    ds = ds3 // 3
    w = npairs * LANES
    ng = ds // w
    return pl.pallas_call(
        functools.partial(_sbattn_kernel, tq=tq, tk=tk, nb=nb, npairs=npairs),
        out_shape=jax.ShapeDtypeStruct((b, t, ds), BF16),
        grid=(b, ng, t // tq),
        in_specs=[pl.BlockSpec((1, tq, w), lambda i, h, j: (i, j, h)),
                  pl.BlockSpec((1, t, w), lambda i, h, j: (i, 0, ng + h)),
                  pl.BlockSpec((1, t, w), lambda i, h, j: (i, 0, 2 * ng + h))],
        out_specs=pl.BlockSpec((1, tq, w), lambda i, h, j: (i, j, h)),
        compiler_params=_cparams(3),
        name="sbattn",
    )(qkv, qkv, qkv)


def _outproj_kernel(yr_ref, ys_ref, w_ref, x_ref, gt_ref, g2_ref, sc_ref, sh_ref, x1_ref, h2_ref):
    dr = yr_ref.shape[1]
    mix = _dot(yr_ref[...], w_ref[:dr, :]) + _dot(ys_ref[...], w_ref[dr:, :])
    x1 = x_ref[...] + gt_ref[0] * mix
    x1_ref[...] = x1
    ms = jnp.mean(x1 * x1, axis=-1, keepdims=True)
    y = x1 * lax.rsqrt(ms + RMS_EPS) * g2_ref[...]
    h2_ref[...] = (y * (1.0 + sc_ref[0]) + sh_ref[0]).astype(BF16)


def _outproj(yr, ys, w_out, x2, gt1, gain2, sc2, sh2, seq, tm=512):
    m, d = x2.shape
    dr = yr.shape[1]
    ds = ys.shape[1]
    per_seq = seq // tm
    mod = pl.BlockSpec((1, 1, d), lambda i: (i // per_seq, 0, 0))
    return pl.pallas_call(
        _outproj_kernel,
        out_shape=(jax.ShapeDtypeStruct((m, d), F32), jax.ShapeDtypeStruct((m, d), BF16)),
        grid=(m // tm,),
        in_specs=[pl.BlockSpec((tm, dr), lambda i: (i, 0)),
                  pl.BlockSpec((tm, ds), lambda i: (i, 0)),
                  pl.BlockSpec(w_out.shape, lambda i: (0, 0)),
                  pl.BlockSpec((tm, d), lambda i: (i, 0)),
                  mod,
                  pl.BlockSpec((1, d), lambda i: (0, 0)),
                  mod, mod],
        out_specs=(pl.BlockSpec((tm, d), lambda i: (i, 0)), pl.BlockSpec((tm, d), lambda i: (i, 0))),
        compiler_params=_cparams(1),
        name="outproj",
    )(yr, ys, w_out, x2, gt1, gain2, sc2, sh2)


def _ffn_kernel(h_ref, wg_ref, wu_ref, wd_ref, x_ref, gt_ref, o_ref, acc_scr):
    j = pl.program_id(1)

    @pl.when(j == 0)
    def _():
        acc_scr[...] = jnp.zeros_like(acc_scr)

    h = h_ref[...]
    gate = _dot(h, wg_ref[...])
    up = _dot(h, wu_ref[...])
    act = (gate * _sigmoid(gate) * up).astype(BF16)
    acc_scr[...] += _dot(act, wd_ref[...])

    @pl.when(j == pl.num_programs(1) - 1)
    def _():
        o_ref[...] = x_ref[...] + gt_ref[0] * acc_scr[...]


def _ffn(h2, w_gu, w_down, x1, gt2, seq, tm=512, tf=512):
    m, d = h2.shape
    dff = w_down.shape[0]
    nf = dff // tf
    per_seq = seq // tm
    return pl.pallas_call(
        _ffn_kernel,
        out_shape=jax.ShapeDtypeStruct((m, d), F32),
        grid=(m // tm, nf),
        in_specs=[pl.BlockSpec((tm, d), lambda i, j: (i, 0)),
                  pl.BlockSpec((d, tf), lambda i, j: (0, j)),
                  pl.BlockSpec((d, tf), lambda i, j: (0, j + nf)),
                  pl.BlockSpec((tf, d), lambda i, j: (j, 0)),
                  pl.BlockSpec((tm, d), lambda i, j: (i, 0)),
                  pl.BlockSpec((1, 1, d), lambda i, j: (i // per_seq, 0, 0))],
        out_specs=pl.BlockSpec((tm, d), lambda i, j: (i, 0)),
        scratch_shapes=[pltpu.VMEM((tm, d), F32)],
        compiler_params=_cparams(2),
        name="ffn",
    )(h2, w_gu, w_gu, w_down, x1, gt2)


def _pad_to(a, n, axis):
    pad = [(0, 0)] * a.ndim
    pad[axis] = (0, n - a.shape[axis])
    return jnp.pad(a, pad)


def _layer(x, c_pad, w_ada, b_ada, norm1_gain, norm2_gain, w_in, mu_rkv, mu_w, mu_a, mu_g, w0, w1,
           w2, a0, a1, a2, g1, g2, k_k, k_a, r_k, ln_x_gain, ln_x_bias, q_norm_gain, k_norm_gain,
           w_out, w_gate_up, w_down):
    b, t, d = x.shape
    dr = w0.shape[0]
    ds = d - dr
    n_rwkv = dr // HEAD_DIM
    n_sb = ds // HEAD_DIM
    row = lambda a: a.reshape(1, -1)

    mod = _ada(c_pad, w_ada, row(b_ada))[:b]
    sh1, sc1, gt1, sh2, sc2, gt2 = [m.reshape(b, 1, d) for m in jnp.split(mod, 6, axis=-1)]

    nw = -(-w1.shape[1] // LANES) * LANES
    na = -(-a1.shape[1] // LANES) * LANES
    ng = -(-g1.shape[1] // LANES) * LANES
    w_l = jnp.concatenate([_pad_to(w1, nw, 1), _pad_to(a1, na, 1), _pad_to(g1, ng, 1)], axis=1)
    mu_l = jnp.concatenate([jnp.broadcast_to(mu_w[:, None], (d, nw)),
                            jnp.broadcast_to(mu_a[:, None], (d, na)),
                            jnp.broadcast_to(mu_g[:, None], (d, ng))], axis=1)
    w_lora = _fold(w_l, mu_l)
    lora_w = w_lora.shape[1]
    assert dr == ds == lora_w, "column tiles of the input projection are one head group wide"
    w_in16 = w_in.astype(BF16)
    w_ext = jnp.concatenate([w_in16[:, :3 * dr], w_lora, w_in16[:, 3 * dr:]], axis=1)

    hsum = jnp.kron(jnp.eye(MXU_N // HEAD_DIM, dtype=F32),
                    jnp.ones((HEAD_DIM, HEAD_DIM), F32)).astype(BF16)
    x2 = x.reshape(b * t, d)
    p, qkv = _inproj(x2, row(norm1_gain), sc1, sh1, w_ext, row(jnp.tile(q_norm_gain, n_sb)),
                     row(jnp.tile(k_norm_gain, n_sb)), hsum, t, 4)
    p3 = p.reshape(b, t, -1)
    y_rwkv = _rwkv(p3, row(mu_rkv), row(w0), _pad_to(w2, nw, 0).astype(BF16), row(a0),
                   _pad_to(a2, na, 0).astype(BF16), _pad_to(g2, ng, 0).astype(BF16),
                   row(k_k), row(k_a), row(r_k), row(ln_x_gain), row(ln_x_bias), hsum,
                   n_rwkv, lora_w)

    y_sb = _sbattn(qkv.reshape(b, t, -1))

    x1, h2 = _outproj(y_rwkv.reshape(b * t, dr), y_sb.reshape(b * t, ds), w_out.astype(BF16), x2,
                      gt1, row(norm2_gain), sc2, sh2, t)
    out = _ffn(h2, w_gate_up.astype(BF16), w_down.astype(BF16), x1, gt2, t)
    return out.reshape(b, t, d)


def kernel(x, c, w_ada, b_ada, norm1_gain, norm2_gain, w_in, mu_rkv, mu_w, mu_a, mu_g, w0, w1, w2,
           a0, a1, a2, g1, g2, k_k, k_a, r_k, ln_x_gain, ln_x_bias, q_norm_gain, k_norm_gain, w_out,
           w_gate_up, w_down):
    depth = w_ada.shape[0]
    c_pad = _pad_to(c, 8, 0)
    for l in range(depth):
        x = _layer(x, c_pad, w_ada[l], b_ada[l], norm1_gain[l], norm2_gain[l], w_in[l], mu_rkv[l],
                   mu_w[l], mu_a[l], mu_g[l], w0[l], w1[l], w2[l], a0[l], a1[l], a2[l], g1[l],
                   g2[l], k_k[l], k_a[l], r_k[l].reshape(-1), ln_x_gain[l], ln_x_bias[l],
                   q_norm_gain[l], k_norm_gain[l], w_out[l], w_gate_up[l], w_down[l])
    return x
```

```python
import functools
import math

import jax
import jax.numpy as jnp
from jax import lax
from jax.experimental import pallas as pl
from jax.experimental.pallas import tpu as pltpu

F32 = jnp.float32
BF16 = jnp.bfloat16

HEAD_DIM = 64
RMS_EPS = 1e-6
GN_EPS = 64e-5
L2_EPS = 1e-12
LOG2E = math.log2(math.e)
LANES = 128
MXU_N = 256
CHUNK = 64
RWKV_CHUNKS = 2
VMEM_LIMIT = 48 * 1024 * 1024
VMEM_LIMIT_BIG = 56 * 1024 * 1024


def _cparams(n_axes, vmem_limit=VMEM_LIMIT):
    return pltpu.CompilerParams(dimension_semantics=("arbitrary",) * n_axes,
                                vmem_limit_bytes=vmem_limit)


def _sigmoid(x):
    return 1.0 / (1.0 + jnp.exp(-x))


def _softplus(x):
    return jnp.maximum(x, 0.0) + jnp.log(1.0 + jnp.exp(-jnp.abs(x)))


def _dot(x, y):
    return jnp.dot(x, y, preferred_element_type=F32)


def _dot_nt(x, y):
    return lax.dot_general(x, y, (((1,), (1,)), ((), ())), preferred_element_type=F32)


def _split(x):
    hi = x.astype(BF16)
    lo = (x - hi.astype(F32)).astype(BF16)
    return hi, lo


def _mm3(x, y, nt=False):
    d = _dot_nt if nt else _dot
    xh, xl = _split(x)
    yh, yl = _split(y)
    return d(xh, yh) + (d(xh, yl) + d(xl, yh))


def _mm(x, y, passes, nt=False):
    d = _dot_nt if nt else _dot
    if passes == 1:
        return d(x.astype(BF16), y.astype(BF16))
    xh, xl = _split(x)
    if passes == 2:
        yh = y.astype(BF16)
        return d(xh, yh) + d(xl, yh)
    yh, yl = _split(y)
    return d(xh, yh) + (d(xh, yl) + d(xl, yh))


_PASSES = {"pair": 1, "akv": 1, "solve": 1, "state_read": 1, "rbu": 1, "state_write": 1}


def _headsum(t, hs):
    g = hs.shape[0]
    return jnp.concatenate([_mm2_exact_rhs(t[:, i:i + g], hs) for i in range(0, t.shape[1], g)],
                           axis=1)


def _mm2_exact_rhs(x, y_bf16):
    xh, xl = _split(x)
    return _dot(xh, y_bf16) + _dot(xl, y_bf16)


def _ada_kernel(c_ref, w_ref, b_ref, o_ref):
    c = c_ref[...]
    ca = c * _sigmoid(c)
    o_ref[...] = _dot(ca.astype(BF16), w_ref[...].astype(BF16)) + b_ref[...]


def _ada(c_pad, w_ada, b_ada, tn=1024):
    m, d = c_pad.shape
    n = w_ada.shape[1]
    return pl.pallas_call(
        _ada_kernel,
        out_shape=jax.ShapeDtypeStruct((m, n), F32),
        grid=(n // tn,),
        in_specs=[pl.BlockSpec((m, d), lambda j: (0, 0)),
                  pl.BlockSpec((d, tn), lambda j: (0, j)),
                  pl.BlockSpec((1, tn), lambda j: (0, j))],
        out_specs=pl.BlockSpec((m, tn), lambda j: (0, j)),
        compiler_params=_cparams(1),
        name="ada",
    )(c_pad, w_ada, b_ada)


def _fold_kernel(w_ref, mu_ref, o_ref):
    w = w_ref[...]
    mu = mu_ref[...]
    half = w.shape[1]
    o_ref[:, :half] = (w * (1.0 - mu)).astype(BF16)
    o_ref[:, half:] = (w * mu).astype(BF16)


def _fold(w_l, mu_l):
    d, half = w_l.shape
    return pl.pallas_call(
        _fold_kernel,
        out_shape=jax.ShapeDtypeStruct((d, 2 * half), BF16),
        compiler_params=pltpu.CompilerParams(vmem_limit_bytes=VMEM_LIMIT),
        name="fold",
    )(w_l, mu_l)


def _inproj_kernel(x_ref, g_ref, sc_ref, sh_ref, w_ref, qg_ref, kg_ref, hs_ref, p_ref, s_ref, h_scr,
                   *, n_f32):
    j = pl.program_id(1)

    @pl.when(j == 0)
    def _():
        x = x_ref[...]
        ms = jnp.mean(x * x, axis=-1, keepdims=True)
        y = x * lax.rsqrt(ms + RMS_EPS) * g_ref[...]
        h_scr[...] = (y * (1.0 + sc_ref[0]) + sh_ref[0]).astype(BF16)

    acc = _dot(h_scr[...], w_ref[...])

    def norm(t, gain):
        ms = _headsum(t * t, hs_ref[...]) * (1.0 / HEAD_DIM)
        return t * lax.rsqrt(ms + RMS_EPS) * gain

    @pl.when(j < n_f32)
    def _():
        p_ref[...] = acc

    @pl.when(j == n_f32)
    def _():
        s_ref[...] = (norm(acc, qg_ref[...]) * (LOG2E / math.sqrt(HEAD_DIM))).astype(BF16)

    @pl.when(j == n_f32 + 1)
    def _():
        s_ref[...] = norm(acc, kg_ref[...]).astype(BF16)

    @pl.when(j == n_f32 + 2)
    def _():
        s_ref[...] = acc.astype(BF16)


def _inproj(x2, gain, sc, sh, w_ext, qg, kg, hsum, seq, n_f32, tm=1024):
    m, d = x2.shape
    tn = qg.shape[1]
    nt = w_ext.shape[1] // tn
    assert nt == n_f32 + 3
    per_seq = seq // tm
    const = lambda a: pl.BlockSpec(a.shape, lambda i, j: (0,) * a.ndim)
    return pl.pallas_call(
        functools.partial(_inproj_kernel, n_f32=n_f32),
        out_shape=(jax.ShapeDtypeStruct((m, n_f32 * tn), F32),
                   jax.ShapeDtypeStruct((m, 3 * tn), BF16)),
        grid=(m // tm, nt),
        in_specs=[pl.BlockSpec((tm, d), lambda i, j: (i, 0)),
                  const(gain),
                  pl.BlockSpec((1, 1, d), lambda i, j: (i // per_seq, 0, 0)),
                  pl.BlockSpec((1, 1, d), lambda i, j: (i // per_seq, 0, 0)),
                  pl.BlockSpec((d, tn), lambda i, j: (0, j)),
                  const(qg), const(kg), const(hsum)],
        out_specs=(pl.BlockSpec((tm, tn), lambda i, j: (i, jnp.minimum(j, n_f32 - 1))),
                   pl.BlockSpec((tm, tn), lambda i, j: (i, jnp.maximum(j - n_f32, 0)))),
        scratch_shapes=[pltpu.VMEM((tm, d), BF16)],
        compiler_params=_cparams(2, VMEM_LIMIT_BIG),
        name="inproj",
    )(x2, gain, sc, sh, w_ext, qg, kg, hsum)


def _shift_rows(cur, prev_row):
    rolled = pltpu.roll(cur, 1, axis=0)
    row = lax.broadcasted_iota(jnp.int32, cur.shape, 0)
    return jnp.where(row == 0, jnp.broadcast_to(prev_row, cur.shape), rolled)


_PREP = ("at", "rt", "bt", "kt", "v", "bh", "kh", "bonus", "g")


def _rwkv_kernel(p_ref, l_ref, mu_ref, w0_ref, w2_ref, a0_ref, a2_ref, g2_ref, kk_ref, ka_ref,
                 rk_ref, lng_ref, lnb_ref, hs_ref, o_ref, s_scr, prevp_scr, prevl_scr, y_scr,
                 prep_scr, gend_scr, *, n_heads):
    dr = n_heads * HEAD_DIM
    c = CHUNK
    nc = RWKV_CHUNKS
    d = HEAD_DIM
    heads = range(n_heads)
    units = [(ck, h) for ck in range(nc) for h in heads]
    sls = [slice(h * d, (h + 1) * d) for h in heads]
    rws = [slice(ck * c, (ck + 1) * c) for ck in range(nc)]

    @pl.when(pl.program_id(1) == 0)
    def _():
        s_scr[...] = jnp.zeros_like(s_scr)
        prevp_scr[...] = jnp.zeros_like(prevp_scr)
        prevl_scr[...] = jnp.zeros_like(prevl_scr)
        prep_scr[...] = jnp.zeros_like(prep_scr)
        gend_scr[...] = jnp.zeros_like(gend_scr)

    headsum = functools.partial(_headsum, hs=hs_ref[...])

    def prev(name, ck, h):
        i = _PREP.index(name)
        pair = prep_scr[i, rws[ck], (h // 2) * LANES:(h // 2 + 1) * LANES]
        return pair[:, (h % 2) * d:(h % 2 + 1) * d]

    rows2 = lax.broadcasted_iota(jnp.int32, (2 * c, 2 * c), 0)
    cols2 = lax.broadcasted_iota(jnp.int32, (2 * c, 2 * c), 1)
    tt = jnp.where(rows2 >= c, rows2 - c, rows2)
    ii = jnp.where(cols2 >= c, cols2 - c, cols2)
    quad_mask = tt + jnp.where(rows2 >= c, 1, 0) > ii
    keep_z = lax.broadcasted_iota(jnp.int32, (c, 4 * d), 1) >= 2 * d

    m4 = {}
    for un in units:
        ar = jnp.concatenate([prev("at", *un), prev("rt", *un)], axis=0)
        bk = jnp.concatenate([prev("bt", *un), prev("kt", *un)], axis=0)
        m4[un] = jnp.where(quad_mask, _mm(ar, bk, _PASSES["pair"], nt=True), 0.0)

    p = p_ref[0]
    tb = nc * c
    pshift = _shift_rows(p, prevp_scr[...])
    prevp_scr[...] = p[tb - 1:tb, :]
    pm = p + (pshift - p) * mu_ref[...]
    r = pm[:, :dr]
    k = pm[:, dr:2 * dr]
    v = pm[:, 2 * dr:]

    l = l_ref[0]
    half = l.shape[1] // 2
    lb = l[:, half:]
    lin = l[:, :half] + _shift_rows(lb, prevl_scr[...])
    prevl_scr[...] = lb[tb - 1:tb, :]
    nw = w2_ref.shape[0]
    na = a2_ref.shape[0]
    lw = jnp.tanh(lin[:, :nw])
    la = lin[:, nw:nw + na]
    lg = _sigmoid(lin[:, nw + na:])
    w_pre = w0_ref[...] + _dot(lw.astype(BF16), w2_ref[...])
    w_log = -_softplus(-w_pre) - 0.5
    logw = -jnp.exp(w_log)
    a_gate = _sigmoid(a0_ref[...] + _dot(la.astype(BF16), a2_ref[...]))
    g = _dot(lg.astype(BF16), g2_ref[...])
    kk = k * kk_ref[...]
    kk = kk * lax.rsqrt(headsum(kk * kk) + L2_EPS)
    k2 = k * (1.0 + (a_gate - 1.0) * ka_ref[...])
    avec = -kk
    bvec = kk * a_gate

    akv_yk = {un: _mm(m4[un][:, c:], prev("v", *un), _PASSES["akv"]) for un in units}

    ti = lax.broadcasted_iota(jnp.int32, (tb, tb), 0)
    si = lax.broadcasted_iota(jnp.int32, (tb, tb), 1)
    same_chunk = sum(((ti >= ck * c) & (ti < (ck + 1) * c) & (si >= ck * c)) for ck in range(nc)) > 0
    tri = jnp.where(same_chunk & (ti >= si), 1.0, 0.0).astype(BF16)
    l1, l2 = _split(logw)
    l3 = (logw - l1.astype(F32) - l2.astype(F32)).astype(BF16)
    lgc = _dot(tri, l1) + (_dot(tri, l2) + _dot(tri, l3))
    lg_ends = [lgc[(ck + 1) * c - 1:(ck + 1) * c, :] for ck in range(nc)]
    lg_end = jnp.concatenate([jnp.broadcast_to(e, (c, dr)) for e in lg_ends], axis=0)
    ginv = jnp.exp(-lgc)
    gend = jnp.exp(lg_end - lgc)
    new = {"at": avec * jnp.exp(lgc - logw), "rt": r * jnp.exp(lgc), "bt": bvec * ginv,
           "kt": k2 * ginv, "v": v, "bh": bvec * gend, "kh": k2 * gend, "g": g}
    new_gend = jnp.exp(jnp.concatenate(lg_ends, axis=0))

    zpad = jnp.zeros((c, d), F32)
    w = {un: jnp.concatenate([m4[un][:c, :c], zpad, prev("at", *un), akv_yk[un][:c]], axis=1)
         for un in units}
    for lvl in range(6):
        rhs = w if lvl < 5 else {un: w[un][:, 2 * d:] for un in units}
        prod = {un: _mm(w[un][:, :d], rhs[un], _PASSES["solve"]) for un in units}
        if lvl < 5:
            w = {un: prod[un] + jnp.where(keep_z, w[un], 0.0) for un in units}
        else:
            w = {un: prod[un] + w[un][:, 2 * d:] for un in units}

    new["bonus"] = headsum(r * k2 * rk_ref[...]) * v

    g_end = gend_scr[...]
    s = [s_scr[h] for h in heads]
    for ck in range(nc):
        su = [_mm(jnp.concatenate([w[ck, h][:, :d], prev("rt", ck, h)], axis=0), s[h],
                  _PASSES["state_read"], nt=True) for h in heads]
        u = [su[h][:c] + w[ck, h][:, d:] for h in heads]
        rbu = [_mm(m4[ck, h][c:, :c], u[h], _PASSES["rbu"]) for h in heads]
        upd = [_mm(jnp.concatenate([u[h], prev("v", ck, h)], axis=0).T,
                   jnp.concatenate([prev("bh", ck, h), prev("kh", ck, h)], axis=0),
                   _PASSES["state_write"]) for h in heads]
        for h in heads:
            y_scr[rws[ck], sls[h]] = su[h][c:] + rbu[h] + akv_yk[ck, h][c:]
        s = [s[h] * g_end[ck:ck + 1, sls[h]] + upd[h] for h in heads]
    for h in heads:
        s_scr[h] = s[h]

    y = y_scr[...]
    inv_n = 1.0 / HEAD_DIM
    mean = headsum(y) * inv_n
    yc = y - mean
    var = headsum(yc * yc) * inv_n
    yn = yc * lax.rsqrt(var + GN_EPS) * lng_ref[...] + lnb_ref[...]
    o_ref[0] = ((yn + prep_scr[_PREP.index("bonus")]) * prep_scr[_PREP.index("g")]).astype(o_ref.dtype)

    for i, name in enumerate(_PREP):
        prep_scr[i] = new[name]
    gend_scr[...] = new_gend


def _rwkv(p3, mu_rkv, w0, w2p, a0, a2p, g2p, k_k, k_a, r_k, ln_g, ln_b, hsum, n_heads, lora_w):
    b, t, _ = p3.shape
    dr = n_heads * HEAD_DIM
    c = CHUNK * RWKV_CHUNKS
    nt = t // c
    lora_blk = (3 * dr) // lora_w
    row = lambda n: pl.BlockSpec((1, n), lambda i, j: (0, 0))
    full = lambda a: pl.BlockSpec(a.shape, lambda i, j: (0, 0))
    return pl.pallas_call(
        functools.partial(_rwkv_kernel, n_heads=n_heads),
        out_shape=jax.ShapeDtypeStruct((b, t, dr), BF16),
        grid=(b, nt + 1),
        in_specs=[pl.BlockSpec((1, c, 3 * dr), lambda i, j: (i, jnp.minimum(j, nt - 1), 0)),
                  pl.BlockSpec((1, c, lora_w), lambda i, j: (i, jnp.minimum(j, nt - 1), lora_blk)),
                  row(3 * dr), row(dr), full(w2p), row(dr), full(a2p), full(g2p),
                  row(dr), row(dr), row(dr), row(dr), row(dr), full(hsum)],
        out_specs=pl.BlockSpec((1, c, dr), lambda i, j: (i, jnp.maximum(j - 1, 0), 0)),
        scratch_shapes=[pltpu.VMEM((n_heads, HEAD_DIM, HEAD_DIM), F32),
                        pltpu.VMEM((1, 3 * dr), F32),
                        pltpu.VMEM((1, lora_w // 2), F32),
                        pltpu.VMEM((c, dr), F32),
                        pltpu.VMEM((len(_PREP), c, dr), F32),
                        pltpu.VMEM((RWKV_CHUNKS, dr), F32)],
        compiler_params=_cparams(2),
        name="rwkv",
    )(p3, p3, mu_rkv, w0, w2p, a0, a2p, g2p, k_k, k_a, r_k, ln_g, ln_b, hsum)


def _sbattn_kernel(q_ref, k_ref, v_ref, o_ref, *, tq, tk, nb, npairs):
    qi = pl.program_id(2)
    nsub = tq // tk
    first = lax.broadcasted_iota(jnp.int32, (tk, LANES), 1) < HEAD_DIM
    rr = lax.broadcasted_iota(jnp.int32, (2 * tk, 2 * tk), 0)
    cc = lax.broadcasted_iota(jnp.int32, (2 * tk, 2 * tk), 1)
    tri2 = jnp.where((rr >= cc) & ((rr < tk) == (cc < tk)), 1.0, 0.0).astype(BF16)

    def per_head_rows(blk):
        zero = jnp.zeros_like(blk)
        return jnp.concatenate([jnp.where(first, blk, zero), jnp.where(first, zero, blk)], axis=0)

    def add_rows(full, r0, delta):
        if r0 == 0:
            return full + delta
        return jnp.concatenate([full[:r0], full[r0:] + delta], axis=0)

    def blocks(starts, state, row0s, masks):
        jobs = [(p, u) for p in range(npairs) for u in range(len(starts))]
        lanes = [slice(p * LANES, (p + 1) * LANES) for p in range(npairs)]
        kcs = [per_head_rows(k_ref[0, pl.ds(starts[u], tk), lanes[p]]) for p, u in jobs]
        vcs = [per_head_rows(v_ref[0, pl.ds(starts[u], tk), lanes[p]]) for p, u in jobs]
        zs = [_dot_nt(q_ref[0, row0s[u]:, lanes[p]], kc) for (p, u), kc in zip(jobs, kcs)]
        sps = []
        for (p, u), z in zip(jobs, zs):
            sp = jnp.maximum(z, 0.0) + jnp.log(1.0 + jnp.exp2(-jnp.abs(z))) * LOG2E
            if masks[u] is not None:
                sp = jnp.where(masks[u], sp, 0.0)
            sps.append(sp.astype(BF16))
        css = [_dot(sp, tri2) for sp in sps]
        carries = [c for c, _ in state]
        pvs = []
        for (p, u), z, cs, vc in zip(jobs, zs, css, vcs):
            r0 = row0s[u]
            rows = tq - r0
            attn = jnp.exp2((z - carries[p][r0:]) - cs)
            if masks[u] is not None:
                attn = jnp.where(masks[u], attn, 0.0)
            pvs.append(_dot(attn.astype(BF16), vc))
            tot = jnp.concatenate([jnp.broadcast_to(cs[:, 0:1], (rows, tk)),
                                   jnp.broadcast_to(cs[:, tk:tk + 1], (rows, tk))], axis=1)
            carries[p] = add_rows(carries[p], r0, tot)
        accs = [a for _, a in state]
        for (p, u), pv in zip(jobs, pvs):
            accs[p] = add_rows(accs[p], row0s[u], pv)
        return tuple(zip(carries, accs))

    state = tuple((jnp.zeros((tq, 2 * tk), F32), jnp.zeros((tq, LANES), F32)) for _ in range(npairs))
    for g in range(nsub // nb):
        subs = [nsub - 1 - g * nb - u for u in range(nb)]
        starts = [pl.multiple_of(qi * tq + sub * tk, tk) for sub in subs]
        masks = []
        for sub in subs:
            rows = tq - sub * tk
            qpos = lax.broadcasted_iota(jnp.int32, (rows, 2 * tk), 0)
            col = lax.broadcasted_iota(jnp.int32, (rows, 2 * tk), 1)
            masks.append(jnp.where(col >= tk, col - tk, col) < qpos)
        state = blocks(starts, state, [sub * tk for sub in subs], masks)

    def body(i, st):
        base = qi * tq - (i + 1) * (nb * tk)
        starts = [pl.multiple_of(base + (nb - 1 - u) * tk, tk) for u in range(nb)]
        return blocks(starts, st, [0] * nb, [None] * nb)

    state = lax.fori_loop(0, qi * (nsub // nb), body, state)
    o_ref[0] = jnp.concatenate([a for _, a in state], axis=1).astype(o_ref.dtype)


def _sbattn(qkv, tq=512, tk=128, nb=4, npairs=2):
    b, t, ds3 = qkv.shape
    ds = ds3 // 3
    w = npairs * LANES
    ng = ds // w
    return pl.pallas_call(
        functools.partial(_sbattn_kernel, tq=tq, tk=tk, nb=nb, npairs=npairs),
        out_shape=jax.ShapeDtypeStruct((b, t, ds), BF16),
        grid=(b, ng, t // tq),
        in_specs=[pl.BlockSpec((1, tq, w), lambda i, h, j: (i, j, h)),
                  pl.BlockSpec((1, t, w), lambda i, h, j: (i, 0, ng + h)),
                  pl.BlockSpec((1, t, w), lambda i, h, j: (i, 0, 2 * ng + h))],
        out_specs=pl.BlockSpec((1, tq, w), lambda i, h, j: (i, j, h)),
        compiler_params=_cparams(3),
        name="sbattn",
    )(qkv, qkv, qkv)


def _outproj_kernel(yr_ref, ys_ref, w_ref, x_ref, gt_ref, g2_ref, sc_ref, sh_ref, x1_ref, h2_ref):
    dr = yr_ref.shape[1]
    mix = _dot(yr_ref[...], w_ref[:dr, :]) + _dot(ys_ref[...], w_ref[dr:, :])
    x1 = x_ref[...] + gt_ref[0] * mix
    x1_ref[...] = x1
    ms = jnp.mean(x1 * x1, axis=-1, keepdims=True)
    y = x1 * lax.rsqrt(ms + RMS_EPS) * g2_ref[...]
    h2_ref[...] = (y * (1.0 + sc_ref[0]) + sh_ref[0]).astype(BF16)


def _outproj(yr, ys, w_out, x2, gt1, gain2, sc2, sh2, seq, tm=512):
    m, d = x2.shape
    dr = yr.shape[1]
    ds = ys.shape[1]
    per_seq = seq // tm
    mod = pl.BlockSpec((1, 1, d), lambda i: (i // per_seq, 0, 0))
    return pl.pallas_call(
        _outproj_kernel,
        out_shape=(jax.ShapeDtypeStruct((m, d), F32), jax.ShapeDtypeStruct((m, d), BF16)),
        grid=(m // tm,),
        in_specs=[pl.BlockSpec((tm, dr), lambda i: (i, 0)),
                  pl.BlockSpec((tm, ds), lambda i: (i, 0)),
                  pl.BlockSpec(w_out.shape, lambda i: (0, 0)),
                  pl.BlockSpec((tm, d), lambda i: (i, 0)),
                  mod,
                  pl.BlockSpec((1, d), lambda i: (0, 0)),
                  mod, mod],
        out_specs=(pl.BlockSpec((tm, d), lambda i: (i, 0)), pl.BlockSpec((tm, d), lambda i: (i, 0))),
        compiler_params=_cparams(1),
        name="outproj",
    )(yr, ys, w_out, x2, gt1, gain2, sc2, sh2)


def _ffn_kernel(h_ref, wg_ref, wu_ref, wd_ref, x_ref, gt_ref, o_ref, acc_scr):
    j = pl.program_id(1)

    @pl.when(j == 0)
    def _():
        acc_scr[...] = jnp.zeros_like(acc_scr)

    h = h_ref[...]
    gate = _dot(h, wg_ref[...])
    up = _dot(h, wu_ref[...])
    act = (gate * _sigmoid(gate) * up).astype(BF16)
    acc_scr[...] += _dot(act, wd_ref[...])

    @pl.when(j == pl.num_programs(1) - 1)
    def _():
        o_ref[...] = x_ref[...] + gt_ref[0] * acc_scr[...]


def _ffn(h2, w_gu, w_down, x1, gt2, seq, tm=512, tf=512):
    m, d = h2.shape
    dff = w_down.shape[0]
    nf = dff // tf
    per_seq = seq // tm
    return pl.pallas_call(
        _ffn_kernel,
        out_shape=jax.ShapeDtypeStruct((m, d), F32),
        grid=(m // tm, nf),
        in_specs=[pl.BlockSpec((tm, d), lambda i, j: (i, 0)),
                  pl.BlockSpec((d, tf), lambda i, j: (0, j)),
                  pl.BlockSpec((d, tf), lambda i, j: (0, j + nf)),
                  pl.BlockSpec((tf, d), lambda i, j: (j, 0)),
                  pl.BlockSpec((tm, d), lambda i, j: (i, 0)),
                  pl.BlockSpec((1, 1, d), lambda i, j: (i // per_seq, 0, 0))],
        out_specs=pl.BlockSpec((tm, d), lambda i, j: (i, 0)),
        scratch_shapes=[pltpu.VMEM((tm, d), F32)],
        compiler_params=_cparams(2),
        name="ffn",
    )(h2, w_gu, w_gu, w_down, x1, gt2)


def _pad_to(a, n, axis):
    pad = [(0, 0)] * a.ndim
    pad[axis] = (0, n - a.shape[axis])
    return jnp.pad(a, pad)


def _layer(x, c_pad, w_ada, b_ada, norm1_gain, norm2_gain, w_in, mu_rkv, mu_w, mu_a, mu_g, w0, w1,
           w2, a0, a1, a2, g1, g2, k_k, k_a, r_k, ln_x_gain, ln_x_bias, q_norm_gain, k_norm_gain,
           w_out, w_gate_up, w_down):
    b, t, d = x.shape
    dr = w0.shape[0]
    ds = d - dr
    n_rwkv = dr // HEAD_DIM
    n_sb = ds // HEAD_DIM
    row = lambda a: a.reshape(1, -1)

    mod = _ada(c_pad, w_ada, row(b_ada))[:b]
    sh1, sc1, gt1, sh2, sc2, gt2 = [m.reshape(b, 1, d) for m in jnp.split(mod, 6, axis=-1)]

    nw = -(-w1.shape[1] // LANES) * LANES
    na = -(-a1.shape[1] // LANES) * LANES
    ng = -(-g1.shape[1] // LANES) * LANES
    w_l = jnp.concatenate([_pad_to(w1, nw, 1), _pad_to(a1, na, 1), _pad_to(g1, ng, 1)], axis=1)
    mu_l = jnp.concatenate([jnp.broadcast_to(mu_w[:, None], (d, nw)),
                            jnp.broadcast_to(mu_a[:, None], (d, na)),
                            jnp.broadcast_to(mu_g[:, None], (d, ng))], axis=1)
    w_lora = _fold(w_l, mu_l)
    lora_w = w_lora.shape[1]
    assert dr == ds == lora_w, "column tiles of the input projection are one head group wide"
    w_in16 = w_in.astype(BF16)
    w_ext = jnp.concatenate([w_in16[:, :3 * dr], w_lora, w_in16[:, 3 * dr:]], axis=1)

    hsum = jnp.kron(jnp.eye(MXU_N // HEAD_DIM, dtype=F32),
                    jnp.ones((HEAD_DIM, HEAD_DIM), F32)).astype(BF16)
    x2 = x.reshape(b * t, d)
    p, qkv = _inproj(x2, row(norm1_gain), sc1, sh1, w_ext, row(jnp.tile(q_norm_gain, n_sb)),
                     row(jnp.tile(k_norm_gain, n_sb)), hsum, t, 4)
    p3 = p.reshape(b, t, -1)
    y_rwkv = _rwkv(p3, row(mu_rkv), row(w0), _pad_to(w2, nw, 0).astype(BF16), row(a0),
                   _pad_to(a2, na, 0).astype(BF16), _pad_to(g2, ng, 0).astype(BF16),
                   row(k_k), row(k_a), row(r_k), row(ln_x_gain), row(ln_x_bias), hsum,
                   n_rwkv, lora_w)

    y_sb = _sbattn(qkv.reshape(b, t, -1))

    x1, h2 = _outproj(y_rwkv.reshape(b * t, dr), y_sb.reshape(b * t, ds), w_out.astype(BF16), x2,
                      gt1, row(norm2_gain), sc2, sh2, t)
    out = _ffn(h2, w_gate_up.astype(BF16), w_down.astype(BF16), x1, gt2, t)
    return out.reshape(b, t, d)


def kernel(x, c, w_ada, b_ada, norm1_gain, norm2_gain, w_in, mu_rkv, mu_w, mu_a, mu_g, w0, w1, w2,
           a0, a1, a2, g1, g2, k_k, k_a, r_k, ln_x_gain, ln_x_bias, q_norm_gain, k_norm_gain, w_out,
           w_gate_up, w_down):
    depth = w_ada.shape[0]
    c_pad = _pad_to(c, 8, 0)
    for l in range(depth):
        x = _layer(x, c_pad, w_ada[l], b_ada[l], norm1_gain[l], norm2_gain[l], w_in[l], mu_rkv[l],
                   mu_w[l], mu_a[l], mu_g[l], w0[l], w1[l], w2[l], a0[l], a1[l], a2[l], g1[l],
                   g2[l], k_k[l], k_a[l], r_k[l].reshape(-1), ln_x_gain[l], ln_x_bias[l],
                   q_norm_gain[l], k_norm_gain[l], w_out[l], w_gate_up[l], w_down[l])
    return x
```

```python
import functools
import math

import jax
import jax.numpy as jnp
from jax import lax
from jax.experimental import pallas as pl
from jax.experimental.pallas import tpu as pltpu

F32 = jnp.float32
BF16 = jnp.bfloat16

HEAD_DIM = 64
RMS_EPS = 1e-6
GN_EPS = 64e-5
L2_EPS = 1e-12
LOG2E = math.log2(math.e)
SB_DEAD = 150.0
LANES = 128
MXU_N = 256
CHUNK = 64
RWKV_CHUNKS = 2
VMEM_LIMIT = 48 * 1024 * 1024
VMEM_LIMIT_BIG = 56 * 1024 * 1024


def _cparams(n_axes, vmem_limit=VMEM_LIMIT):
    return pltpu.CompilerParams(dimension_semantics=("arbitrary",) * n_axes,
                                vmem_limit_bytes=vmem_limit)


def _sigmoid(x):
    return 1.0 / (1.0 + jnp.exp(-x))


def _softplus(x):
    return jnp.maximum(x, 0.0) + jnp.log(1.0 + jnp.exp(-jnp.abs(x)))


def _dot(x, y):
    return jnp.dot(x, y, preferred_element_type=F32)


def _dot_nt(x, y):
    return lax.dot_general(x, y, (((1,), (1,)), ((), ())), preferred_element_type=F32)


def _split(x):
    hi = x.astype(BF16)
    lo = (x - hi.astype(F32)).astype(BF16)
    return hi, lo


def _mm3(x, y, nt=False):
    d = _dot_nt if nt else _dot
    xh, xl = _split(x)
    yh, yl = _split(y)
    return d(xh, yh) + (d(xh, yl) + d(xl, yh))


def _mm(x, y, passes, nt=False):
    d = _dot_nt if nt else _dot
    if passes == 1:
        return d(x.astype(BF16), y.astype(BF16))
    xh, xl = _split(x)
    if passes == 2:
        yh = y.astype(BF16)
        return d(xh, yh) + d(xl, yh)
    yh, yl = _split(y)
    return d(xh, yh) + (d(xh, yl) + d(xl, yh))


_PASSES = {"pair": 1, "akv": 1, "solve": 1, "state_read": 1, "rbu": 1, "state_write": 1}


def _headsum(t, hs):
    g = hs.shape[0]
    return jnp.concatenate([_mm2_exact_rhs(t[:, i:i + g], hs) for i in range(0, t.shape[1], g)],
                           axis=1)


def _mm2_exact_rhs(x, y_bf16):
    xh, xl = _split(x)
    return _dot(xh, y_bf16) + _dot(xl, y_bf16)


def _ada_kernel(c_ref, w_ref, b_ref, o_ref):
    c = c_ref[...]
    ca = c * _sigmoid(c)
    o_ref[...] = _dot(ca.astype(BF16), w_ref[...].astype(BF16)) + b_ref[...]


def _ada(c_pad, w_ada, b_ada, tn=1024):
    m, d = c_pad.shape
    n = w_ada.shape[1]
    return pl.pallas_call(
        _ada_kernel,
        out_shape=jax.ShapeDtypeStruct((m, n), F32),
        grid=(n // tn,),
        in_specs=[pl.BlockSpec((m, d), lambda j: (0, 0)),
                  pl.BlockSpec((d, tn), lambda j: (0, j)),
                  pl.BlockSpec((1, tn), lambda j: (0, j))],
        out_specs=pl.BlockSpec((m, tn), lambda j: (0, j)),
        compiler_params=_cparams(1),
        name="ada",
    )(c_pad, w_ada, b_ada)


def _fold_kernel(w_ref, mu_ref, o_ref):
    w = w_ref[...]
    mu = mu_ref[...]
    half = w.shape[1]
    o_ref[:, :half] = (w * (1.0 - mu)).astype(BF16)
    o_ref[:, half:] = (w * mu).astype(BF16)


def _fold(w_l, mu_l):
    d, half = w_l.shape
    return pl.pallas_call(
        _fold_kernel,
        out_shape=jax.ShapeDtypeStruct((d, 2 * half), BF16),
        compiler_params=pltpu.CompilerParams(vmem_limit_bytes=VMEM_LIMIT),
        name="fold",
    )(w_l, mu_l)


def _inproj_kernel(x_ref, g_ref, sc_ref, sh_ref, w_ref, qg_ref, kg_ref, hs_ref, p_ref, s_ref, h_scr,
                   *, n_f32):
    j = pl.program_id(1)

    @pl.when(j == 0)
    def _():
        x = x_ref[...]
        ms = jnp.mean(x * x, axis=-1, keepdims=True)
        y = x * lax.rsqrt(ms + RMS_EPS) * g_ref[...]
        h_scr[...] = (y * (1.0 + sc_ref[0]) + sh_ref[0]).astype(BF16)

    acc = _dot(h_scr[...], w_ref[...])

    def norm(t, gain):
        ms = _headsum(t * t, hs_ref[...]) * (1.0 / HEAD_DIM)
        return t * lax.rsqrt(ms + RMS_EPS) * gain

    @pl.when(j < n_f32)
    def _():
        p_ref[...] = acc

    @pl.when(j == n_f32)
    def _():
        s_ref[...] = (norm(acc, qg_ref[...]) * (LOG2E / math.sqrt(HEAD_DIM))).astype(BF16)

    @pl.when(j == n_f32 + 1)
    def _():
        s_ref[...] = norm(acc, kg_ref[...]).astype(BF16)

    @pl.when(j == n_f32 + 2)
    def _():
        s_ref[...] = acc.astype(BF16)


def _inproj(x2, gain, sc, sh, w_ext, qg, kg, hsum, seq, n_f32, tm=1024):
    m, d = x2.shape
    tn = qg.shape[1]
    nt = w_ext.shape[1] // tn
    assert nt == n_f32 + 3
    per_seq = seq // tm
    const = lambda a: pl.BlockSpec(a.shape, lambda i, j: (0,) * a.ndim)
    return pl.pallas_call(
        functools.partial(_inproj_kernel, n_f32=n_f32),
        out_shape=(jax.ShapeDtypeStruct((m, n_f32 * tn), F32),
                   jax.ShapeDtypeStruct((m, 3 * tn), BF16)),
        grid=(m // tm, nt),
        in_specs=[pl.BlockSpec((tm, d), lambda i, j: (i, 0)),
                  const(gain),
                  pl.BlockSpec((1, 1, d), lambda i, j: (i // per_seq, 0, 0)),
                  pl.BlockSpec((1, 1, d), lambda i, j: (i // per_seq, 0, 0)),
                  pl.BlockSpec((d, tn), lambda i, j: (0, j)),
                  const(qg), const(kg), const(hsum)],
        out_specs=(pl.BlockSpec((tm, tn), lambda i, j: (i, jnp.minimum(j, n_f32 - 1))),
                   pl.BlockSpec((tm, tn), lambda i, j: (i, jnp.maximum(j - n_f32, 0)))),
        scratch_shapes=[pltpu.VMEM((tm, d), BF16)],
        compiler_params=_cparams(2, VMEM_LIMIT_BIG),
        name="inproj",
    )(x2, gain, sc, sh, w_ext, qg, kg, hsum)


def _shift_rows(cur, prev_row):
    rolled = pltpu.roll(cur, 1, axis=0)
    row = lax.broadcasted_iota(jnp.int32, cur.shape, 0)
    return jnp.where(row == 0, jnp.broadcast_to(prev_row, cur.shape), rolled)


_PREP = ("at", "rt", "bt", "kt", "v", "bh", "kh", "bonus", "g")


def _rwkv_kernel(p_ref, l_ref, mu_ref, w0_ref, w2_ref, a0_ref, a2_ref, g2_ref, kk_ref, ka_ref,
                 rk_ref, lng_ref, lnb_ref, hs_ref, o_ref, s_scr, prevp_scr, prevl_scr, y_scr,
                 prep_scr, gend_scr, *, n_heads):
    dr = n_heads * HEAD_DIM
    c = CHUNK
    nc = RWKV_CHUNKS
    d = HEAD_DIM
    heads = range(n_heads)
    units = [(ck, h) for ck in range(nc) for h in heads]
    sls = [slice(h * d, (h + 1) * d) for h in heads]
    rws = [slice(ck * c, (ck + 1) * c) for ck in range(nc)]

    @pl.when(pl.program_id(1) == 0)
    def _():
        s_scr[...] = jnp.zeros_like(s_scr)
        prevp_scr[...] = jnp.zeros_like(prevp_scr)
        prevl_scr[...] = jnp.zeros_like(prevl_scr)
        prep_scr[...] = jnp.zeros_like(prep_scr)
        gend_scr[...] = jnp.zeros_like(gend_scr)

    headsum = functools.partial(_headsum, hs=hs_ref[...])

    def prev(name, ck, h):
        i = _PREP.index(name)
        pair = prep_scr[i, rws[ck], (h // 2) * LANES:(h // 2 + 1) * LANES]
        return pair[:, (h % 2) * d:(h % 2 + 1) * d]

    rows2 = lax.broadcasted_iota(jnp.int32, (2 * c, 2 * c), 0)
    cols2 = lax.broadcasted_iota(jnp.int32, (2 * c, 2 * c), 1)
    tt = jnp.where(rows2 >= c, rows2 - c, rows2)
    ii = jnp.where(cols2 >= c, cols2 - c, cols2)
    quad_mask = tt + jnp.where(rows2 >= c, 1, 0) > ii
    keep_z = lax.broadcasted_iota(jnp.int32, (c, 4 * d), 1) >= 2 * d

    m4 = {}
    for un in units:
        ar = jnp.concatenate([prev("at", *un), prev("rt", *un)], axis=0)
        bk = jnp.concatenate([prev("bt", *un), prev("kt", *un)], axis=0)
        m4[un] = jnp.where(quad_mask, _mm(ar, bk, _PASSES["pair"], nt=True), 0.0)

    p = p_ref[0]
    tb = nc * c
    pshift = _shift_rows(p, prevp_scr[...])
    prevp_scr[...] = p[tb - 1:tb, :]
    pm = p + (pshift - p) * mu_ref[...]
    r = pm[:, :dr]
    k = pm[:, dr:2 * dr]
    v = pm[:, 2 * dr:]

    l = l_ref[0]
    half = l.shape[1] // 2
    lb = l[:, half:]
    lin = l[:, :half] + _shift_rows(lb, prevl_scr[...])
    prevl_scr[...] = lb[tb - 1:tb, :]
    nw = w2_ref.shape[0]
    na = a2_ref.shape[0]
    lw = jnp.tanh(lin[:, :nw])
    la = lin[:, nw:nw + na]
    lg = _sigmoid(lin[:, nw + na:])
    w_pre = w0_ref[...] + _dot(lw.astype(BF16), w2_ref[...])
    w_log = -_softplus(-w_pre) - 0.5
    logw = -jnp.exp(w_log)
    a_gate = _sigmoid(a0_ref[...] + _dot(la.astype(BF16), a2_ref[...]))
    g = _dot(lg.astype(BF16), g2_ref[...])
    kk = k * kk_ref[...]
    kk = kk * lax.rsqrt(headsum(kk * kk) + L2_EPS)
    k2 = k * (1.0 + (a_gate - 1.0) * ka_ref[...])
    avec = -kk
    bvec = kk * a_gate

    akv_yk = {un: _mm(m4[un][:, c:], prev("v", *un), _PASSES["akv"]) for un in units}

    ti = lax.broadcasted_iota(jnp.int32, (tb, tb), 0)
    si = lax.broadcasted_iota(jnp.int32, (tb, tb), 1)
    same_chunk = sum(((ti >= ck * c) & (ti < (ck + 1) * c) & (si >= ck * c)) for ck in range(nc)) > 0
    tri = jnp.where(same_chunk & (ti >= si), 1.0, 0.0).astype(BF16)
    l1, l2 = _split(logw)
    l3 = (logw - l1.astype(F32) - l2.astype(F32)).astype(BF16)
    lgc = _dot(tri, l1) + (_dot(tri, l2) + _dot(tri, l3))
    lg_ends = [lgc[(ck + 1) * c - 1:(ck + 1) * c, :] for ck in range(nc)]
    lg_end = jnp.concatenate([jnp.broadcast_to(e, (c, dr)) for e in lg_ends], axis=0)
    ginv = jnp.exp(-lgc)
    gend = jnp.exp(lg_end - lgc)
    new = {"at": avec * jnp.exp(lgc - logw), "rt": r * jnp.exp(lgc), "bt": bvec * ginv,
           "kt": k2 * ginv, "v": v, "bh": bvec * gend, "kh": k2 * gend, "g": g}
    new_gend = jnp.exp(jnp.concatenate(lg_ends, axis=0))

    zpad = jnp.zeros((c, d), F32)
    w = {un: jnp.concatenate([m4[un][:c, :c], zpad, prev("at", *un), akv_yk[un][:c]], axis=1)
         for un in units}
    for lvl in range(6):
        rhs = w if lvl < 5 else {un: w[un][:, 2 * d:] for un in units}
        prod = {un: _mm(w[un][:, :d], rhs[un], _PASSES["solve"]) for un in units}
        if lvl < 5:
            w = {un: prod[un] + jnp.where(keep_z, w[un], 0.0) for un in units}
        else:
            w = {un: prod[un] + w[un][:, 2 * d:] for un in units}

    new["bonus"] = headsum(r * k2 * rk_ref[...]) * v

    g_end = gend_scr[...]
    s = [s_scr[h] for h in heads]
    for ck in range(nc):
        su = [_mm(jnp.concatenate([w[ck, h][:, :d], prev("rt", ck, h)], axis=0), s[h],
                  _PASSES["state_read"], nt=True) for h in heads]
        u = [su[h][:c] + w[ck, h][:, d:] for h in heads]
        rbu = [_mm(m4[ck, h][c:, :c], u[h], _PASSES["rbu"]) for h in heads]
        upd = [_mm(jnp.concatenate([u[h], prev("v", ck, h)], axis=0).T,
                   jnp.concatenate([prev("bh", ck, h), prev("kh", ck, h)], axis=0),
                   _PASSES["state_write"]) for h in heads]
        for h in heads:
            y_scr[rws[ck], sls[h]] = su[h][c:] + rbu[h] + akv_yk[ck, h][c:]
        s = [s[h] * g_end[ck:ck + 1, sls[h]] + upd[h] for h in heads]
    for h in heads:
        s_scr[h] = s[h]

    y = y_scr[...]
    inv_n = 1.0 / HEAD_DIM
    mean = headsum(y) * inv_n
    yc = y - mean
    var = headsum(yc * yc) * inv_n
    yn = yc * lax.rsqrt(var + GN_EPS) * lng_ref[...] + lnb_ref[...]
    o_ref[0] = ((yn + prep_scr[_PREP.index("bonus")]) * prep_scr[_PREP.index("g")]).astype(o_ref.dtype)

    for i, name in enumerate(_PREP):
        prep_scr[i] = new[name]
    gend_scr[...] = new_gend


def _rwkv(p3, mu_rkv, w0, w2p, a0, a2p, g2p, k_k, k_a, r_k, ln_g, ln_b, hsum, n_heads, lora_w):
    b, t, _ = p3.shape
    dr = n_heads * HEAD_DIM
    c = CHUNK * RWKV_CHUNKS
    nt = t // c
    lora_blk = (3 * dr) // lora_w
    row = lambda n: pl.BlockSpec((1, n), lambda i, j: (0, 0))
    full = lambda a: pl.BlockSpec(a.shape, lambda i, j: (0, 0))
    return pl.pallas_call(
        functools.partial(_rwkv_kernel, n_heads=n_heads),
        out_shape=jax.ShapeDtypeStruct((b, t, dr), BF16),
        grid=(b, nt + 1),
        in_specs=[pl.BlockSpec((1, c, 3 * dr), lambda i, j: (i, jnp.minimum(j, nt - 1), 0)),
                  pl.BlockSpec((1, c, lora_w), lambda i, j: (i, jnp.minimum(j, nt - 1), lora_blk)),
                  row(3 * dr), row(dr), full(w2p), row(dr), full(a2p), full(g2p),
                  row(dr), row(dr), row(dr), row(dr), row(dr), full(hsum)],
        out_specs=pl.BlockSpec((1, c, dr), lambda i, j: (i, jnp.maximum(j - 1, 0), 0)),
        scratch_shapes=[pltpu.VMEM((n_heads, HEAD_DIM, HEAD_DIM), F32),
                        pltpu.VMEM((1, 3 * dr), F32),
                        pltpu.VMEM((1, lora_w // 2), F32),
                        pltpu.VMEM((c, dr), F32),
                        pltpu.VMEM((len(_PREP), c, dr), F32),
                        pltpu.VMEM((RWKV_CHUNKS, dr), F32)],
        compiler_params=_cparams(2),
        name="rwkv",
    )(p3, p3, mu_rkv, w0, w2p, a0, a2p, g2p, k_k, k_a, r_k, ln_g, ln_b, hsum)


def _sbattn_kernel(q_ref, k_ref, v_ref, o_ref, *, tq, tk, nb, npairs):
    qi = pl.program_id(2)
    nsub = tq // tk
    first = lax.broadcasted_iota(jnp.int32, (tk, LANES), 1) < HEAD_DIM
    rr = lax.broadcasted_iota(jnp.int32, (2 * tk, 2 * tk), 0)
    cc = lax.broadcasted_iota(jnp.int32, (2 * tk, 2 * tk), 1)
    tri2 = jnp.where((rr >= cc) & ((rr < tk) == (cc < tk)), 1.0, 0.0).astype(BF16)

    def per_head_rows(blk):
        zero = jnp.zeros_like(blk)
        return jnp.concatenate([jnp.where(first, blk, zero), jnp.where(first, zero, blk)], axis=0)

    def add_rows(full, r0, delta):
        if r0 == 0:
            return full + delta
        return jnp.concatenate([full[:r0], full[r0:] + delta], axis=0)

    def blocks(starts, state, row0s, masks):
        jobs = [(p, u) for p in range(npairs) for u in range(len(starts))]
        lanes = [slice(p * LANES, (p + 1) * LANES) for p in range(npairs)]
        kcs = [per_head_rows(k_ref[0, pl.ds(starts[u], tk), lanes[p]]) for p, u in jobs]
        vcs = [per_head_rows(v_ref[0, pl.ds(starts[u], tk), lanes[p]]) for p, u in jobs]
        zs = [_dot_nt(q_ref[0, row0s[u]:, lanes[p]], kc) for (p, u), kc in zip(jobs, kcs)]
        sps = []
        for (p, u), z in zip(jobs, zs):
            sp = jnp.maximum(z, 0.0) + jnp.log(1.0 + jnp.exp2(-jnp.abs(z))) * LOG2E
            if masks[u] is not None:
                sp = jnp.where(masks[u], sp, 0.0)
            sps.append(sp.astype(BF16))
        css = [_dot(sp, tri2) for sp in sps]
        carries = [c for c, _ in state]
        pvs = []
        for (p, u), z, cs, vc in zip(jobs, zs, css, vcs):
            r0 = row0s[u]
            rows = tq - r0
            attn = jnp.exp2(jnp.minimum(z - cs, 0.0) - carries[p][r0:])
            if masks[u] is not None:
                attn = jnp.where(masks[u], attn, 0.0)
            pvs.append(_dot(attn.astype(BF16), vc))
            tot = jnp.concatenate([jnp.broadcast_to(cs[:, 0:1], (rows, tk)),
                                   jnp.broadcast_to(cs[:, tk:tk + 1], (rows, tk))], axis=1)
            carries[p] = add_rows(carries[p], r0, tot)
        accs = [a for _, a in state]
        for (p, u), pv in zip(jobs, pvs):
            accs[p] = add_rows(accs[p], row0s[u], pv)
        return tuple(zip(carries, accs))

    state = tuple((jnp.zeros((tq, 2 * tk), F32), jnp.zeros((tq, LANES), F32)) for _ in range(npairs))
    for g in range(nsub // nb):
        subs = [nsub - 1 - g * nb - u for u in range(nb)]
        starts = [pl.multiple_of(qi * tq + sub * tk, tk) for sub in subs]
        masks = []
        for sub in subs:
            rows = tq - sub * tk
            qpos = lax.broadcasted_iota(jnp.int32, (rows, 2 * tk), 0)
            col = lax.broadcasted_iota(jnp.int32, (rows, 2 * tk), 1)
            masks.append(jnp.where(col >= tk, col - tk, col) < qpos)
        state = blocks(starts, state, [sub * tk for sub in subs], masks)

    def min_carry(st):
        m = st[0][0]
        for c_, _ in st[1:]:
            m = jnp.minimum(m, c_)
        return jnp.min(jnp.minimum(m[:, :tk], m[:, tk:]))

    n_steps = qi * (nsub // nb)

    def cond(loop):
        i, _, cmin = loop
        return (i < n_steps) & (cmin < SB_DEAD)

    def body(loop):
        i, st, _ = loop
        base = qi * tq - (i + 1) * (nb * tk)
        starts = [pl.multiple_of(base + (nb - 1 - u) * tk, tk) for u in range(nb)]
        st = blocks(starts, st, [0] * nb, [None] * nb)
        return i + 1, st, min_carry(st)

    _, state, _ = lax.while_loop(cond, body, (jnp.int32(0), state, min_carry(state)))
    o_ref[0] = jnp.concatenate([a for _, a in state], axis=1).astype(o_ref.dtype)


def _sbattn(qkv, tq=512, tk=128, nb=2, npairs=2):
    b, t, ds3 = qkv.shape
    ds = ds3 // 3
    w = npairs * LANES
    ng = ds // w
    return pl.pallas_call(
        functools.partial(_sbattn_kernel, tq=tq, tk=tk, nb=nb, npairs=npairs),
        out_shape=jax.ShapeDtypeStruct((b, t, ds), BF16),
        grid=(b, ng, t // tq),
        in_specs=[pl.BlockSpec((1, tq, w), lambda i, h, j: (i, j, h)),
                  pl.BlockSpec((1, t, w), lambda i, h, j: (i, 0, ng + h)),
                  pl.BlockSpec((1, t, w), lambda i, h, j: (i, 0, 2 * ng + h))],
        out_specs=pl.BlockSpec((1, tq, w), lambda i, h, j: (i, j, h)),
        compiler_params=_cparams(3),
        name="sbattn",
    )(qkv, qkv, qkv)


def _outproj_kernel(yr_ref, ys_ref, w_ref, x_ref, gt_ref, g2_ref, sc_ref, sh_ref, x1_ref, h2_ref):
    dr = yr_ref.shape[1]
    mix = _dot(yr_ref[...], w_ref[:dr, :]) + _dot(ys_ref[...], w_ref[dr:, :])
    x1 = x_ref[...] + gt_ref[0] * mix
    x1_ref[...] = x1
    ms = jnp.mean(x1 * x1, axis=-1, keepdims=True)
    y = x1 * lax.rsqrt(ms + RMS_EPS) * g2_ref[...]
    h2_ref[...] = (y * (1.0 + sc_ref[0]) + sh_ref[0]).astype(BF16)


def _outproj(yr, ys, w_out, x2, gt1, gain2, sc2, sh2, seq, tm=512):
    m, d = x2.shape
    dr = yr.shape[1]
    ds = ys.shape[1]
    per_seq = seq // tm
    mod = pl.BlockSpec((1, 1, d), lambda i: (i // per_seq, 0, 0))
    return pl.pallas_call(
        _outproj_kernel,
        out_shape=(jax.ShapeDtypeStruct((m, d), F32), jax.ShapeDtypeStruct((m, d), BF16)),
        grid=(m // tm,),
        in_specs=[pl.BlockSpec((tm, dr), lambda i: (i, 0)),
                  pl.BlockSpec((tm, ds), lambda i: (i, 0)),
                  pl.BlockSpec(w_out.shape, lambda i: (0, 0)),
                  pl.BlockSpec((tm, d), lambda i: (i, 0)),
                  mod,
                  pl.BlockSpec((1, d), lambda i: (0, 0)),
                  mod, mod],
        out_specs=(pl.BlockSpec((tm, d), lambda i: (i, 0)), pl.BlockSpec((tm, d), lambda i: (i, 0))),
        compiler_params=_cparams(1),
        name="outproj",
    )(yr, ys, w_out, x2, gt1, gain2, sc2, sh2)


def _ffn_kernel(h_ref, wg_ref, wu_ref, wd_ref, x_ref, gt_ref, o_ref, acc_scr):
    j = pl.program_id(1)

    @pl.when(j == 0)
    def _():
        acc_scr[...] = jnp.zeros_like(acc_scr)

    h = h_ref[...]
    gate = _dot(h, wg_ref[...])
    up = _dot(h, wu_ref[...])
    act = (gate * _sigmoid(gate) * up).astype(BF16)
    acc_scr[...] += _dot(act, wd_ref[...])

    @pl.when(j == pl.num_programs(1) - 1)
    def _():
        o_ref[...] = x_ref[...] + gt_ref[0] * acc_scr[...]


def _ffn(h2, w_gu, w_down, x1, gt2, seq, tm=1024, tf=512):
    m, d = h2.shape
    dff = w_down.shape[0]
    nf = dff // tf
    per_seq = seq // tm
    once = pl.Buffered(1)
    return pl.pallas_call(
        _ffn_kernel,
        out_shape=jax.ShapeDtypeStruct((m, d), F32),
        grid=(m // tm, nf),
        in_specs=[pl.BlockSpec((tm, d), lambda i, j: (i, 0)),
                  pl.BlockSpec((d, tf), lambda i, j: (0, j)),
                  pl.BlockSpec((d, tf), lambda i, j: (0, j + nf)),
                  pl.BlockSpec((tf, d), lambda i, j: (j, 0)),
                  pl.BlockSpec((tm, d), lambda i, j: (i, 0), pipeline_mode=once),
                  pl.BlockSpec((1, 1, d), lambda i, j: (i // per_seq, 0, 0))],
        out_specs=pl.BlockSpec((tm, d), lambda i, j: (i, 0), pipeline_mode=once),
        scratch_shapes=[pltpu.VMEM((tm, d), F32)],
        compiler_params=_cparams(2, VMEM_LIMIT_BIG),
        name="ffn",
    )(h2, w_gu, w_gu, w_down, x1, gt2)


def _pad_to(a, n, axis):
    pad = [(0, 0)] * a.ndim
    pad[axis] = (0, n - a.shape[axis])
    return jnp.pad(a, pad)


def _layer(x, c_pad, w_ada, b_ada, norm1_gain, norm2_gain, w_in, mu_rkv, mu_w, mu_a, mu_g, w0, w1,
           w2, a0, a1, a2, g1, g2, k_k, k_a, r_k, ln_x_gain, ln_x_bias, q_norm_gain, k_norm_gain,
           w_out, w_gate_up, w_down):
    b, t, d = x.shape
    dr = w0.shape[0]
    ds = d - dr
    n_rwkv = dr // HEAD_DIM
    n_sb = ds // HEAD_DIM
    row = lambda a: a.reshape(1, -1)

    mod = _ada(c_pad, w_ada, row(b_ada))[:b]
    sh1, sc1, gt1, sh2, sc2, gt2 = [m.reshape(b, 1, d) for m in jnp.split(mod, 6, axis=-1)]

    nw = -(-w1.shape[1] // LANES) * LANES
    na = -(-a1.shape[1] // LANES) * LANES
    ng = -(-g1.shape[1] // LANES) * LANES
    w_l = jnp.concatenate([_pad_to(w1, nw, 1), _pad_to(a1, na, 1), _pad_to(g1, ng, 1)], axis=1)
    mu_l = jnp.concatenate([jnp.broadcast_to(mu_w[:, None], (d, nw)),
                            jnp.broadcast_to(mu_a[:, None], (d, na)),
                            jnp.broadcast_to(mu_g[:, None], (d, ng))], axis=1)
    w_lora = _fold(w_l, mu_l)
    lora_w = w_lora.shape[1]
    assert dr == ds == lora_w, "column tiles of the input projection are one head group wide"
    w_in16 = w_in.astype(BF16)
    w_ext = jnp.concatenate([w_in16[:, :3 * dr], w_lora, w_in16[:, 3 * dr:]], axis=1)

    hsum = jnp.kron(jnp.eye(MXU_N // HEAD_DIM, dtype=F32),
                    jnp.ones((HEAD_DIM, HEAD_DIM), F32)).astype(BF16)
    x2 = x.reshape(b * t, d)
    p, qkv = _inproj(x2, row(norm1_gain), sc1, sh1, w_ext, row(jnp.tile(q_norm_gain, n_sb)),
                     row(jnp.tile(k_norm_gain, n_sb)), hsum, t, 4)
    p3 = p.reshape(b, t, -1)
    y_rwkv = _rwkv(p3, row(mu_rkv), row(w0), _pad_to(w2, nw, 0).astype(BF16), row(a0),
                   _pad_to(a2, na, 0).astype(BF16), _pad_to(g2, ng, 0).astype(BF16),
                   row(k_k), row(k_a), row(r_k), row(ln_x_gain), row(ln_x_bias), hsum,
                   n_rwkv, lora_w)

    y_sb = _sbattn(qkv.reshape(b, t, -1))

    x1, h2 = _outproj(y_rwkv.reshape(b * t, dr), y_sb.reshape(b * t, ds), w_out.astype(BF16), x2,
                      gt1, row(norm2_gain), sc2, sh2, t)
    out = _ffn(h2, w_gate_up.astype(BF16), w_down.astype(BF16), x1, gt2, t)
    return out.reshape(b, t, d)


def kernel(x, c, w_ada, b_ada, norm1_gain, norm2_gain, w_in, mu_rkv, mu_w, mu_a, mu_g, w0, w1, w2,
           a0, a1, a2, g1, g2, k_k, k_a, r_k, ln_x_gain, ln_x_bias, q_norm_gain, k_norm_gain, w_out,
           w_gate_up, w_down):
    depth = w_ada.shape[0]
    c_pad = _pad_to(c, 8, 0)
    for l in range(depth):
        x = _layer(x, c_pad, w_ada[l], b_ada[l], norm1_gain[l], norm2_gain[l], w_in[l], mu_rkv[l],
                   mu_w[l], mu_a[l], mu_g[l], w0[l], w1[l], w2[l], a0[l], a1[l], a2[l], g1[l],
                   g2[l], k_k[l], k_a[l], r_k[l].reshape(-1), ln_x_gain[l], ln_x_bias[l],
                   q_norm_gain[l], k_norm_gain[l], w_out[l], w_gate_up[l], w_down[l])
    return x
```

```python
import functools
import math

import jax
import jax.numpy as jnp
from jax import lax
from jax.experimental import pallas as pl
from jax.experimental.pallas import tpu as pltpu

F32 = jnp.float32
BF16 = jnp.bfloat16

HEAD_DIM = 64
RMS_EPS = 1e-6
GN_EPS = 64e-5
L2_EPS = 1e-12
LOG2E = math.log2(math.e)
SB_DEAD = 150.0
LANES = 128
MXU_N = 256
CHUNK = 64
RWKV_CHUNKS = 2
VMEM_LIMIT = 48 * 1024 * 1024
VMEM_LIMIT_BIG = 56 * 1024 * 1024


def _cparams(n_axes, vmem_limit=VMEM_LIMIT):
    return pltpu.CompilerParams(dimension_semantics=("arbitrary",) * n_axes,
                                vmem_limit_bytes=vmem_limit)


def _sigmoid(x):
    return 1.0 / (1.0 + jnp.exp(-x))


def _softplus(x):
    return jnp.maximum(x, 0.0) + jnp.log(1.0 + jnp.exp(-jnp.abs(x)))


def _dot(x, y):
    return jnp.dot(x, y, preferred_element_type=F32)


def _dot_nt(x, y):
    return lax.dot_general(x, y, (((1,), (1,)), ((), ())), preferred_element_type=F32)


def _split(x):
    hi = x.astype(BF16)
    lo = (x - hi.astype(F32)).astype(BF16)
    return hi, lo


def _mm3(x, y, nt=False):
    d = _dot_nt if nt else _dot
    xh, xl = _split(x)
    yh, yl = _split(y)
    return d(xh, yh) + (d(xh, yl) + d(xl, yh))


def _mm(x, y, passes, nt=False):
    d = _dot_nt if nt else _dot
    if passes == 1:
        return d(x.astype(BF16), y.astype(BF16))
    xh, xl = _split(x)
    if passes == 2:
        yh = y.astype(BF16)
        return d(xh, yh) + d(xl, yh)
    yh, yl = _split(y)
    return d(xh, yh) + (d(xh, yl) + d(xl, yh))


_PASSES = {"pair": 1, "akv": 1, "solve": 1, "state_read": 1, "rbu": 1, "state_write": 1}


def _headsum(t, hs):
    g = hs.shape[0]
    return jnp.concatenate([_mm2_exact_rhs(t[:, i:i + g], hs) for i in range(0, t.shape[1], g)],
                           axis=1)


def _mm2_exact_rhs(x, y_bf16):
    xh, xl = _split(x)
    return _dot(xh, y_bf16) + _dot(xl, y_bf16)


def _ada_kernel(c_ref, w_ref, b_ref, o_ref):
    c = c_ref[...]
    ca = c * _sigmoid(c)
    o_ref[...] = _dot(ca.astype(BF16), w_ref[...].astype(BF16)) + b_ref[...]


def _ada(c_pad, w_ada, b_ada, tn=1024):
    m, d = c_pad.shape
    n = w_ada.shape[1]
    return pl.pallas_call(
        _ada_kernel,
        out_shape=jax.ShapeDtypeStruct((m, n), F32),
        grid=(n // tn,),
        in_specs=[pl.BlockSpec((m, d), lambda j: (0, 0)),
                  pl.BlockSpec((d, tn), lambda j: (0, j)),
                  pl.BlockSpec((1, tn), lambda j: (0, j))],
        out_specs=pl.BlockSpec((m, tn), lambda j: (0, j)),
        compiler_params=_cparams(1),
        name="ada",
    )(c_pad, w_ada, b_ada)


def _fold_kernel(w_ref, mu_ref, o_ref):
    w = w_ref[...]
    mu = mu_ref[...]
    half = w.shape[1]
    o_ref[:, :half] = (w * (1.0 - mu)).astype(BF16)
    o_ref[:, half:] = (w * mu).astype(BF16)


def _fold(w_l, mu_l):
    d, half = w_l.shape
    return pl.pallas_call(
        _fold_kernel,
        out_shape=jax.ShapeDtypeStruct((d, 2 * half), BF16),
        compiler_params=pltpu.CompilerParams(vmem_limit_bytes=VMEM_LIMIT),
        name="fold",
    )(w_l, mu_l)


def _inproj_kernel(x_ref, g_ref, sc_ref, sh_ref, w_ref, qg_ref, kg_ref, hs_ref, p_ref, s_ref, h_scr,
                   *, n_f32):
    j = pl.program_id(1)

    @pl.when(j == 0)
    def _():
        x = x_ref[...]
        ms = jnp.mean(x * x, axis=-1, keepdims=True)
        y = x * lax.rsqrt(ms + RMS_EPS) * g_ref[...]
        h_scr[...] = (y * (1.0 + sc_ref[0]) + sh_ref[0]).astype(BF16)

    acc = _dot(h_scr[...], w_ref[...])

    def norm(t, gain):
        ms = _headsum(t * t, hs_ref[...]) * (1.0 / HEAD_DIM)
        return t * lax.rsqrt(ms + RMS_EPS) * gain

    @pl.when(j < n_f32)
    def _():
        p_ref[...] = acc

    @pl.when(j == n_f32)
    def _():
        s_ref[...] = (norm(acc, qg_ref[...]) * (LOG2E / math.sqrt(HEAD_DIM))).astype(BF16)

    @pl.when(j == n_f32 + 1)
    def _():
        s_ref[...] = norm(acc, kg_ref[...]).astype(BF16)

    @pl.when(j == n_f32 + 2)
    def _():
        s_ref[...] = acc.astype(BF16)


def _inproj(x2, gain, sc, sh, w_ext, qg, kg, hsum, seq, n_f32, tm=1024):
    m, d = x2.shape
    tn = qg.shape[1]
    nt = w_ext.shape[1] // tn
    assert nt == n_f32 + 3
    per_seq = seq // tm
    const = lambda a: pl.BlockSpec(a.shape, lambda i, j: (0,) * a.ndim)
    return pl.pallas_call(
        functools.partial(_inproj_kernel, n_f32=n_f32),
        out_shape=(jax.ShapeDtypeStruct((m, n_f32 * tn), F32),
                   jax.ShapeDtypeStruct((m, 3 * tn), BF16)),
        grid=(m // tm, nt),
        in_specs=[pl.BlockSpec((tm, d), lambda i, j: (i, 0)),
                  const(gain),
                  pl.BlockSpec((1, 1, d), lambda i, j: (i // per_seq, 0, 0)),
                  pl.BlockSpec((1, 1, d), lambda i, j: (i // per_seq, 0, 0)),
                  pl.BlockSpec((d, tn), lambda i, j: (0, j)),
                  const(qg), const(kg), const(hsum)],
        out_specs=(pl.BlockSpec((tm, tn), lambda i, j: (i, jnp.minimum(j, n_f32 - 1))),
                   pl.BlockSpec((tm, tn), lambda i, j: (i, jnp.maximum(j - n_f32, 0)))),
        scratch_shapes=[pltpu.VMEM((tm, d), BF16)],
        compiler_params=_cparams(2, VMEM_LIMIT_BIG),
        name="inproj",
    )(x2, gain, sc, sh, w_ext, qg, kg, hsum)


def _shift_rows(cur, prev_row):
    rolled = pltpu.roll(cur, 1, axis=0)
    row = lax.broadcasted_iota(jnp.int32, cur.shape, 0)
    return jnp.where(row == 0, jnp.broadcast_to(prev_row, cur.shape), rolled)


_PREP = ("at", "rt", "bt", "kt", "v", "bh", "kh", "bonus", "g")


def _rwkv_kernel(p_ref, l_ref, mu_ref, w0_ref, w2_ref, a0_ref, a2_ref, g2_ref, kk_ref, ka_ref,
                 rk_ref, lng_ref, lnb_ref, hs_ref, o_ref, s_scr, prevp_scr, prevl_scr, y_scr,
                 prep_scr, gend_scr, *, n_heads):
    dr = n_heads * HEAD_DIM
    c = CHUNK
    nc = RWKV_CHUNKS
    d = HEAD_DIM
    heads = range(n_heads)
    units = [(ck, h) for ck in range(nc) for h in heads]
    sls = [slice(h * d, (h + 1) * d) for h in heads]
    rws = [slice(ck * c, (ck + 1) * c) for ck in range(nc)]

    @pl.when(pl.program_id(1) == 0)
    def _():
        s_scr[...] = jnp.zeros_like(s_scr)
        prevp_scr[...] = jnp.zeros_like(prevp_scr)
        prevl_scr[...] = jnp.zeros_like(prevl_scr)
        prep_scr[...] = jnp.zeros_like(prep_scr)
        gend_scr[...] = jnp.zeros_like(gend_scr)

    headsum = functools.partial(_headsum, hs=hs_ref[...])

    def prev(name, ck, h):
        i = _PREP.index(name)
        pair = prep_scr[i, rws[ck], (h // 2) * LANES:(h // 2 + 1) * LANES]
        return pair[:, (h % 2) * d:(h % 2 + 1) * d]

    rows2 = lax.broadcasted_iota(jnp.int32, (2 * c, 2 * c), 0)
    cols2 = lax.broadcasted_iota(jnp.int32, (2 * c, 2 * c), 1)
    tt = jnp.where(rows2 >= c, rows2 - c, rows2)
    ii = jnp.where(cols2 >= c, cols2 - c, cols2)
    quad_mask = tt + jnp.where(rows2 >= c, 1, 0) > ii

    m4 = {}
    for un in units:
        ar = jnp.concatenate([prev("at", *un), prev("rt", *un)], axis=0)
        bk = jnp.concatenate([prev("bt", *un), prev("kt", *un)], axis=0)
        m4[un] = jnp.where(quad_mask, _mm(ar, bk, _PASSES["pair"], nt=True), 0.0)

    p = p_ref[0]
    tb = nc * c
    pshift = _shift_rows(p, prevp_scr[...])
    prevp_scr[...] = p[tb - 1:tb, :]
    pm = p + (pshift - p) * mu_ref[...]
    r = pm[:, :dr]
    k = pm[:, dr:2 * dr]
    v = pm[:, 2 * dr:]

    l = l_ref[0]
    half = l.shape[1] // 2
    lb = l[:, half:]
    lin = l[:, :half] + _shift_rows(lb, prevl_scr[...])
    prevl_scr[...] = lb[tb - 1:tb, :]
    nw = w2_ref.shape[0]
    na = a2_ref.shape[0]
    lw = jnp.tanh(lin[:, :nw])
    la = lin[:, nw:nw + na]
    lg = _sigmoid(lin[:, nw + na:])
    w_pre = w0_ref[...] + _dot(lw.astype(BF16), w2_ref[...])
    w_log = -_softplus(-w_pre) - 0.5
    logw = -jnp.exp(w_log)
    a_gate = _sigmoid(a0_ref[...] + _dot(la.astype(BF16), a2_ref[...]))
    g = _dot(lg.astype(BF16), g2_ref[...])
    kk = k * kk_ref[...]
    kk = kk * lax.rsqrt(headsum(kk * kk) + L2_EPS)
    k2 = k * (1.0 + (a_gate - 1.0) * ka_ref[...])
    avec = -kk
    bvec = kk * a_gate

    akv_yk = {un: _mm(m4[un][:, c:], prev("v", *un), _PASSES["akv"]) for un in units}

    ti = lax.broadcasted_iota(jnp.int32, (tb, tb), 0)
    si = lax.broadcasted_iota(jnp.int32, (tb, tb), 1)
    same_chunk = sum(((ti >= ck * c) & (ti < (ck + 1) * c) & (si >= ck * c)) for ck in range(nc)) > 0
    tri = jnp.where(same_chunk & (ti >= si), 1.0, 0.0).astype(BF16)
    l1, l2 = _split(logw)
    l3 = (logw - l1.astype(F32) - l2.astype(F32)).astype(BF16)
    lgc = _dot(tri, l1) + (_dot(tri, l2) + _dot(tri, l3))
    lg_ends = [lgc[(ck + 1) * c - 1:(ck + 1) * c, :] for ck in range(nc)]
    lg_end = jnp.concatenate([jnp.broadcast_to(e, (c, dr)) for e in lg_ends], axis=0)
    ginv = jnp.exp(-lgc)
    gend = jnp.exp(lg_end - lgc)
    new = {"at": avec * jnp.exp(lgc - logw), "rt": r * jnp.exp(lgc), "bt": bvec * ginv,
           "kt": k2 * ginv, "v": v, "bh": bvec * gend, "kh": k2 * gend, "g": g}
    new_gend = jnp.exp(jnp.concatenate(lg_ends, axis=0))

    zpad = jnp.zeros((c, d), F32)
    w = {un: jnp.concatenate([m4[un][:c, :c], zpad, prev("at", *un), akv_yk[un][:c]], axis=1)
         for un in units}
    for lvl in range(6):
        rhs = w if lvl < 5 else {un: w[un][:, 2 * d:] for un in units}
        prod = {un: _mm(w[un][:, :d], rhs[un], _PASSES["solve"]) for un in units}
        if lvl < 5:
            w = {un: jnp.concatenate([prod[un][:, :2 * d], prod[un][:, 2 * d:] + w[un][:, 2 * d:]],
                                     axis=1) for un in units}
        else:
            w = {un: prod[un] + w[un][:, 2 * d:] for un in units}

    new["bonus"] = headsum(r * k2 * rk_ref[...]) * v

    g_end = gend_scr[...]
    s = [s_scr[h] for h in heads]
    for ck in range(nc):
        su = [_mm(jnp.concatenate([w[ck, h][:, :d], prev("rt", ck, h)], axis=0), s[h],
                  _PASSES["state_read"], nt=True) for h in heads]
        u = [su[h][:c] + w[ck, h][:, d:] for h in heads]
        rbu = [_mm(m4[ck, h][c:, :c], u[h], _PASSES["rbu"]) for h in heads]
        upd = [_mm(jnp.concatenate([u[h], prev("v", ck, h)], axis=0).T,
                   jnp.concatenate([prev("bh", ck, h), prev("kh", ck, h)], axis=0),
                   _PASSES["state_write"]) for h in heads]
        for h in heads:
            y_scr[rws[ck], sls[h]] = su[h][c:] + rbu[h] + akv_yk[ck, h][c:]
        s = [s[h] * g_end[ck:ck + 1, sls[h]] + upd[h] for h in heads]
    for h in heads:
        s_scr[h] = s[h]

    y = y_scr[...]
    inv_n = 1.0 / HEAD_DIM
    mean = headsum(y) * inv_n
    yc = y - mean
    var = headsum(yc * yc) * inv_n
    yn = yc * lax.rsqrt(var + GN_EPS) * lng_ref[...] + lnb_ref[...]
    o_ref[0] = ((yn + prep_scr[_PREP.index("bonus")]) * prep_scr[_PREP.index("g")]).astype(o_ref.dtype)

    for i, name in enumerate(_PREP):
        prep_scr[i] = new[name]
    gend_scr[...] = new_gend


def _rwkv(p3, mu_rkv, w0, w2p, a0, a2p, g2p, k_k, k_a, r_k, ln_g, ln_b, hsum, n_heads, lora_w):
    b, t, _ = p3.shape
    dr = n_heads * HEAD_DIM
    c = CHUNK * RWKV_CHUNKS
    nt = t // c
    lora_blk = (3 * dr) // lora_w
    row = lambda n: pl.BlockSpec((1, n), lambda i, j: (0, 0))
    full = lambda a: pl.BlockSpec(a.shape, lambda i, j: (0, 0))
    return pl.pallas_call(
        functools.partial(_rwkv_kernel, n_heads=n_heads),
        out_shape=jax.ShapeDtypeStruct((b, t, dr), BF16),
        grid=(b, nt + 1),
        in_specs=[pl.BlockSpec((1, c, 3 * dr), lambda i, j: (i, jnp.minimum(j, nt - 1), 0)),
                  pl.BlockSpec((1, c, lora_w), lambda i, j: (i, jnp.minimum(j, nt - 1), lora_blk)),
                  row(3 * dr), row(dr), full(w2p), row(dr), full(a2p), full(g2p),
                  row(dr), row(dr), row(dr), row(dr), row(dr), full(hsum)],
        out_specs=pl.BlockSpec((1, c, dr), lambda i, j: (i, jnp.maximum(j - 1, 0), 0)),
        scratch_shapes=[pltpu.VMEM((n_heads, HEAD_DIM, HEAD_DIM), F32),
                        pltpu.VMEM((1, 3 * dr), F32),
                        pltpu.VMEM((1, lora_w // 2), F32),
                        pltpu.VMEM((c, dr), F32),
                        pltpu.VMEM((len(_PREP), c, dr), F32),
                        pltpu.VMEM((RWKV_CHUNKS, dr), F32)],
        compiler_params=_cparams(2),
        name="rwkv",
    )(p3, p3, mu_rkv, w0, w2p, a0, a2p, g2p, k_k, k_a, r_k, ln_g, ln_b, hsum)


def _sbattn_kernel(q_ref, k_ref, v_ref, o_ref, *, tq, tk, nb, npairs):
    qi = pl.program_id(2)
    nsub = tq // tk
    first = lax.broadcasted_iota(jnp.int32, (tk, LANES), 1) < HEAD_DIM
    rr = lax.broadcasted_iota(jnp.int32, (2 * tk, 2 * tk), 0)
    cc = lax.broadcasted_iota(jnp.int32, (2 * tk, 2 * tk), 1)
    tri2 = jnp.where((rr >= cc) & ((rr < tk) == (cc < tk)), 1.0, 0.0).astype(BF16)

    def per_head_rows(blk):
        zero = jnp.zeros_like(blk)
        return jnp.concatenate([jnp.where(first, blk, zero), jnp.where(first, zero, blk)], axis=0)

    def add_rows(full, r0, delta):
        if r0 == 0:
            return full + delta
        return jnp.concatenate([full[:r0], full[r0:] + delta], axis=0)

    def blocks(starts, state, row0s, masks):
        jobs = [(p, u) for p in range(npairs) for u in range(len(starts))]
        lanes = [slice(p * LANES, (p + 1) * LANES) for p in range(npairs)]
        kcs = [per_head_rows(k_ref[0, pl.ds(starts[u], tk), lanes[p]]) for p, u in jobs]
        vcs = [per_head_rows(v_ref[0, pl.ds(starts[u], tk), lanes[p]]) for p, u in jobs]
        zs = [_dot_nt(q_ref[0, row0s[u]:, lanes[p]], kc) for (p, u), kc in zip(jobs, kcs)]
        sps = []
        for (p, u), z in zip(jobs, zs):
            sp = jnp.maximum(z, 0.0) + jnp.log(1.0 + jnp.exp2(-jnp.abs(z))) * LOG2E
            if masks[u] is not None:
                sp = jnp.where(masks[u], sp, 0.0)
            sps.append(sp.astype(BF16))
        css = [_dot(sp, tri2) for sp in sps]
        carries = [c for c, _ in state]
        pvs = []
        for (p, u), z, cs, vc in zip(jobs, zs, css, vcs):
            r0 = row0s[u]
            rows = tq - r0
            attn = jnp.exp2(jnp.minimum(z - cs, 0.0) - carries[p][r0:])
            if masks[u] is not None:
                attn = jnp.where(masks[u], attn, 0.0)
            pvs.append(_dot(attn.astype(BF16), vc))
            tot = jnp.concatenate([jnp.broadcast_to(cs[:, 0:1], (rows, tk)),
                                   jnp.broadcast_to(cs[:, tk:tk + 1], (rows, tk))], axis=1)
            carries[p] = add_rows(carries[p], r0, tot)
        accs = [a for _, a in state]
        for (p, u), pv in zip(jobs, pvs):
            accs[p] = add_rows(accs[p], row0s[u], pv)
        return tuple(zip(carries, accs))

    state = tuple((jnp.zeros((tq, 2 * tk), F32), jnp.zeros((tq, LANES), F32)) for _ in range(npairs))
    for g in range(nsub // nb):
        subs = [nsub - 1 - g * nb - u for u in range(nb)]
        starts = [pl.multiple_of(qi * tq + sub * tk, tk) for sub in subs]
        masks = []
        for sub in subs:
            rows = tq - sub * tk
            qpos = lax.broadcasted_iota(jnp.int32, (rows, 2 * tk), 0)
            col = lax.broadcasted_iota(jnp.int32, (rows, 2 * tk), 1)
            masks.append(jnp.where(col >= tk, col - tk, col) < qpos)
        state = blocks(starts, state, [sub * tk for sub in subs], masks)

    def min_carry(st):
        m = st[0][0]
        for c_, _ in st[1:]:
            m = jnp.minimum(m, c_)
        return jnp.min(jnp.minimum(m[:, :tk], m[:, tk:]))

    n_steps = qi * (nsub // nb)

    def cond(loop):
        i, _, cmin = loop
        return (i < n_steps) & (cmin < SB_DEAD)

    def body(loop):
        i, st, _ = loop
        base = qi * tq - (i + 1) * (nb * tk)
        starts = [pl.multiple_of(base + (nb - 1 - u) * tk, tk) for u in range(nb)]
        st = blocks(starts, st, [0] * nb, [None] * nb)
        return i + 1, st, min_carry(st)

    _, state, _ = lax.while_loop(cond, body, (jnp.int32(0), state, min_carry(state)))
    o_ref[0] = jnp.concatenate([a for _, a in state], axis=1).astype(o_ref.dtype)


def _sbattn(qkv, tq=256, tk=128, nb=2, npairs=4):
    b, t, ds3 = qkv.shape
    ds = ds3 // 3
    w = npairs * LANES
    ng = ds // w
    return pl.pallas_call(
        functools.partial(_sbattn_kernel, tq=tq, tk=tk, nb=nb, npairs=npairs),
        out_shape=jax.ShapeDtypeStruct((b, t, ds), BF16),
        grid=(b, ng, t // tq),
        in_specs=[pl.BlockSpec((1, tq, w), lambda i, h, j: (i, j, h)),
                  pl.BlockSpec((1, t, w), lambda i, h, j: (i, 0, ng + h)),
                  pl.BlockSpec((1, t, w), lambda i, h, j: (i, 0, 2 * ng + h))],
        out_specs=pl.BlockSpec((1, tq, w), lambda i, h, j: (i, j, h)),
        compiler_params=_cparams(3),
        name="sbattn",
    )(qkv, qkv, qkv)


def _outproj_kernel(yr_ref, ys_ref, w_ref, x_ref, gt_ref, g2_ref, sc_ref, sh_ref, x1_ref, h2_ref):
    dr = yr_ref.shape[1]
    mix = _dot(yr_ref[...], w_ref[:dr, :]) + _dot(ys_ref[...], w_ref[dr:, :])
    x1 = x_ref[...] + gt_ref[0] * mix
    x1_ref[...] = x1
    ms = jnp.mean(x1 * x1, axis=-1, keepdims=True)
    y = x1 * lax.rsqrt(ms + RMS_EPS) * g2_ref[...]
    h2_ref[...] = (y * (1.0 + sc_ref[0]) + sh_ref[0]).astype(BF16)


def _outproj(yr, ys, w_out, x2, gt1, gain2, sc2, sh2, seq, tm=512):
    m, d = x2.shape
    dr = yr.shape[1]
    ds = ys.shape[1]
    per_seq = seq // tm
    mod = pl.BlockSpec((1, 1, d), lambda i: (i // per_seq, 0, 0))
    return pl.pallas_call(
        _outproj_kernel,
        out_shape=(jax.ShapeDtypeStruct((m, d), F32), jax.ShapeDtypeStruct((m, d), BF16)),
        grid=(m // tm,),
        in_specs=[pl.BlockSpec((tm, dr), lambda i: (i, 0)),
                  pl.BlockSpec((tm, ds), lambda i: (i, 0)),
                  pl.BlockSpec(w_out.shape, lambda i: (0, 0)),
                  pl.BlockSpec((tm, d), lambda i: (i, 0)),
                  mod,
                  pl.BlockSpec((1, d), lambda i: (0, 0)),
                  mod, mod],
        out_specs=(pl.BlockSpec((tm, d), lambda i: (i, 0)), pl.BlockSpec((tm, d), lambda i: (i, 0))),
        compiler_params=_cparams(1),
        name="outproj",
    )(yr, ys, w_out, x2, gt1, gain2, sc2, sh2)


def _ffn_kernel(h_ref, wg_ref, wu_ref, wd_ref, x_ref, gt_ref, o_ref, acc_scr):
    j = pl.program_id(1)

    @pl.when(j == 0)
    def _():
        acc_scr[...] = jnp.zeros_like(acc_scr)

    h = h_ref[...]
    gate = _dot(h, wg_ref[...])
    up = _dot(h, wu_ref[...])
    act = (gate * _sigmoid(gate) * up).astype(BF16)
    acc_scr[...] += _dot(act, wd_ref[...])

    @pl.when(j == pl.num_programs(1) - 1)
    def _():
        o_ref[...] = x_ref[...] + gt_ref[0] * acc_scr[...]


def _ffn(h2, w_gu, w_down, x1, gt2, seq, tm=512, tf=512):
    m, d = h2.shape
    dff = w_down.shape[0]
    nf = dff // tf
    per_seq = seq // tm
    return pl.pallas_call(
        _ffn_kernel,
        out_shape=jax.ShapeDtypeStruct((m, d), F32),
        grid=(m // tm, nf),
        in_specs=[pl.BlockSpec((tm, d), lambda i, j: (i, 0)),
                  pl.BlockSpec((d, tf), lambda i, j: (0, j)),
                  pl.BlockSpec((d, tf), lambda i, j: (0, j + nf)),
                  pl.BlockSpec((tf, d), lambda i, j: (j, 0)),
                  pl.BlockSpec((tm, d), lambda i, j: (i, 0)),
                  pl.BlockSpec((1, 1, d), lambda i, j: (i // per_seq, 0, 0))],
        out_specs=pl.BlockSpec((tm, d), lambda i, j: (i, 0)),
        scratch_shapes=[pltpu.VMEM((tm, d), F32)],
        compiler_params=_cparams(2),
        name="ffn",
    )(h2, w_gu, w_gu, w_down, x1, gt2)


def _pad_to(a, n, axis):
    pad = [(0, 0)] * a.ndim
    pad[axis] = (0, n - a.shape[axis])
    return jnp.pad(a, pad)


def _layer(x, c_pad, w_ada, b_ada, norm1_gain, norm2_gain, w_in, mu_rkv, mu_w, mu_a, mu_g, w0, w1,
           w2, a0, a1, a2, g1, g2, k_k, k_a, r_k, ln_x_gain, ln_x_bias, q_norm_gain, k_norm_gain,
           w_out, w_gate_up, w_down):
    b, t, d = x.shape
    dr = w0.shape[0]
    ds = d - dr
    n_rwkv = dr // HEAD_DIM
    n_sb = ds // HEAD_DIM
    row = lambda a: a.reshape(1, -1)

    mod = _ada(c_pad, w_ada, row(b_ada))[:b]
    sh1, sc1, gt1, sh2, sc2, gt2 = [m.reshape(b, 1, d) for m in jnp.split(mod, 6, axis=-1)]

    nw = -(-w1.shape[1] // LANES) * LANES
    na = -(-a1.shape[1] // LANES) * LANES
    ng = -(-g1.shape[1] // LANES) * LANES
    w_l = jnp.concatenate([_pad_to(w1, nw, 1), _pad_to(a1, na, 1), _pad_to(g1, ng, 1)], axis=1)
    mu_l = jnp.concatenate([jnp.broadcast_to(mu_w[:, None], (d, nw)),
                            jnp.broadcast_to(mu_a[:, None], (d, na)),
                            jnp.broadcast_to(mu_g[:, None], (d, ng))], axis=1)
    w_lora = _fold(w_l, mu_l)
    lora_w = w_lora.shape[1]
    assert dr == ds == lora_w, "column tiles of the input projection are one head group wide"
    w_in16 = w_in.astype(BF16)
    w_ext = jnp.concatenate([w_in16[:, :3 * dr], w_lora, w_in16[:, 3 * dr:]], axis=1)

    hsum = jnp.kron(jnp.eye(MXU_N // HEAD_DIM, dtype=F32),
                    jnp.ones((HEAD_DIM, HEAD_DIM), F32)).astype(BF16)
    x2 = x.reshape(b * t, d)
    p, qkv = _inproj(x2, row(norm1_gain), sc1, sh1, w_ext, row(jnp.tile(q_norm_gain, n_sb)),
                     row(jnp.tile(k_norm_gain, n_sb)), hsum, t, 4)
    p3 = p.reshape(b, t, -1)
    y_rwkv = _rwkv(p3, row(mu_rkv), row(w0), _pad_to(w2, nw, 0).astype(BF16), row(a0),
                   _pad_to(a2, na, 0).astype(BF16), _pad_to(g2, ng, 0).astype(BF16),
                   row(k_k), row(k_a), row(r_k), row(ln_x_gain), row(ln_x_bias), hsum,
                   n_rwkv, lora_w)

    y_sb = _sbattn(qkv.reshape(b, t, -1))

    x1, h2 = _outproj(y_rwkv.reshape(b * t, dr), y_sb.reshape(b * t, ds), w_out.astype(BF16), x2,
                      gt1, row(norm2_gain), sc2, sh2, t)
    out = _ffn(h2, w_gate_up.astype(BF16), w_down.astype(BF16), x1, gt2, t)
    return out.reshape(b, t, d)


def kernel(x, c, w_ada, b_ada, norm1_gain, norm2_gain, w_in, mu_rkv, mu_w, mu_a, mu_g, w0, w1, w2,
           a0, a1, a2, g1, g2, k_k, k_a, r_k, ln_x_gain, ln_x_bias, q_norm_gain, k_norm_gain, w_out,
           w_gate_up, w_down):
    depth = w_ada.shape[0]
    c_pad = _pad_to(c, 8, 0)
    for l in range(depth):
        x = _layer(x, c_pad, w_ada[l], b_ada[l], norm1_gain[l], norm2_gain[l], w_in[l], mu_rkv[l],
                   mu_w[l], mu_a[l], mu_g[l], w0[l], w1[l], w2[l], a0[l], a1[l], a2[l], g1[l],
                   g2[l], k_k[l], k_a[l], r_k[l].reshape(-1), ln_x_gain[l], ln_x_bias[l],
                   q_norm_gain[l], k_norm_gain[l], w_out[l], w_gate_up[l], w_down[l])
    return x
```

```python
import functools
import math

import jax
import jax.numpy as jnp
from jax import lax
from jax.experimental import pallas as pl
from jax.experimental.pallas import tpu as pltpu

F32 = jnp.float32
BF16 = jnp.bfloat16

HEAD_DIM = 64
RMS_EPS = 1e-6
GN_EPS = 64e-5
L2_EPS = 1e-12
LOG2E = math.log2(math.e)
SB_DEAD = 150.0
LANES = 128
MXU_N = 256
CHUNK = 64
RWKV_CHUNKS = 2
VMEM_LIMIT = 48 * 1024 * 1024
VMEM_LIMIT_BIG = 56 * 1024 * 1024


def _cparams(n_axes, vmem_limit=VMEM_LIMIT):
    return pltpu.CompilerParams(dimension_semantics=("arbitrary",) * n_axes,
                                vmem_limit_bytes=vmem_limit)


def _sigmoid(x):
    return 1.0 / (1.0 + jnp.exp(-x))


def _softplus(x):
    return jnp.maximum(x, 0.0) + jnp.log(1.0 + jnp.exp(-jnp.abs(x)))


def _dot(x, y):
    return jnp.dot(x, y, preferred_element_type=F32)


def _dot_nt(x, y):
    return lax.dot_general(x, y, (((1,), (1,)), ((), ())), preferred_element_type=F32)


def _split(x):
    hi = x.astype(BF16)
    lo = (x - hi.astype(F32)).astype(BF16)
    return hi, lo


def _mmb(x, y, nt=False):
    return (_dot_nt if nt else _dot)(x.astype(BF16), y.astype(BF16))


def _headsum(t, hs):
    g = hs.shape[0]
    return jnp.concatenate([_mm2_exact_rhs(t[:, i:i + g], hs) for i in range(0, t.shape[1], g)],
                           axis=1)


def _mm2_exact_rhs(x, y_bf16):
    xh, xl = _split(x)
    return _dot(xh, y_bf16) + _dot(xl, y_bf16)


def _ada_kernel(c_ref, w_ref, b_ref, o_ref):
    c = c_ref[...]
    ca = c * _sigmoid(c)
    o_ref[...] = _dot(ca.astype(BF16), w_ref[...].astype(BF16)) + b_ref[...]


def _ada(c_pad, w_ada, b_ada, tn=1024):
    m, d = c_pad.shape
    n = w_ada.shape[1]
    return pl.pallas_call(
        _ada_kernel,
        out_shape=jax.ShapeDtypeStruct((m, n), F32),
        grid=(n // tn,),
        in_specs=[pl.BlockSpec((m, d), lambda j: (0, 0)),
                  pl.BlockSpec((d, tn), lambda j: (0, j)),
                  pl.BlockSpec((1, tn), lambda j: (0, j))],
        out_specs=pl.BlockSpec((m, tn), lambda j: (0, j)),
        compiler_params=_cparams(1),
        name="ada",
    )(c_pad, w_ada, b_ada)


def _fold_kernel(w1_ref, a1_ref, g1_ref, muw_ref, mua_ref, mug_ref, o_ref, *, offsets):
    o_ref[...] = jnp.zeros_like(o_ref)
    half = o_ref.shape[1] // 2
    for w_ref, mu_ref, off in zip((w1_ref, a1_ref, g1_ref), (muw_ref, mua_ref, mug_ref), offsets):
        w = w_ref[...]
        mu = mu_ref[...]
        n = w.shape[1]
        o_ref[:, off:off + n] = (w * (1.0 - mu)).astype(BF16)
        o_ref[:, half + off:half + off + n] = (w * mu).astype(BF16)


def _fold(w1, a1, g1, mu_w, mu_a, mu_g, offsets, half):
    d = w1.shape[0]
    col = lambda m: m.reshape(d, 1)
    return pl.pallas_call(
        functools.partial(_fold_kernel, offsets=offsets),
        out_shape=jax.ShapeDtypeStruct((d, 2 * half), BF16),
        compiler_params=pltpu.CompilerParams(vmem_limit_bytes=VMEM_LIMIT),
        name="fold",
    )(w1, a1, g1, col(mu_w), col(mu_a), col(mu_g))


INPROJ_SLICES = 4


def _inproj_kernel(x_ref, g_ref, sc_ref, sh_ref, w_ref, wl_ref, qg_ref, kg_ref, hs_ref, p_ref, s_ref,
                   h_scr, *, n_rkv):
    i = pl.program_id(0)
    j = pl.program_id(1)
    rows = x_ref.shape[0]

    def prepare():
        x = x_ref[...]
        ms = jnp.mean(x * x, axis=-1, keepdims=True)
        y = x * lax.rsqrt(ms + RMS_EPS) * g_ref[...]
        sl = jnp.clip(j - 1, 0, INPROJ_SLICES - 1)
        start = pl.multiple_of(sl * rows, rows)
        h_scr[i % 2, pl.ds(start, rows), :] = (y * (1.0 + sc_ref[0]) + sh_ref[0]).astype(BF16)

    def h_cur():
        return h_scr[(i + 1) % 2]

    def norm(t, gain):
        ms = _headsum(t * t, hs_ref[...]) * (1.0 / HEAD_DIM)
        return t * lax.rsqrt(ms + RMS_EPS) * gain

    @pl.when(i == 0)
    def _():
        prepare()

    @pl.when((i > 0) & (j < n_rkv))
    def _():
        p_ref[...] = _dot(h_cur(), w_ref[...])
        prepare()

    @pl.when((i > 0) & (j == n_rkv))
    def _():
        p_ref[...] = _dot(h_cur(), wl_ref[...])
        prepare()

    @pl.when((i > 0) & (j == n_rkv + 1))
    def _():
        q = norm(_dot(h_cur(), w_ref[...]), qg_ref[...])
        s_ref[...] = (q * (LOG2E / math.sqrt(HEAD_DIM))).astype(BF16)
        prepare()

    @pl.when((i > 0) & (j == n_rkv + 2))
    def _():
        s_ref[...] = norm(_dot(h_cur(), w_ref[...]), kg_ref[...]).astype(BF16)
        prepare()

    @pl.when((i > 0) & (j == n_rkv + 3))
    def _():
        s_ref[...] = _dot(h_cur(), w_ref[...]).astype(BF16)
        prepare()


def _inproj(x2, gain, sc, sh, w_in16, w_lora, qg, kg, hsum, seq, n_rkv, tm=1024):
    m, d = x2.shape
    tn = qg.shape[1]
    assert w_lora.shape[1] == tn and w_in16.shape[1] == (n_rkv + 3) * tn
    nrow = m // tm
    nt = n_rkv + 4
    rows = tm // INPROJ_SLICES
    per_seq = seq // tm
    const = lambda a: pl.BlockSpec(a.shape, lambda i, j: (0,) * a.ndim)
    batch = lambda i, j: (jnp.minimum(i, nrow - 1) // per_seq, 0, 0)
    prev_row = lambda i: jnp.maximum(i - 1, 0)
    return pl.pallas_call(
        functools.partial(_inproj_kernel, n_rkv=n_rkv),
        out_shape=(jax.ShapeDtypeStruct((m, (n_rkv + 1) * tn), F32),
                   jax.ShapeDtypeStruct((m, 3 * tn), BF16)),
        grid=(nrow + 1, nt),
        in_specs=[pl.BlockSpec((rows, d), lambda i, j: (jnp.minimum(
                      INPROJ_SLICES * i + jnp.clip(j - 1, 0, INPROJ_SLICES - 1),
                      INPROJ_SLICES * nrow - 1), 0)),
                  const(gain),
                  pl.BlockSpec((1, 1, d), batch),
                  pl.BlockSpec((1, 1, d), batch),
                  pl.BlockSpec((d, tn), lambda i, j: (0, jnp.where(j < n_rkv, j, jnp.maximum(j - 1, n_rkv - 1)))),
                  const(w_lora), const(qg), const(kg), const(hsum)],
        out_specs=(pl.BlockSpec((tm, tn), lambda i, j: (prev_row(i), jnp.where(i == 0, 0, jnp.minimum(j, n_rkv)))),
                   pl.BlockSpec((tm, tn), lambda i, j: (prev_row(i), jnp.where(i == 0, 0, jnp.maximum(j - n_rkv - 1, 0))))),
        scratch_shapes=[pltpu.VMEM((2, tm, d), BF16)],
        compiler_params=_cparams(2, VMEM_LIMIT_BIG),
        name="inproj",
    )(x2, gain, sc, sh, w_in16, w_lora, qg, kg, hsum)


def _shift_rows(cur, prev_row):
    rolled = pltpu.roll(cur, 1, axis=0)
    row = lax.broadcasted_iota(jnp.int32, cur.shape, 0)
    return jnp.where(row == 0, jnp.broadcast_to(prev_row, cur.shape), rolled)


_PREP = ("at", "rt", "bt", "kt", "v", "bh", "kh", "bonus", "g")


def _rwkv_kernel(p_ref, l_ref, mu_ref, w0_ref, w2_ref, a0_ref, a2_ref, g2_ref, kk_ref, ka_ref,
                 rk_ref, lng_ref, lnb_ref, hs_ref, o_ref, s_scr, prevp_scr, prevl_scr, y_scr,
                 prep_scr, gend_scr, *, n_heads):
    dr = n_heads * HEAD_DIM
    c = CHUNK
    nc = RWKV_CHUNKS
    d = HEAD_DIM
    heads = range(n_heads)
    units = [(ck, h) for ck in range(nc) for h in heads]
    sls = [slice(h * d, (h + 1) * d) for h in heads]
    rws = [slice(ck * c, (ck + 1) * c) for ck in range(nc)]

    @pl.when(pl.program_id(1) == 0)
    def _():
        s_scr[...] = jnp.zeros_like(s_scr)
        prevp_scr[...] = jnp.zeros_like(prevp_scr)
        prevl_scr[...] = jnp.zeros_like(prevl_scr)
        prep_scr[...] = jnp.zeros_like(prep_scr)
        gend_scr[...] = jnp.zeros_like(gend_scr)

    headsum = functools.partial(_headsum, hs=hs_ref[...])

    def prev(name, ck, h):
        i = _PREP.index(name)
        pair = prep_scr[i, rws[ck], (h // 2) * LANES:(h // 2 + 1) * LANES]
        return pair[:, (h % 2) * d:(h % 2 + 1) * d]

    rows2 = lax.broadcasted_iota(jnp.int32, (2 * c, 2 * c), 0)
    cols2 = lax.broadcasted_iota(jnp.int32, (2 * c, 2 * c), 1)
    tt = jnp.where(rows2 >= c, rows2 - c, rows2)
    ii = jnp.where(cols2 >= c, cols2 - c, cols2)
    quad_mask = tt + jnp.where(rows2 >= c, 1, 0) > ii

    m4 = {}
    for un in units:
        ar = jnp.concatenate([prev("at", *un), prev("rt", *un)], axis=0)
        bk = jnp.concatenate([prev("bt", *un), prev("kt", *un)], axis=0)
        m4[un] = jnp.where(quad_mask, _mmb(ar, bk, nt=True), 0.0)

    p = p_ref[0]
    tb = nc * c
    pshift = _shift_rows(p, prevp_scr[...])
    prevp_scr[...] = p[tb - 1:tb, :]
    pm = p + (pshift - p) * mu_ref[...]
    r = pm[:, :dr]
    k = pm[:, dr:2 * dr]
    v = pm[:, 2 * dr:]

    l = l_ref[0]
    half = l.shape[1] // 2
    lb = l[:, half:]
    lin = l[:, :half] + _shift_rows(lb, prevl_scr[...])
    prevl_scr[...] = lb[tb - 1:tb, :]
    nw = w2_ref.shape[0]
    na = a2_ref.shape[0]
    lw = jnp.tanh(lin[:, :nw])
    la = lin[:, nw:nw + na]
    lg = _sigmoid(lin[:, nw + na:])
    w_pre = w0_ref[...] + _dot(lw.astype(BF16), w2_ref[...])
    w_log = -_softplus(-w_pre) - 0.5
    logw = -jnp.exp(w_log)
    a_gate = _sigmoid(a0_ref[...] + _dot(la.astype(BF16), a2_ref[...]))
    g = _dot(lg.astype(BF16), g2_ref[...])
    kk = k * kk_ref[...]
    kk = kk * lax.rsqrt(headsum(kk * kk) + L2_EPS)
    k2 = k * (1.0 + (a_gate - 1.0) * ka_ref[...])
    avec = -kk
    bvec = kk * a_gate

    akv_yk = {un: _mmb(m4[un][:, c:], prev("v", *un)) for un in units}

    ti = lax.broadcasted_iota(jnp.int32, (tb, tb), 0)
    si = lax.broadcasted_iota(jnp.int32, (tb, tb), 1)
    same_chunk = sum(((ti >= ck * c) & (ti < (ck + 1) * c) & (si >= ck * c)) for ck in range(nc)) > 0
    tri = jnp.where(same_chunk & (ti >= si), 1.0, 0.0).astype(BF16)
    l1, l2 = _split(logw)
    l3 = (logw - l1.astype(F32) - l2.astype(F32)).astype(BF16)
    lgc = _dot(tri, l1) + (_dot(tri, l2) + _dot(tri, l3))
    lg_ends = [lgc[(ck + 1) * c - 1:(ck + 1) * c, :] for ck in range(nc)]
    lg_end = jnp.concatenate([jnp.broadcast_to(e, (c, dr)) for e in lg_ends], axis=0)
    ginv = jnp.exp(-lgc)
    gend = jnp.exp(lg_end - lgc)
    new = {"at": avec * jnp.exp(lgc - logw), "rt": r * jnp.exp(lgc), "bt": bvec * ginv,
           "kt": k2 * ginv, "v": v, "bh": bvec * gend, "kh": k2 * gend, "g": g}
    new_gend = jnp.exp(jnp.concatenate(lg_ends, axis=0))

    zpad = jnp.zeros((c, d), F32)
    w = {un: jnp.concatenate([m4[un][:c, :c], zpad, prev("at", *un), akv_yk[un][:c]], axis=1)
         for un in units}
    for lvl in range(6):
        rhs = w if lvl < 5 else {un: w[un][:, 2 * d:] for un in units}
        prod = {un: _mmb(w[un][:, :d], rhs[un]) for un in units}
        if lvl < 5:
            w = {un: jnp.concatenate([prod[un][:, :2 * d], prod[un][:, 2 * d:] + w[un][:, 2 * d:]],
                                     axis=1) for un in units}
        else:
            w = {un: prod[un] + w[un][:, 2 * d:] for un in units}

    new["bonus"] = headsum(r * k2 * rk_ref[...]) * v

    g_end = gend_scr[...]
    s = [s_scr[h] for h in heads]
    for ck in range(nc):
        su = [_mmb(jnp.concatenate([w[ck, h][:, :d], prev("rt", ck, h)], axis=0), s[h], nt=True)
              for h in heads]
        u = [su[h][:c] + w[ck, h][:, d:] for h in heads]
        rbu = [_mmb(m4[ck, h][c:, :c], u[h]) for h in heads]
        upd = [_mmb(jnp.concatenate([u[h], prev("v", ck, h)], axis=0).T,
                    jnp.concatenate([prev("bh", ck, h), prev("kh", ck, h)], axis=0))
               for h in heads]
        for h in heads:
            y_scr[rws[ck], sls[h]] = su[h][c:] + rbu[h] + akv_yk[ck, h][c:]
        s = [s[h] * g_end[ck:ck + 1, sls[h]] + upd[h] for h in heads]
    for h in heads:
        s_scr[h] = s[h]

    y = y_scr[...]
    inv_n = 1.0 / HEAD_DIM
    mean = headsum(y) * inv_n
    yc = y - mean
    var = headsum(yc * yc) * inv_n
    yn = yc * lax.rsqrt(var + GN_EPS) * lng_ref[...] + lnb_ref[...]
    o_ref[0] = ((yn + prep_scr[_PREP.index("bonus")]) * prep_scr[_PREP.index("g")]).astype(o_ref.dtype)

    for i, name in enumerate(_PREP):
        prep_scr[i] = new[name]
    gend_scr[...] = new_gend


def _rwkv(p3, mu_rkv, w0, w2p, a0, a2p, g2p, k_k, k_a, r_k, ln_g, ln_b, hsum, n_heads, lora_w):
    b, t, _ = p3.shape
    dr = n_heads * HEAD_DIM
    c = CHUNK * RWKV_CHUNKS
    nt = t // c
    lora_blk = (3 * dr) // lora_w
    row = lambda n: pl.BlockSpec((1, n), lambda i, j: (0, 0))
    full = lambda a: pl.BlockSpec(a.shape, lambda i, j: (0, 0))
    return pl.pallas_call(
        functools.partial(_rwkv_kernel, n_heads=n_heads),
        out_shape=jax.ShapeDtypeStruct((b, t, dr), BF16),
        grid=(b, nt + 1),
        in_specs=[pl.BlockSpec((1, c, 3 * dr), lambda i, j: (i, jnp.minimum(j, nt - 1), 0)),
                  pl.BlockSpec((1, c, lora_w), lambda i, j: (i, jnp.minimum(j, nt - 1), lora_blk)),
                  row(3 * dr), row(dr), full(w2p), row(dr), full(a2p), full(g2p),
                  row(dr), row(dr), row(dr), row(dr), row(dr), full(hsum)],
        out_specs=pl.BlockSpec((1, c, dr), lambda i, j: (i, jnp.maximum(j - 1, 0), 0)),
        scratch_shapes=[pltpu.VMEM((n_heads, HEAD_DIM, HEAD_DIM), F32),
                        pltpu.VMEM((1, 3 * dr), F32),
                        pltpu.VMEM((1, lora_w // 2), F32),
                        pltpu.VMEM((c, dr), F32),
                        pltpu.VMEM((len(_PREP), c, dr), F32),
                        pltpu.VMEM((RWKV_CHUNKS, dr), F32)],
        compiler_params=_cparams(2),
        name="rwkv",
    )(p3, p3, mu_rkv, w0, w2p, a0, a2p, g2p, k_k, k_a, r_k, ln_g, ln_b, hsum)


def _sbattn_kernel(q_ref, k_ref, v_ref, o_ref, *, tq, tk, nb, npairs):
    qi = pl.program_id(2)
    nsub = tq // tk
    first = lax.broadcasted_iota(jnp.int32, (tk, LANES), 1) < HEAD_DIM
    rr = lax.broadcasted_iota(jnp.int32, (2 * tk, 2 * tk), 0)
    cc = lax.broadcasted_iota(jnp.int32, (2 * tk, 2 * tk), 1)
    tri2 = jnp.where((rr >= cc) & ((rr < tk) == (cc < tk)), 1.0, 0.0).astype(BF16)

    def per_head_rows(blk):
        zero = jnp.zeros_like(blk)
        return jnp.concatenate([jnp.where(first, blk, zero), jnp.where(first, zero, blk)], axis=0)

    def add_rows(full, r0, delta):
        if r0 == 0:
            return full + delta
        return jnp.concatenate([full[:r0], full[r0:] + delta], axis=0)

    def blocks(starts, state, row0s, masks):
        jobs = [(p, u) for p in range(npairs) for u in range(len(starts))]
        lanes = [slice(p * LANES, (p + 1) * LANES) for p in range(npairs)]
        kcs = [per_head_rows(k_ref[0, pl.ds(starts[u], tk), lanes[p]]) for p, u in jobs]
        vcs = [per_head_rows(v_ref[0, pl.ds(starts[u], tk), lanes[p]]) for p, u in jobs]
        zs = [_dot_nt(q_ref[0, row0s[u]:, lanes[p]], kc) for (p, u), kc in zip(jobs, kcs)]
        sps = []
        for (p, u), z in zip(jobs, zs):
            sp = jnp.maximum(z, 0.0) + jnp.log(1.0 + jnp.exp2(-jnp.abs(z))) * LOG2E
            if masks[u] is not None:
                sp = jnp.where(masks[u], sp, 0.0)
            sps.append(sp.astype(BF16))
        css = [_dot(sp, tri2) for sp in sps]
        carries = [c for c, _ in state]
        pvs = []
        for (p, u), z, cs, vc in zip(jobs, zs, css, vcs):
            r0 = row0s[u]
            rows = tq - r0
            attn = jnp.exp2(jnp.minimum(z - cs, 0.0) - carries[p][r0:])
            if masks[u] is not None:
                attn = jnp.where(masks[u], attn, 0.0)
            pvs.append(_dot(attn.astype(BF16), vc))
            tot = jnp.concatenate([jnp.broadcast_to(cs[:, 0:1], (rows, tk)),
                                   jnp.broadcast_to(cs[:, tk:tk + 1], (rows, tk))], axis=1)
            carries[p] = add_rows(carries[p], r0, tot)
        accs = [a for _, a in state]
        for (p, u), pv in zip(jobs, pvs):
            accs[p] = add_rows(accs[p], row0s[u], pv)
        return tuple(zip(carries, accs))

    state = tuple((jnp.zeros((tq, 2 * tk), F32), jnp.zeros((tq, LANES), F32)) for _ in range(npairs))
    for g in range(nsub // nb):
        subs = [nsub - 1 - g * nb - u for u in range(nb)]
        starts = [pl.multiple_of(qi * tq + sub * tk, tk) for sub in subs]
        masks = []
        for sub in subs:
            rows = tq - sub * tk
            qpos = lax.broadcasted_iota(jnp.int32, (rows, 2 * tk), 0)
            col = lax.broadcasted_iota(jnp.int32, (rows, 2 * tk), 1)
            masks.append(jnp.where(col >= tk, col - tk, col) < qpos)
        state = blocks(starts, state, [sub * tk for sub in subs], masks)

    def min_carry(st):
        m = st[0][0]
        for c_, _ in st[1:]:
            m = jnp.minimum(m, c_)
        return jnp.min(jnp.minimum(m[:, :tk], m[:, tk:]))

    n_steps = qi * (nsub // nb)

    def cond(loop):
        i, _, cmin = loop
        return (i < n_steps) & (cmin < SB_DEAD)

    def body(loop):
        i, st, _ = loop
        base = qi * tq - (i + 1) * (nb * tk)
        starts = [pl.multiple_of(base + (nb - 1 - u) * tk, tk) for u in range(nb)]
        st = blocks(starts, st, [0] * nb, [None] * nb)
        return i + 1, st, min_carry(st)

    _, state, _ = lax.while_loop(cond, body, (jnp.int32(0), state, min_carry(state)))
    o_ref[0] = jnp.concatenate([a for _, a in state], axis=1).astype(o_ref.dtype)


def _sbattn(qkv, tq=256, tk=128, nb=2, npairs=4):
    b, t, ds3 = qkv.shape
    ds = ds3 // 3
    w = npairs * LANES
    ng = ds // w
    return pl.pallas_call(
        functools.partial(_sbattn_kernel, tq=tq, tk=tk, nb=nb, npairs=npairs),
        out_shape=jax.ShapeDtypeStruct((b, t, ds), BF16),
        grid=(b, ng, t // tq),
        in_specs=[pl.BlockSpec((1, tq, w), lambda i, h, j: (i, j, h)),
                  pl.BlockSpec((1, t, w), lambda i, h, j: (i, 0, ng + h)),
                  pl.BlockSpec((1, t, w), lambda i, h, j: (i, 0, 2 * ng + h))],
        out_specs=pl.BlockSpec((1, tq, w), lambda i, h, j: (i, j, h)),
        compiler_params=_cparams(3),
        name="sbattn",
    )(qkv, qkv, qkv)


def _outproj_kernel(yr_ref, ys_ref, w_ref, x_ref, gt_ref, g2_ref, sc_ref, sh_ref, x1_ref, h2_ref):
    dr = yr_ref.shape[1]
    mix = _dot(yr_ref[...], w_ref[:dr, :]) + _dot(ys_ref[...], w_ref[dr:, :])
    x1 = x_ref[...] + gt_ref[0] * mix
    x1_ref[...] = x1
    ms = jnp.mean(x1 * x1, axis=-1, keepdims=True)
    y = x1 * lax.rsqrt(ms + RMS_EPS) * g2_ref[...]
    h2_ref[...] = (y * (1.0 + sc_ref[0]) + sh_ref[0]).astype(BF16)


def _outproj(yr, ys, w_out, x2, gt1, gain2, sc2, sh2, seq, tm=512):
    m, d = x2.shape
    dr = yr.shape[1]
    ds = ys.shape[1]
    per_seq = seq // tm
    mod = pl.BlockSpec((1, 1, d), lambda i: (i // per_seq, 0, 0))
    return pl.pallas_call(
        _outproj_kernel,
        out_shape=(jax.ShapeDtypeStruct((m, d), F32), jax.ShapeDtypeStruct((m, d), BF16)),
        grid=(m // tm,),
        in_specs=[pl.BlockSpec((tm, dr), lambda i: (i, 0)),
                  pl.BlockSpec((tm, ds), lambda i: (i, 0)),
                  pl.BlockSpec(w_out.shape, lambda i: (0, 0)),
                  pl.BlockSpec((tm, d), lambda i: (i, 0)),
                  mod,
                  pl.BlockSpec((1, d), lambda i: (0, 0)),
                  mod, mod],
        out_specs=(pl.BlockSpec((tm, d), lambda i: (i, 0)), pl.BlockSpec((tm, d), lambda i: (i, 0))),
        compiler_params=_cparams(1),
        name="outproj",
    )(yr, ys, w_out, x2, gt1, gain2, sc2, sh2)


def _ffn_kernel(h_ref, wg_ref, wu_ref, wd_ref, x_ref, gt_ref, o_ref, acc_scr):
    j = pl.program_id(1)

    @pl.when(j == 0)
    def _():
        acc_scr[...] = jnp.zeros_like(acc_scr)

    h = h_ref[...]
    gate = _dot(h, wg_ref[...])
    up = _dot(h, wu_ref[...])
    act = (gate * _sigmoid(gate) * up).astype(BF16)
    acc_scr[...] += _dot(act, wd_ref[...])

    @pl.when(j == pl.num_programs(1) - 1)
    def _():
        o_ref[...] = x_ref[...] + gt_ref[0] * acc_scr[...]


def _ffn(h2, w_gu, w_down, x1, gt2, seq, tm=512, tf=512):
    m, d = h2.shape
    dff = w_down.shape[0]
    nf = dff // tf
    per_seq = seq // tm
    return pl.pallas_call(
        _ffn_kernel,
        out_shape=jax.ShapeDtypeStruct((m, d), F32),
        grid=(m // tm, nf),
        in_specs=[pl.BlockSpec((tm, d), lambda i, j: (i, 0)),
                  pl.BlockSpec((d, tf), lambda i, j: (0, j)),
                  pl.BlockSpec((d, tf), lambda i, j: (0, j + nf)),
                  pl.BlockSpec((tf, d), lambda i, j: (j, 0)),
                  pl.BlockSpec((tm, d), lambda i, j: (i, 0)),
                  pl.BlockSpec((1, 1, d), lambda i, j: (i // per_seq, 0, 0))],
        out_specs=pl.BlockSpec((tm, d), lambda i, j: (i, 0)),
        scratch_shapes=[pltpu.VMEM((tm, d), F32)],
        compiler_params=_cparams(2),
        name="ffn",
    )(h2, w_gu, w_gu, w_down, x1, gt2)


def _pad_to(a, n, axis):
    pad = [(0, 0)] * a.ndim
    pad[axis] = (0, n - a.shape[axis])
    return jnp.pad(a, pad)


def _layer(x, c_pad, w_ada, b_ada, norm1_gain, norm2_gain, w_in, mu_rkv, mu_w, mu_a, mu_g, w0, w1,
           w2, a0, a1, a2, g1, g2, k_k, k_a, r_k, ln_x_gain, ln_x_bias, q_norm_gain, k_norm_gain,
           w_out, w_gate_up, w_down):
    b, t, d = x.shape
    dr = w0.shape[0]
    ds = d - dr
    n_rwkv = dr // HEAD_DIM
    n_sb = ds // HEAD_DIM
    row = lambda a: a.reshape(1, -1)

    mod = _ada(c_pad, w_ada, row(b_ada))[:b]
    sh1, sc1, gt1, sh2, sc2, gt2 = [m.reshape(b, 1, d) for m in jnp.split(mod, 6, axis=-1)]

    nw = -(-w1.shape[1] // LANES) * LANES
    na = -(-a1.shape[1] // LANES) * LANES
    ng = -(-g1.shape[1] // LANES) * LANES
    w_lora = _fold(w1, a1, g1, mu_w, mu_a, mu_g, (0, nw, nw + na), nw + na + ng)
    lora_w = w_lora.shape[1]
    assert dr == ds == lora_w, "column tiles of the input projection are one head group wide"

    hsum = jnp.kron(jnp.eye(MXU_N // HEAD_DIM, dtype=F32),
                    jnp.ones((HEAD_DIM, HEAD_DIM), F32)).astype(BF16)
    x2 = x.reshape(b * t, d)
    p, qkv = _inproj(x2, row(norm1_gain), sc1, sh1, w_in.astype(BF16), w_lora,
                     row(jnp.tile(q_norm_gain, n_sb)), row(jnp.tile(k_norm_gain, n_sb)), hsum, t,
                     3 * dr // lora_w)
    p3 = p.reshape(b, t, -1)
    y_rwkv = _rwkv(p3, row(mu_rkv), row(w0), _pad_to(w2, nw, 0).astype(BF16), row(a0),
                   _pad_to(a2, na, 0).astype(BF16), _pad_to(g2, ng, 0).astype(BF16),
                   row(k_k), row(k_a), row(r_k), row(ln_x_gain), row(ln_x_bias), hsum,
                   n_rwkv, lora_w)

    y_sb = _sbattn(qkv.reshape(b, t, -1))

    x1, h2 = _outproj(y_rwkv.reshape(b * t, dr), y_sb.reshape(b * t, ds), w_out.astype(BF16), x2,
                      gt1, row(norm2_gain), sc2, sh2, t)
    out = _ffn(h2, w_gate_up.astype(BF16), w_down.astype(BF16), x1, gt2, t)
    return out.reshape(b, t, d)


def kernel(x, c, w_ada, b_ada, norm1_gain, norm2_gain, w_in, mu_rkv, mu_w, mu_a, mu_g, w0, w1, w2,
           a0, a1, a2, g1, g2, k_k, k_a, r_k, ln_x_gain, ln_x_bias, q_norm_gain, k_norm_gain, w_out,
           w_gate_up, w_down):
    depth = w_ada.shape[0]
    c_pad = _pad_to(c, 8, 0)
    for l in range(depth):
        x = _layer(x, c_pad, w_ada[l], b_ada[l], norm1_gain[l], norm2_gain[l], w_in[l], mu_rkv[l],
                   mu_w[l], mu_a[l], mu_g[l], w0[l], w1[l], w2[l], a0[l], a1[l], a2[l], g1[l],
                   g2[l], k_k[l], k_a[l], r_k[l].reshape(-1), ln_x_gain[l], ln_x_bias[l],
                   q_norm_gain[l], k_norm_gain[l], w_out[l], w_gate_up[l], w_down[l])
    return x
```

```python
import functools
import math

import jax
import jax.numpy as jnp
from jax import lax
from jax.experimental import pallas as pl
from jax.experimental.pallas import tpu as pltpu

F32 = jnp.float32
BF16 = jnp.bfloat16

HEAD_DIM = 64
RMS_EPS = 1e-6
GN_EPS = 64e-5
L2_EPS = 1e-12
LOG2E = math.log2(math.e)
SB_DEAD = 150.0
LANES = 128
MXU_N = 256
CHUNK = 64
RWKV_CHUNKS = 2
VMEM_LIMIT = 48 * 1024 * 1024
VMEM_LIMIT_BIG = 56 * 1024 * 1024


def _cparams(n_axes, vmem_limit=VMEM_LIMIT):
    return pltpu.CompilerParams(dimension_semantics=("arbitrary",) * n_axes,
                                vmem_limit_bytes=vmem_limit)


def _sigmoid(x):
    return 1.0 / (1.0 + jnp.exp(-x))


def _softplus(x):
    return jnp.maximum(x, 0.0) + jnp.log(1.0 + jnp.exp(-jnp.abs(x)))


def _dot(x, y):
    return jnp.dot(x, y, preferred_element_type=F32)


def _dot_nt(x, y):
    return lax.dot_general(x, y, (((1,), (1,)), ((), ())), preferred_element_type=F32)


def _split(x):
    hi = x.astype(BF16)
    lo = (x - hi.astype(F32)).astype(BF16)
    return hi, lo


def _mmb(x, y, nt=False):
    return (_dot_nt if nt else _dot)(x.astype(BF16), y.astype(BF16))


def _headsum(t, hs):
    g = hs.shape[0]
    return jnp.concatenate([_dot(t[:, i:i + g].astype(BF16), hs) for i in range(0, t.shape[1], g)],
                           axis=1)


def _ada_kernel(c_ref, w_ref, b_ref, o_ref):
    c = c_ref[...]
    ca = c * _sigmoid(c)
    o_ref[...] = _dot(ca.astype(BF16), w_ref[...].astype(BF16)) + b_ref[...]


def _ada(c_pad, w_ada, b_ada, tn=1024):
    m, d = c_pad.shape
    n = w_ada.shape[1]
    return pl.pallas_call(
        _ada_kernel,
        out_shape=jax.ShapeDtypeStruct((m, n), F32),
        grid=(n // tn,),
        in_specs=[pl.BlockSpec((m, d), lambda j: (0, 0)),
                  pl.BlockSpec((d, tn), lambda j: (0, j)),
                  pl.BlockSpec((1, tn), lambda j: (0, j))],
        out_specs=pl.BlockSpec((m, tn), lambda j: (0, j)),
        compiler_params=_cparams(1),
        name="ada",
    )(c_pad, w_ada, b_ada)


def _fold_kernel(w1_ref, a1_ref, g1_ref, muw_ref, mua_ref, mug_ref, o_ref, *, offsets):
    o_ref[...] = jnp.zeros_like(o_ref)
    half = o_ref.shape[1] // 2
    for w_ref, mu_ref, off in zip((w1_ref, a1_ref, g1_ref), (muw_ref, mua_ref, mug_ref), offsets):
        w = w_ref[...]
        mu = mu_ref[...]
        n = w.shape[1]
        o_ref[:, off:off + n] = (w * (1.0 - mu)).astype(BF16)
        o_ref[:, half + off:half + off + n] = (w * mu).astype(BF16)


def _fold(w1, a1, g1, mu_w, mu_a, mu_g, offsets, half):
    d = w1.shape[0]
    col = lambda m: m.reshape(d, 1)
    return pl.pallas_call(
        functools.partial(_fold_kernel, offsets=offsets),
        out_shape=jax.ShapeDtypeStruct((d, 2 * half), BF16),
        compiler_params=pltpu.CompilerParams(vmem_limit_bytes=VMEM_LIMIT),
        name="fold",
    )(w1, a1, g1, col(mu_w), col(mu_a), col(mu_g))


INPROJ_SLICES = 4


def _inproj_kernel(x_ref, g_ref, sc_ref, sh_ref, w_ref, wl_ref, qg_ref, kg_ref, hs_ref, p_ref, s_ref,
                   h_scr, *, n_rkv):
    i = pl.program_id(0)
    j = pl.program_id(1)
    rows = x_ref.shape[0]

    def prepare():
        x = x_ref[...]
        ms = jnp.mean(x * x, axis=-1, keepdims=True)
        y = x * lax.rsqrt(ms + RMS_EPS) * g_ref[...]
        sl = jnp.clip(j - 1, 0, INPROJ_SLICES - 1)
        start = pl.multiple_of(sl * rows, rows)
        h_scr[i % 2, pl.ds(start, rows), :] = (y * (1.0 + sc_ref[0]) + sh_ref[0]).astype(BF16)

    def h_cur():
        return h_scr[(i + 1) % 2]

    def norm(t, gain):
        ms = _headsum(t * t, hs_ref[...]) * (1.0 / HEAD_DIM)
        return t * lax.rsqrt(ms + RMS_EPS) * gain

    @pl.when(i == 0)
    def _():
        prepare()

    @pl.when((i > 0) & (j < n_rkv))
    def _():
        p_ref[...] = _dot(h_cur(), w_ref[...])
        prepare()

    @pl.when((i > 0) & (j == n_rkv))
    def _():
        p_ref[...] = _dot(h_cur(), wl_ref[...])
        prepare()

    @pl.when((i > 0) & (j == n_rkv + 1))
    def _():
        q = norm(_dot(h_cur(), w_ref[...]), qg_ref[...])
        s_ref[...] = (q * (LOG2E / math.sqrt(HEAD_DIM))).astype(BF16)
        prepare()

    @pl.when((i > 0) & (j == n_rkv + 2))
    def _():
        s_ref[...] = norm(_dot(h_cur(), w_ref[...]), kg_ref[...]).astype(BF16)
        prepare()

    @pl.when((i > 0) & (j == n_rkv + 3))
    def _():
        s_ref[...] = _dot(h_cur(), w_ref[...]).astype(BF16)
        prepare()


def _inproj(x2, gain, sc, sh, w_in16, w_lora, qg, kg, hsum, seq, n_rkv, tm=1024):
    m, d = x2.shape
    tn = qg.shape[1]
    assert w_lora.shape[1] == tn and w_in16.shape[1] == (n_rkv + 3) * tn
    nrow = m // tm
    nt = n_rkv + 4
    rows = tm // INPROJ_SLICES
    per_seq = seq // tm
    const = lambda a: pl.BlockSpec(a.shape, lambda i, j: (0,) * a.ndim)
    batch = lambda i, j: (jnp.minimum(i, nrow - 1) // per_seq, 0, 0)
    prev_row = lambda i: jnp.maximum(i - 1, 0)
    return pl.pallas_call(
        functools.partial(_inproj_kernel, n_rkv=n_rkv),
        out_shape=(jax.ShapeDtypeStruct((m, (n_rkv + 1) * tn), F32),
                   jax.ShapeDtypeStruct((m, 3 * tn), BF16)),
        grid=(nrow + 1, nt),
        in_specs=[pl.BlockSpec((rows, d), lambda i, j: (jnp.minimum(
                      INPROJ_SLICES * i + jnp.clip(j - 1, 0, INPROJ_SLICES - 1),
                      INPROJ_SLICES * nrow - 1), 0)),
                  const(gain),
                  pl.BlockSpec((1, 1, d), batch),
                  pl.BlockSpec((1, 1, d), batch),
                  pl.BlockSpec((d, tn), lambda i, j: (0, jnp.where(j < n_rkv, j, jnp.maximum(j - 1, n_rkv - 1)))),
                  const(w_lora), const(qg), const(kg), const(hsum)],
        out_specs=(pl.BlockSpec((tm, tn), lambda i, j: (prev_row(i), jnp.where(i == 0, 0, jnp.minimum(j, n_rkv)))),
                   pl.BlockSpec((tm, tn), lambda i, j: (prev_row(i), jnp.where(i == 0, 0, jnp.maximum(j - n_rkv - 1, 0))))),
        scratch_shapes=[pltpu.VMEM((2, tm, d), BF16)],
        compiler_params=_cparams(2, VMEM_LIMIT_BIG),
        name="inproj",
    )(x2, gain, sc, sh, w_in16, w_lora, qg, kg, hsum)


def _shift_rows(cur, prev_row):
    rolled = pltpu.roll(cur, 1, axis=0)
    row = lax.broadcasted_iota(jnp.int32, cur.shape, 0)
    return jnp.where(row == 0, jnp.broadcast_to(prev_row, cur.shape), rolled)


_PREP = ("at", "rt", "bt", "kt", "v", "bh", "kh", "bonus", "g")


def _rwkv_kernel(p_ref, l_ref, mu_ref, w0_ref, w2_ref, a0_ref, a2_ref, g2_ref, kk_ref, ka_ref,
                 rk_ref, lng_ref, lnb_ref, hs_ref, o_ref, s_scr, prevp_scr, prevl_scr, y_scr,
                 prep_scr, gend_scr, *, n_heads):
    dr = n_heads * HEAD_DIM
    c = CHUNK
    nc = RWKV_CHUNKS
    d = HEAD_DIM
    heads = range(n_heads)
    units = [(ck, h) for ck in range(nc) for h in heads]
    sls = [slice(h * d, (h + 1) * d) for h in heads]
    rws = [slice(ck * c, (ck + 1) * c) for ck in range(nc)]

    @pl.when(pl.program_id(1) == 0)
    def _():
        s_scr[...] = jnp.zeros_like(s_scr)
        prevp_scr[...] = jnp.zeros_like(prevp_scr)
        prevl_scr[...] = jnp.zeros_like(prevl_scr)
        prep_scr[...] = jnp.zeros_like(prep_scr)
        gend_scr[...] = jnp.zeros_like(gend_scr)

    headsum = functools.partial(_headsum, hs=hs_ref[...])

    def prev(name, ck, h):
        i = _PREP.index(name)
        pair = prep_scr[i, rws[ck], (h // 2) * LANES:(h // 2 + 1) * LANES]
        return pair[:, (h % 2) * d:(h % 2 + 1) * d]

    rows2 = lax.broadcasted_iota(jnp.int32, (2 * c, 2 * c), 0)
    cols2 = lax.broadcasted_iota(jnp.int32, (2 * c, 2 * c), 1)
    tt = jnp.where(rows2 >= c, rows2 - c, rows2)
    ii = jnp.where(cols2 >= c, cols2 - c, cols2)
    quad_mask = tt + jnp.where(rows2 >= c, 1, 0) > ii

    m4 = {}
    for un in units:
        ar = jnp.concatenate([prev("at", *un), prev("rt", *un)], axis=0)
        bk = jnp.concatenate([prev("bt", *un), prev("kt", *un)], axis=0)
        m4[un] = jnp.where(quad_mask, _mmb(ar, bk, nt=True), 0.0)

    p = p_ref[0]
    tb = nc * c
    pshift = _shift_rows(p, prevp_scr[...])
    prevp_scr[...] = p[tb - 1:tb, :]
    pm = p + (pshift - p) * mu_ref[...]
    r = pm[:, :dr]
    k = pm[:, dr:2 * dr]
    v = pm[:, 2 * dr:]

    l = l_ref[0]
    half = l.shape[1] // 2
    lb = l[:, half:]
    lin = l[:, :half] + _shift_rows(lb, prevl_scr[...])
    prevl_scr[...] = lb[tb - 1:tb, :]
    nw = w2_ref.shape[0]
    na = a2_ref.shape[0]
    lw = jnp.tanh(lin[:, :nw])
    la = lin[:, nw:nw + na]
    lg = _sigmoid(lin[:, nw + na:])
    w_pre = w0_ref[...] + _dot(lw.astype(BF16), w2_ref[...])
    w_log = -_softplus(-w_pre) - 0.5
    logw = -jnp.exp(w_log)
    a_gate = _sigmoid(a0_ref[...] + _dot(la.astype(BF16), a2_ref[...]))
    g = _dot(lg.astype(BF16), g2_ref[...])
    kk = k * kk_ref[...]
    kk = kk * lax.rsqrt(headsum(kk * kk) + L2_EPS)
    k2 = k * (1.0 + (a_gate - 1.0) * ka_ref[...])
    avec = -kk
    bvec = kk * a_gate

    akv_yk = {un: _mmb(m4[un][:, c:], prev("v", *un)) for un in units}

    ti = lax.broadcasted_iota(jnp.int32, (tb, tb), 0)
    si = lax.broadcasted_iota(jnp.int32, (tb, tb), 1)
    same_chunk = sum(((ti >= ck * c) & (ti < (ck + 1) * c) & (si >= ck * c)) for ck in range(nc)) > 0
    tri = jnp.where(same_chunk & (ti >= si), 1.0, 0.0).astype(BF16)
    l1, l2 = _split(logw)
    lgc = _dot(tri, l1) + _dot(tri, l2)
    lg_ends = [lgc[(ck + 1) * c - 1:(ck + 1) * c, :] for ck in range(nc)]
    lg_end = jnp.concatenate([jnp.broadcast_to(e, (c, dr)) for e in lg_ends], axis=0)
    ginv = jnp.exp(-lgc)
    gend = jnp.exp(lg_end - lgc)
    new = {"at": avec * jnp.exp(lgc - logw), "rt": r * jnp.exp(lgc), "bt": bvec * ginv,
           "kt": k2 * ginv, "v": v, "bh": bvec * gend, "kh": k2 * gend, "g": g}
    new_gend = jnp.exp(jnp.concatenate(lg_ends, axis=0))

    zpad = jnp.zeros((c, d), F32)
    w = {un: jnp.concatenate([m4[un][:c, :c], zpad, prev("at", *un), akv_yk[un][:c]], axis=1)
         for un in units}
    for lvl in range(6):
        rhs = w if lvl < 5 else {un: w[un][:, 2 * d:] for un in units}
        prod = {un: _mmb(w[un][:, :d], rhs[un]) for un in units}
        if lvl < 5:
            w = {un: jnp.concatenate([prod[un][:, :2 * d], prod[un][:, 2 * d:] + w[un][:, 2 * d:]],
                                     axis=1) for un in units}
        else:
            w = {un: prod[un] + w[un][:, 2 * d:] for un in units}

    new["bonus"] = headsum(r * k2 * rk_ref[...]) * v

    g_end = gend_scr[...]
    s = [s_scr[h] for h in heads]
    for ck in range(nc):
        su = [_mmb(jnp.concatenate([w[ck, h][:, :d], prev("rt", ck, h)], axis=0), s[h], nt=True)
              for h in heads]
        u = [su[h][:c] + w[ck, h][:, d:] for h in heads]
        rbu = [_mmb(m4[ck, h][c:, :c], u[h]) for h in heads]
        upd = [_mmb(jnp.concatenate([u[h], prev("v", ck, h)], axis=0).T,
                    jnp.concatenate([prev("bh", ck, h), prev("kh", ck, h)], axis=0))
               for h in heads]
        for h in heads:
            y_scr[rws[ck], sls[h]] = su[h][c:] + rbu[h] + akv_yk[ck, h][c:]
        s = [s[h] * g_end[ck:ck + 1, sls[h]] + upd[h] for h in heads]
    for h in heads:
        s_scr[h] = s[h]

    y = y_scr[...]
    inv_n = 1.0 / HEAD_DIM
    mean = headsum(y) * inv_n
    yc = y - mean
    var = headsum(yc * yc) * inv_n
    yn = yc * lax.rsqrt(var + GN_EPS) * lng_ref[...] + lnb_ref[...]
    o_ref[0] = ((yn + prep_scr[_PREP.index("bonus")]) * prep_scr[_PREP.index("g")]).astype(o_ref.dtype)

    for i, name in enumerate(_PREP):
        prep_scr[i] = new[name]
    gend_scr[...] = new_gend


def _rwkv(p3, mu_rkv, w0, w2p, a0, a2p, g2p, k_k, k_a, r_k, ln_g, ln_b, hsum, n_heads, lora_w):
    b, t, _ = p3.shape
    dr = n_heads * HEAD_DIM
    c = CHUNK * RWKV_CHUNKS
    nt = t // c
    lora_blk = (3 * dr) // lora_w
    row = lambda n: pl.BlockSpec((1, n), lambda i, j: (0, 0))
    full = lambda a: pl.BlockSpec(a.shape, lambda i, j: (0, 0))
    return pl.pallas_call(
        functools.partial(_rwkv_kernel, n_heads=n_heads),
        out_shape=jax.ShapeDtypeStruct((b, t, dr), BF16),
        grid=(b, nt + 1),
        in_specs=[pl.BlockSpec((1, c, 3 * dr), lambda i, j: (i, jnp.minimum(j, nt - 1), 0)),
                  pl.BlockSpec((1, c, lora_w), lambda i, j: (i, jnp.minimum(j, nt - 1), lora_blk)),
                  row(3 * dr), row(dr), full(w2p), row(dr), full(a2p), full(g2p),
                  row(dr), row(dr), row(dr), row(dr), row(dr), full(hsum)],
        out_specs=pl.BlockSpec((1, c, dr), lambda i, j: (i, jnp.maximum(j - 1, 0), 0)),
        scratch_shapes=[pltpu.VMEM((n_heads, HEAD_DIM, HEAD_DIM), F32),
                        pltpu.VMEM((1, 3 * dr), F32),
                        pltpu.VMEM((1, lora_w // 2), F32),
                        pltpu.VMEM((c, dr), F32),
                        pltpu.VMEM((len(_PREP), c, dr), F32),
                        pltpu.VMEM((RWKV_CHUNKS, dr), F32)],
        compiler_params=_cparams(2),
        name="rwkv",
    )(p3, p3, mu_rkv, w0, w2p, a0, a2p, g2p, k_k, k_a, r_k, ln_g, ln_b, hsum)


def _sbattn_kernel(q_ref, k_ref, v_ref, o_ref, *, tq, tk, nb, npairs):
    qi = pl.program_id(2)
    nsub = tq // tk
    first = lax.broadcasted_iota(jnp.int32, (tk, LANES), 1) < HEAD_DIM
    rr = lax.broadcasted_iota(jnp.int32, (2 * tk, 2 * tk), 0)
    cc = lax.broadcasted_iota(jnp.int32, (2 * tk, 2 * tk), 1)
    tri2 = jnp.where((rr >= cc) & ((rr < tk) == (cc < tk)), 1.0, 0.0).astype(BF16)

    def per_head_rows(blk):
        zero = jnp.zeros_like(blk)
        return jnp.concatenate([jnp.where(first, blk, zero), jnp.where(first, zero, blk)], axis=0)

    def add_rows(full, r0, delta):
        if r0 == 0:
            return full + delta
        return jnp.concatenate([full[:r0], full[r0:] + delta], axis=0)

    def blocks(starts, state, row0s, masks):
        jobs = [(p, u) for p in range(npairs) for u in range(len(starts))]
        lanes = [slice(p * LANES, (p + 1) * LANES) for p in range(npairs)]
        kcs = [per_head_rows(k_ref[0, pl.ds(starts[u], tk), lanes[p]]) for p, u in jobs]
        vcs = [per_head_rows(v_ref[0, pl.ds(starts[u], tk), lanes[p]]) for p, u in jobs]
        zs = [_dot_nt(q_ref[0, row0s[u]:, lanes[p]], kc) for (p, u), kc in zip(jobs, kcs)]
        sps = []
        for (p, u), z in zip(jobs, zs):
            sp = jnp.maximum(z, 0.0) + jnp.log(1.0 + jnp.exp2(-jnp.abs(z))) * LOG2E
            if masks[u] is not None:
                sp = jnp.where(masks[u], sp, 0.0)
            sps.append(sp.astype(BF16))
        css = [_dot(sp, tri2) for sp in sps]
        carries = [c for c, _ in state]
        pvs = []
        for (p, u), z, cs, vc in zip(jobs, zs, css, vcs):
            r0 = row0s[u]
            rows = tq - r0
            attn = jnp.exp2(jnp.minimum(z - cs, 0.0) - carries[p][r0:])
            if masks[u] is not None:
                attn = jnp.where(masks[u], attn, 0.0)
            pvs.append(_dot(attn.astype(BF16), vc))
            tot = jnp.concatenate([jnp.broadcast_to(cs[:, 0:1], (rows, tk)),
                                   jnp.broadcast_to(cs[:, tk:tk + 1], (rows, tk))], axis=1)
            carries[p] = add_rows(carries[p], r0, tot)
        accs = [a for _, a in state]
        for (p, u), pv in zip(jobs, pvs):
            accs[p] = add_rows(accs[p], row0s[u], pv)
        return tuple(zip(carries, accs))

    state = tuple((jnp.zeros((tq, 2 * tk), F32), jnp.zeros((tq, LANES), F32)) for _ in range(npairs))
    for g in range(nsub // nb):
        subs = [nsub - 1 - g * nb - u for u in range(nb)]
        starts = [pl.multiple_of(qi * tq + sub * tk, tk) for sub in subs]
        masks = []
        for sub in subs:
            rows = tq - sub * tk
            qpos = lax.broadcasted_iota(jnp.int32, (rows, 2 * tk), 0)
            col = lax.broadcasted_iota(jnp.int32, (rows, 2 * tk), 1)
            masks.append(jnp.where(col >= tk, col - tk, col) < qpos)
        state = blocks(starts, state, [sub * tk for sub in subs], masks)

    def min_carry(st):
        m = st[0][0]
        for c_, _ in st[1:]:
            m = jnp.minimum(m, c_)
        return jnp.min(jnp.minimum(m[:, :tk], m[:, tk:]))

    n_steps = qi * (nsub // nb)

    def cond(loop):
        i, _, cmin = loop
        return (i < n_steps) & (cmin < SB_DEAD)

    def body(loop):
        i, st, _ = loop
        base = qi * tq - (i + 1) * (nb * tk)
        starts = [pl.multiple_of(base + (nb - 1 - u) * tk, tk) for u in range(nb)]
        st = blocks(starts, st, [0] * nb, [None] * nb)
        return i + 1, st, min_carry(st)

    _, state, _ = lax.while_loop(cond, body, (jnp.int32(0), state, min_carry(state)))
    o_ref[0] = jnp.concatenate([a for _, a in state], axis=1).astype(o_ref.dtype)


def _sbattn(qkv, tq=256, tk=128, nb=2, npairs=4):
    b, t, ds3 = qkv.shape
    ds = ds3 // 3
    w = npairs * LANES
    ng = ds // w
    return pl.pallas_call(
        functools.partial(_sbattn_kernel, tq=tq, tk=tk, nb=nb, npairs=npairs),
        out_shape=jax.ShapeDtypeStruct((b, t, ds), BF16),
        grid=(b, ng, t // tq),
        in_specs=[pl.BlockSpec((1, tq, w), lambda i, h, j: (i, j, h)),
                  pl.BlockSpec((1, t, w), lambda i, h, j: (i, 0, ng + h)),
                  pl.BlockSpec((1, t, w), lambda i, h, j: (i, 0, 2 * ng + h))],
        out_specs=pl.BlockSpec((1, tq, w), lambda i, h, j: (i, j, h)),
        compiler_params=_cparams(3),
        name="sbattn",
    )(qkv, qkv, qkv)


def _outproj_kernel(yr_ref, ys_ref, w_ref, x_ref, gt_ref, g2_ref, sc_ref, sh_ref, x1_ref, h2_ref):
    dr = yr_ref.shape[1]
    mix = _dot(yr_ref[...], w_ref[:dr, :]) + _dot(ys_ref[...], w_ref[dr:, :])
    x1 = x_ref[...] + gt_ref[0] * mix
    x1_ref[...] = x1
    ms = jnp.mean(x1 * x1, axis=-1, keepdims=True)
    y = x1 * lax.rsqrt(ms + RMS_EPS) * g2_ref[...]
    h2_ref[...] = (y * (1.0 + sc_ref[0]) + sh_ref[0]).astype(BF16)


def _outproj(yr, ys, w_out, x2, gt1, gain2, sc2, sh2, seq, tm=512):
    m, d = x2.shape
    dr = yr.shape[1]
    ds = ys.shape[1]
    per_seq = seq // tm
    mod = pl.BlockSpec((1, 1, d), lambda i: (i // per_seq, 0, 0))
    return pl.pallas_call(
        _outproj_kernel,
        out_shape=(jax.ShapeDtypeStruct((m, d), F32), jax.ShapeDtypeStruct((m, d), BF16)),
        grid=(m // tm,),
        in_specs=[pl.BlockSpec((tm, dr), lambda i: (i, 0)),
                  pl.BlockSpec((tm, ds), lambda i: (i, 0)),
                  pl.BlockSpec(w_out.shape, lambda i: (0, 0)),
                  pl.BlockSpec((tm, d), lambda i: (i, 0)),
                  mod,
                  pl.BlockSpec((1, d), lambda i: (0, 0)),
                  mod, mod],
        out_specs=(pl.BlockSpec((tm, d), lambda i: (i, 0)), pl.BlockSpec((tm, d), lambda i: (i, 0))),
        compiler_params=_cparams(1),
        name="outproj",
    )(yr, ys, w_out, x2, gt1, gain2, sc2, sh2)


def _ffn_kernel(h_ref, wg_ref, wu_ref, wd_ref, x_ref, gt_ref, o_ref, acc_scr):
    j = pl.program_id(1)

    @pl.when(j == 0)
    def _():
        acc_scr[...] = jnp.zeros_like(acc_scr)

    h = h_ref[...]
    gate = _dot(h, wg_ref[...])
    up = _dot(h, wu_ref[...])
    act = (gate * _sigmoid(gate) * up).astype(BF16)
    acc_scr[...] += _dot(act, wd_ref[...])

    @pl.when(j == pl.num_programs(1) - 1)
    def _():
        o_ref[...] = x_ref[...] + gt_ref[0] * acc_scr[...]


def _ffn(h2, w_gu, w_down, x1, gt2, seq, tm=512, tf=512):
    m, d = h2.shape
    dff = w_down.shape[0]
    nf = dff // tf
    per_seq = seq // tm
    return pl.pallas_call(
        _ffn_kernel,
        out_shape=jax.ShapeDtypeStruct((m, d), F32),
        grid=(m // tm, nf),
        in_specs=[pl.BlockSpec((tm, d), lambda i, j: (i, 0)),
                  pl.BlockSpec((d, tf), lambda i, j: (0, j)),
                  pl.BlockSpec((d, tf), lambda i, j: (0, j + nf)),
                  pl.BlockSpec((tf, d), lambda i, j: (j, 0)),
                  pl.BlockSpec((tm, d), lambda i, j: (i, 0)),
                  pl.BlockSpec((1, 1, d), lambda i, j: (i // per_seq, 0, 0))],
        out_specs=pl.BlockSpec((tm, d), lambda i, j: (i, 0)),
        scratch_shapes=[pltpu.VMEM((tm, d), F32)],
        compiler_params=_cparams(2),
        name="ffn",
    )(h2, w_gu, w_gu, w_down, x1, gt2)


def _pad_to(a, n, axis):
    pad = [(0, 0)] * a.ndim
    pad[axis] = (0, n - a.shape[axis])
    return jnp.pad(a, pad)


def _layer(x, c_pad, w_ada, b_ada, norm1_gain, norm2_gain, w_in, mu_rkv, mu_w, mu_a, mu_g, w0, w1,
           w2, a0, a1, a2, g1, g2, k_k, k_a, r_k, ln_x_gain, ln_x_bias, q_norm_gain, k_norm_gain,
           w_out, w_gate_up, w_down):
    b, t, d = x.shape
    dr = w0.shape[0]
    ds = d - dr
    n_rwkv = dr // HEAD_DIM
    n_sb = ds // HEAD_DIM
    row = lambda a: a.reshape(1, -1)

    mod = _ada(c_pad, w_ada, row(b_ada))[:b]
    sh1, sc1, gt1, sh2, sc2, gt2 = [m.reshape(b, 1, d) for m in jnp.split(mod, 6, axis=-1)]

    nw = -(-w1.shape[1] // LANES) * LANES
    na = -(-a1.shape[1] // LANES) * LANES
    ng = -(-g1.shape[1] // LANES) * LANES
    w_lora = _fold(w1, a1, g1, mu_w, mu_a, mu_g, (0, nw, nw + na), nw + na + ng)
    lora_w = w_lora.shape[1]
    assert dr == ds == lora_w, "column tiles of the input projection are one head group wide"

    hsum = jnp.kron(jnp.eye(MXU_N // HEAD_DIM, dtype=F32),
                    jnp.ones((HEAD_DIM, HEAD_DIM), F32)).astype(BF16)
    x2 = x.reshape(b * t, d)
    p, qkv = _inproj(x2, row(norm1_gain), sc1, sh1, w_in.astype(BF16), w_lora,
                     row(jnp.tile(q_norm_gain, n_sb)), row(jnp.tile(k_norm_gain, n_sb)), hsum, t,
                     3 * dr // lora_w)
    p3 = p.reshape(b, t, -1)
    y_rwkv = _rwkv(p3, row(mu_rkv), row(w0), _pad_to(w2, nw, 0).astype(BF16), row(a0),
                   _pad_to(a2, na, 0).astype(BF16), _pad_to(g2, ng, 0).astype(BF16),
                   row(k_k), row(k_a), row(r_k), row(ln_x_gain), row(ln_x_bias), hsum,
                   n_rwkv, lora_w)

    y_sb = _sbattn(qkv.reshape(b, t, -1))

    x1, h2 = _outproj(y_rwkv.reshape(b * t, dr), y_sb.reshape(b * t, ds), w_out.astype(BF16), x2,
                      gt1, row(norm2_gain), sc2, sh2, t)
    out = _ffn(h2, w_gate_up.astype(BF16), w_down.astype(BF16), x1, gt2, t)
    return out.reshape(b, t, d)


def kernel(x, c, w_ada, b_ada, norm1_gain, norm2_gain, w_in, mu_rkv, mu_w, mu_a, mu_g, w0, w1, w2,
           a0, a1, a2, g1, g2, k_k, k_a, r_k, ln_x_gain, ln_x_bias, q_norm_gain, k_norm_gain, w_out,
           w_gate_up, w_down):
    depth = w_ada.shape[0]
    c_pad = _pad_to(c, 8, 0)
    for l in range(depth):
        x = _layer(x, c_pad, w_ada[l], b_ada[l], norm1_gain[l], norm2_gain[l], w_in[l], mu_rkv[l],
                   mu_w[l], mu_a[l], mu_g[l], w0[l], w1[l], w2[l], a0[l], a1[l], a2[l], g1[l],
                   g2[l], k_k[l], k_a[l], r_k[l].reshape(-1), ln_x_gain[l], ln_x_bias[l],
                   q_norm_gain[l], k_norm_gain[l], w_out[l], w_gate_up[l], w_down[l])
    return x
```

```python
import functools
import math

import jax
import jax.numpy as jnp
from jax import lax
from jax.experimental import pallas as pl
from jax.experimental.pallas import tpu as pltpu

F32 = jnp.float32
BF16 = jnp.bfloat16

HEAD_DIM = 64
RMS_EPS = 1e-6
GN_EPS = 64e-5
L2_EPS = 1e-12
LOG2E = math.log2(math.e)
SB_DEAD = 150.0
LANES = 128
MXU_N = 256
CHUNK = 64
RWKV_CHUNKS = 2
VMEM_LIMIT = 48 * 1024 * 1024
VMEM_LIMIT_BIG = 56 * 1024 * 1024


def _cparams(n_axes, vmem_limit=VMEM_LIMIT):
    return pltpu.CompilerParams(dimension_semantics=("arbitrary",) * n_axes,
                                vmem_limit_bytes=vmem_limit)


def _sigmoid(x):
    return 1.0 / (1.0 + jnp.exp(-x))


def _softplus(x):
    return jnp.maximum(x, 0.0) + jnp.log(1.0 + jnp.exp(-jnp.abs(x)))


def _dot(x, y):
    return jnp.dot(x, y, preferred_element_type=F32)


def _dot_nt(x, y):
    return lax.dot_general(x, y, (((1,), (1,)), ((), ())), preferred_element_type=F32)


def _split(x):
    hi = x.astype(BF16)
    lo = (x - hi.astype(F32)).astype(BF16)
    return hi, lo


def _mmb(x, y, nt=False):
    return (_dot_nt if nt else _dot)(x.astype(BF16), y.astype(BF16))


def _headsum(t, hs):
    g = hs.shape[0]
    return jnp.concatenate([_dot(t[:, i:i + g].astype(BF16), hs) for i in range(0, t.shape[1], g)],
                           axis=1)


def _ada_kernel(c_ref, w_ref, b_ref, o_ref):
    c = c_ref[...]
    ca = c * _sigmoid(c)
    o_ref[...] = _dot(ca.astype(BF16), w_ref[...].astype(BF16)) + b_ref[...]


def _ada(c_pad, w_ada, b_ada, tn=1024):
    m, d = c_pad.shape
    n = w_ada.shape[1]
    return pl.pallas_call(
        _ada_kernel,
        out_shape=jax.ShapeDtypeStruct((m, n), F32),
        grid=(n // tn,),
        in_specs=[pl.BlockSpec((m, d), lambda j: (0, 0)),
                  pl.BlockSpec((d, tn), lambda j: (0, j)),
                  pl.BlockSpec((1, tn), lambda j: (0, j))],
        out_specs=pl.BlockSpec((m, tn), lambda j: (0, j)),
        compiler_params=_cparams(1),
        name="ada",
    )(c_pad, w_ada, b_ada)


def _fold_kernel(w1_ref, a1_ref, g1_ref, muw_ref, mua_ref, mug_ref, o_ref, *, offsets):
    o_ref[...] = jnp.zeros_like(o_ref)
    half = o_ref.shape[1] // 2
    for w_ref, mu_ref, off in zip((w1_ref, a1_ref, g1_ref), (muw_ref, mua_ref, mug_ref), offsets):
        w = w_ref[...]
        mu = mu_ref[...]
        n = w.shape[1]
        o_ref[:, off:off + n] = (w * (1.0 - mu)).astype(BF16)
        o_ref[:, half + off:half + off + n] = (w * mu).astype(BF16)


def _fold(w1, a1, g1, mu_w, mu_a, mu_g, offsets, half):
    d = w1.shape[0]
    col = lambda m: m.reshape(d, 1)
    return pl.pallas_call(
        functools.partial(_fold_kernel, offsets=offsets),
        out_shape=jax.ShapeDtypeStruct((d, 2 * half), BF16),
        compiler_params=pltpu.CompilerParams(vmem_limit_bytes=VMEM_LIMIT),
        name="fold",
    )(w1, a1, g1, col(mu_w), col(mu_a), col(mu_g))


INPROJ_SLICES = 4


def _inproj_kernel(x_ref, g_ref, sc_ref, sh_ref, w_ref, wl_ref, qg_ref, kg_ref, hs_ref, p_ref, s_ref,
                   h_scr, raw_scr, *, n_rkv):
    i = pl.program_id(0)
    j = pl.program_id(1)
    rows = x_ref.shape[0]

    def prepare():
        x = x_ref[...]
        ms = jnp.mean(x * x, axis=-1, keepdims=True)
        y = x * lax.rsqrt(ms + RMS_EPS) * g_ref[...]
        sl = jnp.clip(j - 1, 0, INPROJ_SLICES - 1)
        start = pl.multiple_of(sl * rows, rows)
        h_scr[i % 2, pl.ds(start, rows), :] = (y * (1.0 + sc_ref[0]) + sh_ref[0]).astype(BF16)

    def h_cur():
        return h_scr[(i + 1) % 2]

    def norm(t, gain):
        ms = _headsum(t * t, hs_ref[...]) * (1.0 / HEAD_DIM)
        return t * lax.rsqrt(ms + RMS_EPS) * gain

    @pl.when(i == 0)
    def _():
        prepare()

    @pl.when((i > 0) & (j == 0))
    def _():
        raw_scr[0] = _dot(h_cur(), w_ref[...])
        prepare()

    @pl.when((i > 0) & (j == 1))
    def _():
        s_ref[...] = (norm(raw_scr[0], qg_ref[...]) * (LOG2E / math.sqrt(HEAD_DIM))).astype(BF16)
        raw_scr[1] = _dot(h_cur(), w_ref[...])
        prepare()

    @pl.when((i > 0) & (j == 2))
    def _():
        s_ref[...] = norm(raw_scr[1], kg_ref[...]).astype(BF16)
        p_ref[...] = _dot(h_cur(), w_ref[...])
        prepare()

    @pl.when((i > 0) & (j > 2) & (j < 2 + n_rkv))
    def _():
        p_ref[...] = _dot(h_cur(), w_ref[...])
        prepare()

    @pl.when((i > 0) & (j == 2 + n_rkv))
    def _():
        p_ref[...] = _dot(h_cur(), wl_ref[...])
        prepare()

    @pl.when((i > 0) & (j == 3 + n_rkv))
    def _():
        s_ref[...] = _dot(h_cur(), w_ref[...]).astype(BF16)
        prepare()


def _inproj(x2, gain, sc, sh, w_in16, w_lora, qg, kg, hsum, seq, n_rkv, tm=1024):
    m, d = x2.shape
    tn = qg.shape[1]
    assert w_lora.shape[1] == tn and w_in16.shape[1] == (n_rkv + 3) * tn
    nrow = m // tm
    nt = n_rkv + 4
    rows = tm // INPROJ_SLICES
    per_seq = seq // tm
    const = lambda a: pl.BlockSpec(a.shape, lambda i, j: (0,) * a.ndim)
    batch = lambda i, j: (jnp.minimum(i, nrow - 1) // per_seq, 0, 0)
    prev_row = lambda i: jnp.maximum(i - 1, 0)

    def w_tile(i, j):
        rkv = jnp.clip(j - 2, 0, n_rkv - 1)
        return 0, jnp.where(j < 2, n_rkv + j, jnp.where(j < n_rkv + 3, rkv, n_rkv + 2))

    def p_tile(i, j):
        return prev_row(i), jnp.where(i == 0, 0, jnp.clip(j - 2, 0, n_rkv))

    def s_tile(i, j):
        return prev_row(i), jnp.where(i == 0, 0, jnp.where(j < 2, 0, jnp.where(j < n_rkv + 3, 1, 2)))

    return pl.pallas_call(
        functools.partial(_inproj_kernel, n_rkv=n_rkv),
        out_shape=(jax.ShapeDtypeStruct((m, (n_rkv + 1) * tn), F32),
                   jax.ShapeDtypeStruct((m, 3 * tn), BF16)),
        grid=(nrow + 1, nt),
        in_specs=[pl.BlockSpec((rows, d), lambda i, j: (jnp.minimum(
                      INPROJ_SLICES * i + jnp.clip(j - 1, 0, INPROJ_SLICES - 1),
                      INPROJ_SLICES * nrow - 1), 0)),
                  const(gain),
                  pl.BlockSpec((1, 1, d), batch),
                  pl.BlockSpec((1, 1, d), batch),
                  pl.BlockSpec((d, tn), w_tile),
                  const(w_lora), const(qg), const(kg), const(hsum)],
        out_specs=(pl.BlockSpec((tm, tn), p_tile), pl.BlockSpec((tm, tn), s_tile)),
        scratch_shapes=[pltpu.VMEM((2, tm, d), BF16), pltpu.VMEM((2, tm, tn), F32)],
        compiler_params=_cparams(2, VMEM_LIMIT_BIG),
        name="inproj",
    )(x2, gain, sc, sh, w_in16, w_lora, qg, kg, hsum)


def _shift_rows(cur, prev_row):
    rolled = pltpu.roll(cur, 1, axis=0)
    row = lax.broadcasted_iota(jnp.int32, cur.shape, 0)
    return jnp.where(row == 0, jnp.broadcast_to(prev_row, cur.shape), rolled)


_PREP = ("at", "rt", "bt", "kt", "v", "bh", "kh", "bonus", "g")


def _rwkv_kernel(p_ref, l_ref, mu_ref, w0_ref, w2_ref, a0_ref, a2_ref, g2_ref, kk_ref, ka_ref,
                 rk_ref, lng_ref, lnb_ref, hs_ref, o_ref, s_scr, prevp_scr, prevl_scr, y_scr,
                 prep_scr, gend_scr, *, n_heads):
    dr = n_heads * HEAD_DIM
    c = CHUNK
    nc = RWKV_CHUNKS
    d = HEAD_DIM
    heads = range(n_heads)
    units = [(ck, h) for ck in range(nc) for h in heads]
    sls = [slice(h * d, (h + 1) * d) for h in heads]
    rws = [slice(ck * c, (ck + 1) * c) for ck in range(nc)]

    @pl.when(pl.program_id(1) == 0)
    def _():
        s_scr[...] = jnp.zeros_like(s_scr)
        prevp_scr[...] = jnp.zeros_like(prevp_scr)
        prevl_scr[...] = jnp.zeros_like(prevl_scr)
        prep_scr[...] = jnp.zeros_like(prep_scr)
        gend_scr[...] = jnp.zeros_like(gend_scr)

    headsum = functools.partial(_headsum, hs=hs_ref[...])

    def prev(name, ck, h):
        i = _PREP.index(name)
        pair = prep_scr[i, rws[ck], (h // 2) * LANES:(h // 2 + 1) * LANES]
        return pair[:, (h % 2) * d:(h % 2 + 1) * d]

    rows2 = lax.broadcasted_iota(jnp.int32, (2 * c, 2 * c), 0)
    cols2 = lax.broadcasted_iota(jnp.int32, (2 * c, 2 * c), 1)
    tt = jnp.where(rows2 >= c, rows2 - c, rows2)
    ii = jnp.where(cols2 >= c, cols2 - c, cols2)
    quad_mask = tt + jnp.where(rows2 >= c, 1, 0) > ii

    m4 = {}
    for un in units:
        ar = jnp.concatenate([prev("at", *un), prev("rt", *un)], axis=0)
        bk = jnp.concatenate([prev("bt", *un), prev("kt", *un)], axis=0)
        m4[un] = jnp.where(quad_mask, _mmb(ar, bk, nt=True), 0.0)

    p = p_ref[0]
    tb = nc * c
    pshift = _shift_rows(p, prevp_scr[...])
    prevp_scr[...] = p[tb - 1:tb, :]
    pm = p + (pshift - p) * mu_ref[...]
    r = pm[:, :dr]
    k = pm[:, dr:2 * dr]
    v = pm[:, 2 * dr:]

    l = l_ref[0]
    half = l.shape[1] // 2
    lb = l[:, half:]
    lin = l[:, :half] + _shift_rows(lb, prevl_scr[...])
    prevl_scr[...] = lb[tb - 1:tb, :]
    nw = w2_ref.shape[0]
    na = a2_ref.shape[0]
    lw = jnp.tanh(lin[:, :nw])
    la = lin[:, nw:nw + na]
    lg = _sigmoid(lin[:, nw + na:])
    w_pre = w0_ref[...] + _dot(lw.astype(BF16), w2_ref[...])
    w_log = -_softplus(-w_pre) - 0.5
    logw = -jnp.exp(w_log)
    a_gate = _sigmoid(a0_ref[...] + _dot(la.astype(BF16), a2_ref[...]))
    g = _dot(lg.astype(BF16), g2_ref[...])
    kk = k * kk_ref[...]
    kk = kk * lax.rsqrt(headsum(kk * kk) + L2_EPS)
    k2 = k * (1.0 + (a_gate - 1.0) * ka_ref[...])
    avec = -kk
    bvec = kk * a_gate

    akv_yk = {un: _mmb(m4[un][:, c:], prev("v", *un)) for un in units}

    ti = lax.broadcasted_iota(jnp.int32, (tb, tb), 0)
    si = lax.broadcasted_iota(jnp.int32, (tb, tb), 1)
    same_chunk = sum(((ti >= ck * c) & (ti < (ck + 1) * c) & (si >= ck * c)) for ck in range(nc)) > 0
    tri = jnp.where(same_chunk & (ti >= si), 1.0, 0.0).astype(BF16)
    l1, l2 = _split(logw)
    lgc = _dot(tri, l1) + _dot(tri, l2)
    lg_ends = [lgc[(ck + 1) * c - 1:(ck + 1) * c, :] for ck in range(nc)]
    lg_end = jnp.concatenate([jnp.broadcast_to(e, (c, dr)) for e in lg_ends], axis=0)
    ginv = jnp.exp(-lgc)
    gend = jnp.exp(lg_end - lgc)
    new = {"at": avec * jnp.exp(lgc - logw), "rt": r * jnp.exp(lgc), "bt": bvec * ginv,
           "kt": k2 * ginv, "v": v, "bh": bvec * gend, "kh": k2 * gend, "g": g}
    new_gend = jnp.exp(jnp.concatenate(lg_ends, axis=0))

    zpad = jnp.zeros((c, d), F32)
    w = {un: jnp.concatenate([m4[un][:c, :c], zpad, prev("at", *un), akv_yk[un][:c]], axis=1)
         for un in units}
    for lvl in range(6):
        rhs = w if lvl < 5 else {un: w[un][:, 2 * d:] for un in units}
        prod = {un: _mmb(w[un][:, :d], rhs[un]) for un in units}
        if lvl < 5:
            w = {un: jnp.concatenate([prod[un][:, :2 * d], prod[un][:, 2 * d:] + w[un][:, 2 * d:]],
                                     axis=1) for un in units}
        else:
            w = {un: prod[un] + w[un][:, 2 * d:] for un in units}

    new["bonus"] = headsum(r * k2 * rk_ref[...]) * v

    g_end = gend_scr[...]
    s = [s_scr[h] for h in heads]
    for ck in range(nc):
        su = [_mmb(jnp.concatenate([w[ck, h][:, :d], prev("rt", ck, h)], axis=0), s[h], nt=True)
              for h in heads]
        u = [su[h][:c] + w[ck, h][:, d:] for h in heads]
        rbu = [_mmb(m4[ck, h][c:, :c], u[h]) for h in heads]
        upd = [_mmb(jnp.concatenate([u[h], prev("v", ck, h)], axis=0).T,
                    jnp.concatenate([prev("bh", ck, h), prev("kh", ck, h)], axis=0))
               for h in heads]
        for h in heads:
            y_scr[rws[ck], sls[h]] = su[h][c:] + rbu[h] + akv_yk[ck, h][c:]
        s = [s[h] * g_end[ck:ck + 1, sls[h]] + upd[h] for h in heads]
    for h in heads:
        s_scr[h] = s[h]

    y = y_scr[...]
    inv_n = 1.0 / HEAD_DIM
    mean = headsum(y) * inv_n
    yc = y - mean
    var = headsum(yc * yc) * inv_n
    yn = yc * lax.rsqrt(var + GN_EPS) * lng_ref[...] + lnb_ref[...]
    o_ref[0] = ((yn + prep_scr[_PREP.index("bonus")]) * prep_scr[_PREP.index("g")]).astype(o_ref.dtype)

    for i, name in enumerate(_PREP):
        prep_scr[i] = new[name]
    gend_scr[...] = new_gend


def _rwkv(p3, mu_rkv, w0, w2p, a0, a2p, g2p, k_k, k_a, r_k, ln_g, ln_b, hsum, n_heads, lora_w):
    b, t, _ = p3.shape
    dr = n_heads * HEAD_DIM
    c = CHUNK * RWKV_CHUNKS
    nt = t // c
    lora_blk = (3 * dr) // lora_w
    row = lambda n: pl.BlockSpec((1, n), lambda i, j: (0, 0))
    full = lambda a: pl.BlockSpec(a.shape, lambda i, j: (0, 0))
    return pl.pallas_call(
        functools.partial(_rwkv_kernel, n_heads=n_heads),
        out_shape=jax.ShapeDtypeStruct((b, t, dr), BF16),
        grid=(b, nt + 1),
        in_specs=[pl.BlockSpec((1, c, 3 * dr), lambda i, j: (i, jnp.minimum(j, nt - 1), 0)),
                  pl.BlockSpec((1, c, lora_w), lambda i, j: (i, jnp.minimum(j, nt - 1), lora_blk)),
                  row(3 * dr), row(dr), full(w2p), row(dr), full(a2p), full(g2p),
                  row(dr), row(dr), row(dr), row(dr), row(dr), full(hsum)],
        out_specs=pl.BlockSpec((1, c, dr), lambda i, j: (i, jnp.maximum(j - 1, 0), 0)),
        scratch_shapes=[pltpu.VMEM((n_heads, HEAD_DIM, HEAD_DIM), F32),
                        pltpu.VMEM((1, 3 * dr), F32),
                        pltpu.VMEM((1, lora_w // 2), F32),
                        pltpu.VMEM((c, dr), F32),
                        pltpu.VMEM((len(_PREP), c, dr), F32),
                        pltpu.VMEM((RWKV_CHUNKS, dr), F32)],
        compiler_params=_cparams(2),
        name="rwkv",
    )(p3, p3, mu_rkv, w0, w2p, a0, a2p, g2p, k_k, k_a, r_k, ln_g, ln_b, hsum)


def _sbattn_kernel(q_ref, k_ref, v_ref, o_ref, *, tq, tk, nb, npairs):
    qi = pl.program_id(2)
    nsub = tq // tk
    first = lax.broadcasted_iota(jnp.int32, (tk, LANES), 1) < HEAD_DIM
    rr = lax.broadcasted_iota(jnp.int32, (2 * tk, 2 * tk), 0)
    cc = lax.broadcasted_iota(jnp.int32, (2 * tk, 2 * tk), 1)
    tri2 = jnp.where((rr >= cc) & ((rr < tk) == (cc < tk)), 1.0, 0.0).astype(BF16)

    def per_head_rows(blk):
        zero = jnp.zeros_like(blk)
        return jnp.concatenate([jnp.where(first, blk, zero), jnp.where(first, zero, blk)], axis=0)

    def add_rows(full, r0, delta):
        if r0 == 0:
            return full + delta
        return jnp.concatenate([full[:r0], full[r0:] + delta], axis=0)

    def blocks(starts, state, row0s, masks):
        jobs = [(p, u) for p in range(npairs) for u in range(len(starts))]
        lanes = [slice(p * LANES, (p + 1) * LANES) for p in range(npairs)]
        kcs = [per_head_rows(k_ref[0, pl.ds(starts[u], tk), lanes[p]]) for p, u in jobs]
        vcs = [per_head_rows(v_ref[0, pl.ds(starts[u], tk), lanes[p]]) for p, u in jobs]
        zs = [_dot_nt(q_ref[0, row0s[u]:, lanes[p]], kc) for (p, u), kc in zip(jobs, kcs)]
        sps = []
        for (p, u), z in zip(jobs, zs):
            sp = jnp.maximum(z, 0.0) + jnp.log(1.0 + jnp.exp2(-jnp.abs(z))) * LOG2E
            if masks[u] is not None:
                sp = jnp.where(masks[u], sp, 0.0)
            sps.append(sp.astype(BF16))
        css = [_dot(sp, tri2) for sp in sps]
        carries = [c for c, _ in state]
        pvs = []
        for (p, u), z, cs, vc in zip(jobs, zs, css, vcs):
            r0 = row0s[u]
            rows = tq - r0
            attn = jnp.exp2(jnp.minimum(z - cs, 0.0) - carries[p][r0:])
            if masks[u] is not None:
                attn = jnp.where(masks[u], attn, 0.0)
            pvs.append(_dot(attn.astype(BF16), vc))
            tot = jnp.concatenate([jnp.broadcast_to(cs[:, 0:1], (rows, tk)),
                                   jnp.broadcast_to(cs[:, tk:tk + 1], (rows, tk))], axis=1)
            carries[p] = add_rows(carries[p], r0, tot)
        accs = [a for _, a in state]
        for (p, u), pv in zip(jobs, pvs):
            accs[p] = add_rows(accs[p], row0s[u], pv)
        return tuple(zip(carries, accs))

    state = tuple((jnp.zeros((tq, 2 * tk), F32), jnp.zeros((tq, LANES), F32)) for _ in range(npairs))
    for g in range(nsub // nb):
        subs = [nsub - 1 - g * nb - u for u in range(nb)]
        starts = [pl.multiple_of(qi * tq + sub * tk, tk) for sub in subs]
        masks = []
        for sub in subs:
            rows = tq - sub * tk
            qpos = lax.broadcasted_iota(jnp.int32, (rows, 2 * tk), 0)
            col = lax.broadcasted_iota(jnp.int32, (rows, 2 * tk), 1)
            masks.append(jnp.where(col >= tk, col - tk, col) < qpos)
        state = blocks(starts, state, [sub * tk for sub in subs], masks)

    def min_carry(st):
        m = st[0][0]
        for c_, _ in st[1:]:
            m = jnp.minimum(m, c_)
        return jnp.min(jnp.minimum(m[:, :tk], m[:, tk:]))

    n_steps = qi * (nsub // nb)

    def cond(loop):
        i, _, cmin = loop
        return (i < n_steps) & (cmin < SB_DEAD)

    def body(loop):
        i, st, _ = loop
        base = qi * tq - (i + 1) * (nb * tk)
        starts = [pl.multiple_of(base + (nb - 1 - u) * tk, tk) for u in range(nb)]
        st = blocks(starts, st, [0] * nb, [None] * nb)
        return i + 1, st, min_carry(st)

    _, state, _ = lax.while_loop(cond, body, (jnp.int32(0), state, jnp.float32(0.0)))
    o_ref[0] = jnp.concatenate([a for _, a in state], axis=1).astype(o_ref.dtype)


def _sbattn(qkv, tq=256, tk=128, nb=2, npairs=4):
    b, t, ds3 = qkv.shape
    ds = ds3 // 3
    w = npairs * LANES
    ng = ds // w
    return pl.pallas_call(
        functools.partial(_sbattn_kernel, tq=tq, tk=tk, nb=nb, npairs=npairs),
        out_shape=jax.ShapeDtypeStruct((b, t, ds), BF16),
        grid=(b, ng, t // tq),
        in_specs=[pl.BlockSpec((1, tq, w), lambda i, h, j: (i, j, h)),
                  pl.BlockSpec((1, t, w), lambda i, h, j: (i, 0, ng + h)),
                  pl.BlockSpec((1, t, w), lambda i, h, j: (i, 0, 2 * ng + h))],
        out_specs=pl.BlockSpec((1, tq, w), lambda i, h, j: (i, j, h)),
        compiler_params=_cparams(3),
        name="sbattn",
    )(qkv, qkv, qkv)


def _outproj_kernel(yr_ref, ys_ref, w_ref, x_ref, gt_ref, g2_ref, sc_ref, sh_ref, x1_ref, h2_ref):
    dr = yr_ref.shape[1]
    mix = _dot(yr_ref[...], w_ref[:dr, :]) + _dot(ys_ref[...], w_ref[dr:, :])
    x1 = x_ref[...] + gt_ref[0] * mix
    x1_ref[...] = x1
    ms = jnp.mean(x1 * x1, axis=-1, keepdims=True)
    y = x1 * lax.rsqrt(ms + RMS_EPS) * g2_ref[...]
    h2_ref[...] = (y * (1.0 + sc_ref[0]) + sh_ref[0]).astype(BF16)


def _outproj(yr, ys, w_out, x2, gt1, gain2, sc2, sh2, seq, tm=512):
    m, d = x2.shape
    dr = yr.shape[1]
    ds = ys.shape[1]
    per_seq = seq // tm
    mod = pl.BlockSpec((1, 1, d), lambda i: (i // per_seq, 0, 0))
    return pl.pallas_call(
        _outproj_kernel,
        out_shape=(jax.ShapeDtypeStruct((m, d), F32), jax.ShapeDtypeStruct((m, d), BF16)),
        grid=(m // tm,),
        in_specs=[pl.BlockSpec((tm, dr), lambda i: (i, 0)),
                  pl.BlockSpec((tm, ds), lambda i: (i, 0)),
                  pl.BlockSpec(w_out.shape, lambda i: (0, 0)),
                  pl.BlockSpec((tm, d), lambda i: (i, 0)),
                  mod,
                  pl.BlockSpec((1, d), lambda i: (0, 0)),
                  mod, mod],
        out_specs=(pl.BlockSpec((tm, d), lambda i: (i, 0)), pl.BlockSpec((tm, d), lambda i: (i, 0))),
        compiler_params=_cparams(1),
        name="outproj",
    )(yr, ys, w_out, x2, gt1, gain2, sc2, sh2)


def _ffn_kernel(h_ref, wg_ref, wu_ref, wd_ref, x_ref, gt_ref, o_ref, acc_scr):
    j = pl.program_id(1)

    @pl.when(j == 0)
    def _():
        acc_scr[...] = jnp.zeros_like(acc_scr)

    h = h_ref[...]
    gate = _dot(h, wg_ref[...])
    up = _dot(h, wu_ref[...])
    act = (gate * _sigmoid(gate) * up).astype(BF16)
    acc_scr[...] += _dot(act, wd_ref[...])

    @pl.when(j == pl.num_programs(1) - 1)
    def _():
        o_ref[...] = x_ref[...] + gt_ref[0] * acc_scr[...]


def _ffn(h2, w_gu, w_down, x1, gt2, seq, tm=512, tf=512):
    m, d = h2.shape
    dff = w_down.shape[0]
    nf = dff // tf
    per_seq = seq // tm
    return pl.pallas_call(
        _ffn_kernel,
        out_shape=jax.ShapeDtypeStruct((m, d), F32),
        grid=(m // tm, nf),
        in_specs=[pl.BlockSpec((tm, d), lambda i, j: (i, 0)),
                  pl.BlockSpec((d, tf), lambda i, j: (0, j)),
                  pl.BlockSpec((d, tf), lambda i, j: (0, j + nf)),
                  pl.BlockSpec((tf, d), lambda i, j: (j, 0)),
                  pl.BlockSpec((tm, d), lambda i, j: (i, 0)),
                  pl.BlockSpec((1, 1, d), lambda i, j: (i // per_seq, 0, 0))],
        out_specs=pl.BlockSpec((tm, d), lambda i, j: (i, 0)),
        scratch_shapes=[pltpu.VMEM((tm, d), F32)],
        compiler_params=_cparams(2),
        name="ffn",
    )(h2, w_gu, w_gu, w_down, x1, gt2)


def _pad_to(a, n, axis):
    pad = [(0, 0)] * a.ndim
    pad[axis] = (0, n - a.shape[axis])
    return jnp.pad(a, pad)


def _layer(x, c_pad, w_ada, b_ada, norm1_gain, norm2_gain, w_in, mu_rkv, mu_w, mu_a, mu_g, w0, w1,
           w2, a0, a1, a2, g1, g2, k_k, k_a, r_k, ln_x_gain, ln_x_bias, q_norm_gain, k_norm_gain,
           w_out, w_gate_up, w_down):
    b, t, d = x.shape
    dr = w0.shape[0]
    ds = d - dr
    n_rwkv = dr // HEAD_DIM
    n_sb = ds // HEAD_DIM
    row = lambda a: a.reshape(1, -1)

    mod = _ada(c_pad, w_ada, row(b_ada))[:b]
    sh1, sc1, gt1, sh2, sc2, gt2 = [m.reshape(b, 1, d) for m in jnp.split(mod, 6, axis=-1)]

    nw = -(-w1.shape[1] // LANES) * LANES
    na = -(-a1.shape[1] // LANES) * LANES
    ng = -(-g1.shape[1] // LANES) * LANES
    w_lora = _fold(w1, a1, g1, mu_w, mu_a, mu_g, (0, nw, nw + na), nw + na + ng)
    lora_w = w_lora.shape[1]
    assert dr == ds == lora_w, "column tiles of the input projection are one head group wide"

    hsum = jnp.kron(jnp.eye(MXU_N // HEAD_DIM, dtype=F32),
                    jnp.ones((HEAD_DIM, HEAD_DIM), F32)).astype(BF16)
    x2 = x.reshape(b * t, d)
    p, qkv = _inproj(x2, row(norm1_gain), sc1, sh1, w_in.astype(BF16), w_lora,
                     row(jnp.tile(q_norm_gain, n_sb)), row(jnp.tile(k_norm_gain, n_sb)), hsum, t,
                     3 * dr // lora_w)
    p3 = p.reshape(b, t, -1)
    y_rwkv = _rwkv(p3, row(mu_rkv), row(w0), _pad_to(w2, nw, 0).astype(BF16), row(a0),
                   _pad_to(a2, na, 0).astype(BF16), _pad_to(g2, ng, 0).astype(BF16),
                   row(k_k), row(k_a), row(r_k), row(ln_x_gain), row(ln_x_bias), hsum,
                   n_rwkv, lora_w)

    y_sb = _sbattn(qkv.reshape(b, t, -1))

    x1, h2 = _outproj(y_rwkv.reshape(b * t, dr), y_sb.reshape(b * t, ds), w_out.astype(BF16), x2,
                      gt1, row(norm2_gain), sc2, sh2, t)
    out = _ffn(h2, w_gate_up.astype(BF16), w_down.astype(BF16), x1, gt2, t)
    return out.reshape(b, t, d)


def kernel(x, c, w_ada, b_ada, norm1_gain, norm2_gain, w_in, mu_rkv, mu_w, mu_a, mu_g, w0, w1, w2,
           a0, a1, a2, g1, g2, k_k, k_a, r_k, ln_x_gain, ln_x_bias, q_norm_gain, k_norm_gain, w_out,
           w_gate_up, w_down):
    depth = w_ada.shape[0]
    c_pad = _pad_to(c, 8, 0)
    for l in range(depth):
        x = _layer(x, c_pad, w_ada[l], b_ada[l], norm1_gain[l], norm2_gain[l], w_in[l], mu_rkv[l],
                   mu_w[l], mu_a[l], mu_g[l], w0[l], w1[l], w2[l], a0[l], a1[l], a2[l], g1[l],
                   g2[l], k_k[l], k_a[l], r_k[l].reshape(-1), ln_x_gain[l], ln_x_bias[l],
                   q_norm_gain[l], k_norm_gain[l], w_out[l], w_gate_up[l], w_down[l])
    return x
```

```python
import functools
import math

import jax
import jax.numpy as jnp
from jax import lax
from jax.experimental import pallas as pl
from jax.experimental.pallas import tpu as pltpu

F32 = jnp.float32
BF16 = jnp.bfloat16

HEAD_DIM = 64
RMS_EPS = 1e-6
GN_EPS = 64e-5
L2_EPS = 1e-12
LOG2E = math.log2(math.e)
SB_DEAD = 150.0
LANES = 128
SUBLANES = 8
MXU_N = 256
CHUNK = 64
RWKV_CHUNKS = 2
VMEM_LIMIT = 48 * 1024 * 1024
VMEM_LIMIT_BIG = 56 * 1024 * 1024


def _cparams(n_axes, vmem_limit=VMEM_LIMIT):
    return pltpu.CompilerParams(dimension_semantics=("arbitrary",) * n_axes,
                                vmem_limit_bytes=vmem_limit)


def _sigmoid(x):
    return 1.0 / (1.0 + jnp.exp(-x))


def _softplus(x):
    return jnp.maximum(x, 0.0) + jnp.log(1.0 + jnp.exp(-jnp.abs(x)))


def _dot(x, y):
    return jnp.dot(x, y, preferred_element_type=F32)


def _dot_nt(x, y):
    return lax.dot_general(x, y, (((1,), (1,)), ((), ())), preferred_element_type=F32)


def _split(x):
    hi = x.astype(BF16)
    lo = (x - hi.astype(F32)).astype(BF16)
    return hi, lo


def _mmb(x, y, nt=False):
    return (_dot_nt if nt else _dot)(x.astype(BF16), y.astype(BF16))


def _headsum(t, hs):
    g = hs.shape[0]
    return jnp.concatenate([_dot(t[:, i:i + g].astype(BF16), hs) for i in range(0, t.shape[1], g)],
                           axis=1)


def _ada_kernel(c_ref, w_ref, b_ref, o_ref):
    c = c_ref[...]
    ca = c * _sigmoid(c)
    o_ref[...] = _dot(ca.astype(BF16), w_ref[...].astype(BF16)) + b_ref[...]


def _ada(c_pad, w_ada, b_ada, tn=1024):
    m, d = c_pad.shape
    n = w_ada.shape[1]
    return pl.pallas_call(
        _ada_kernel,
        out_shape=jax.ShapeDtypeStruct((m, n), F32),
        grid=(n // tn,),
        in_specs=[pl.BlockSpec((m, d), lambda j: (0, 0)),
                  pl.BlockSpec((d, tn), lambda j: (0, j)),
                  pl.BlockSpec((1, tn), lambda j: (0, j))],
        out_specs=pl.BlockSpec((m, tn), lambda j: (0, j)),
        compiler_params=_cparams(1),
        name="ada",
    )(c_pad, w_ada, b_ada)


def _fold_kernel(w1_ref, a1_ref, g1_ref, muw_ref, mua_ref, mug_ref, o_ref, *, offsets):
    o_ref[...] = jnp.zeros_like(o_ref)
    half = o_ref.shape[1] // 2
    for w_ref, mu_ref, off in zip((w1_ref, a1_ref, g1_ref), (muw_ref, mua_ref, mug_ref), offsets):
        w = w_ref[...]
        mu = mu_ref[...]
        n = w.shape[1]
        o_ref[:, off:off + n] = (w * (1.0 - mu)).astype(BF16)
        o_ref[:, half + off:half + off + n] = (w * mu).astype(BF16)


def _fold(w1, a1, g1, mu_w, mu_a, mu_g, offsets, half):
    d = w1.shape[0]
    col = lambda m: m.reshape(d, 1)
    return pl.pallas_call(
        functools.partial(_fold_kernel, offsets=offsets),
        out_shape=jax.ShapeDtypeStruct((d, 2 * half), BF16),
        compiler_params=pltpu.CompilerParams(vmem_limit_bytes=VMEM_LIMIT),
        name="fold",
    )(w1, a1, g1, col(mu_w), col(mu_a), col(mu_g))


INPROJ_SLICES = 4


def _inproj_kernel(x_ref, g_ref, sc_ref, sh_ref, w_ref, wl_ref, qg_ref, kg_ref, hs_ref, p_ref, s_ref,
                   h_scr, *, n_rkv):
    i = pl.program_id(0)
    j = pl.program_id(1)
    rows = x_ref.shape[0]

    def prepare():
        x = x_ref[...]
        ms = jnp.mean(x * x, axis=-1, keepdims=True)
        y = x * lax.rsqrt(ms + RMS_EPS) * g_ref[...]
        sl = jnp.clip(j - 1, 0, INPROJ_SLICES - 1)
        start = pl.multiple_of(sl * rows, rows)
        h_scr[i % 2, pl.ds(start, rows), :] = (y * (1.0 + sc_ref[0]) + sh_ref[0]).astype(BF16)

    def h_cur():
        return h_scr[(i + 1) % 2]

    def norm(t, gain):
        ms = _headsum(t * t, hs_ref[...]) * (1.0 / HEAD_DIM)
        return t * lax.rsqrt(ms + RMS_EPS) * gain

    @pl.when(i == 0)
    def _():
        prepare()

    @pl.when((i > 0) & (j < n_rkv))
    def _():
        p_ref[...] = _dot(h_cur(), w_ref[...])
        prepare()

    @pl.when((i > 0) & (j == n_rkv))
    def _():
        p_ref[...] = _dot(h_cur(), wl_ref[...])
        prepare()

    @pl.when((i > 0) & (j == n_rkv + 1))
    def _():
        q = norm(_dot(h_cur(), w_ref[...]), qg_ref[...])
        s_ref[...] = (q * (LOG2E / math.sqrt(HEAD_DIM))).astype(BF16)
        prepare()

    @pl.when((i > 0) & (j == n_rkv + 2))
    def _():
        s_ref[...] = norm(_dot(h_cur(), w_ref[...]), kg_ref[...]).astype(BF16)
        prepare()

    @pl.when((i > 0) & (j == n_rkv + 3))
    def _():
        s_ref[...] = _dot(h_cur(), w_ref[...]).astype(BF16)
        prepare()


def _inproj(x2, gain, sc, sh, w_in16, w_lora, qg, kg, hsum, seq, n_rkv, tm=1024):
    m, d = x2.shape
    tn = qg.shape[1]
    assert w_lora.shape[1] == tn and w_in16.shape[1] == (n_rkv + 3) * tn
    nrow = m // tm
    nt = n_rkv + 4
    rows = tm // INPROJ_SLICES
    per_seq = seq // tm
    const = lambda a: pl.BlockSpec(a.shape, lambda i, j: (0,) * a.ndim)
    batch = lambda i, j: (jnp.minimum(i, nrow - 1) // per_seq, 0, 0)
    prev_row = lambda i: jnp.maximum(i - 1, 0)
    return pl.pallas_call(
        functools.partial(_inproj_kernel, n_rkv=n_rkv),
        out_shape=(jax.ShapeDtypeStruct((m, (n_rkv + 1) * tn), F32),
                   jax.ShapeDtypeStruct((m, 3 * tn), BF16)),
        grid=(nrow + 1, nt),
        in_specs=[pl.BlockSpec((rows, d), lambda i, j: (jnp.minimum(
                      INPROJ_SLICES * i + jnp.clip(j - 1, 0, INPROJ_SLICES - 1),
                      INPROJ_SLICES * nrow - 1), 0)),
                  const(gain),
                  pl.BlockSpec((1, 1, d), batch),
                  pl.BlockSpec((1, 1, d), batch),
                  pl.BlockSpec((d, tn), lambda i, j: (0, jnp.where(j < n_rkv, j, jnp.maximum(j - 1, n_rkv - 1)))),
                  const(w_lora), const(qg), const(kg), const(hsum)],
        out_specs=(pl.BlockSpec((tm, tn), lambda i, j: (prev_row(i), jnp.where(i == 0, 0, jnp.minimum(j, n_rkv)))),
                   pl.BlockSpec((tm, tn), lambda i, j: (prev_row(i), jnp.where(i == 0, 0, jnp.maximum(j - n_rkv - 1, 0))))),
        scratch_shapes=[pltpu.VMEM((2, tm, d), BF16)],
        compiler_params=_cparams(2, VMEM_LIMIT_BIG),
        name="inproj",
    )(x2, gain, sc, sh, w_in16, w_lora, qg, kg, hsum)


def _shift_rows(cur, prev_row):
    rolled = pltpu.roll(cur, 1, axis=0)
    row = lax.broadcasted_iota(jnp.int32, cur.shape, 0)
    return jnp.where(row == 0, jnp.broadcast_to(prev_row, cur.shape), rolled)


_PREP = ("at", "rt", "bt", "kt", "v", "bh", "kh", "bonus", "g")


def _rwkv_kernel(p_ref, l_ref, mu_ref, w0_ref, w2_ref, a0_ref, a2_ref, g2_ref, kk_ref, ka_ref,
                 rk_ref, lng_ref, lnb_ref, hs_ref, o_ref, s_scr, prevp_scr, prevl_scr, y_scr,
                 prep_scr, gend_scr, *, n_heads):
    dr = n_heads * HEAD_DIM
    c = CHUNK
    nc = RWKV_CHUNKS
    d = HEAD_DIM
    heads = range(n_heads)
    units = [(ck, h) for ck in range(nc) for h in heads]
    sls = [slice(h * d, (h + 1) * d) for h in heads]
    rws = [slice(ck * c, (ck + 1) * c) for ck in range(nc)]

    @pl.when(pl.program_id(1) == 0)
    def _():
        s_scr[...] = jnp.zeros_like(s_scr)
        prevp_scr[...] = jnp.zeros_like(prevp_scr)
        prevl_scr[...] = jnp.zeros_like(prevl_scr)
        prep_scr[...] = jnp.zeros_like(prep_scr)
        gend_scr[...] = jnp.zeros_like(gend_scr)

    headsum = functools.partial(_headsum, hs=hs_ref[...])

    def prev(name, ck, h):
        i = _PREP.index(name)
        pair = prep_scr[i, rws[ck], (h // 2) * LANES:(h // 2 + 1) * LANES]
        return pair[:, (h % 2) * d:(h % 2 + 1) * d]

    rows2 = lax.broadcasted_iota(jnp.int32, (2 * c, 2 * c), 0)
    cols2 = lax.broadcasted_iota(jnp.int32, (2 * c, 2 * c), 1)
    tt = jnp.where(rows2 >= c, rows2 - c, rows2)
    ii = jnp.where(cols2 >= c, cols2 - c, cols2)
    quad_mask = tt + jnp.where(rows2 >= c, 1, 0) > ii

    m4 = {}
    for un in units:
        ar = jnp.concatenate([prev("at", *un), prev("rt", *un)], axis=0)
        bk = jnp.concatenate([prev("bt", *un), prev("kt", *un)], axis=0)
        m4[un] = jnp.where(quad_mask, _mmb(ar, bk, nt=True), 0.0)

    p = p_ref[0]
    tb = nc * c
    pshift = _shift_rows(p, prevp_scr[...])
    prevp_scr[...] = p[tb - 1:tb, :]
    pm = p + (pshift - p) * mu_ref[...]
    r = pm[:, :dr]
    k = pm[:, dr:2 * dr]
    v = pm[:, 2 * dr:]

    l = l_ref[0]
    half = l.shape[1] // 2
    lb = l[:, half:]
    lin = l[:, :half] + _shift_rows(lb, prevl_scr[...])
    prevl_scr[...] = lb[tb - 1:tb, :]
    nw = w2_ref.shape[0]
    na = a2_ref.shape[0]
    lw = jnp.tanh(lin[:, :nw])
    la = lin[:, nw:nw + na]
    lg = _sigmoid(lin[:, nw + na:])
    w_pre = w0_ref[...] + _dot(lw.astype(BF16), w2_ref[...])
    w_log = -_softplus(-w_pre) - 0.5
    logw = -jnp.exp(w_log)
    a_gate = _sigmoid(a0_ref[...] + _dot(la.astype(BF16), a2_ref[...]))
    g = _dot(lg.astype(BF16), g2_ref[...])
    kk = k * kk_ref[...]
    kk = kk * lax.rsqrt(headsum(kk * kk) + L2_EPS)
    k2 = k * (1.0 + (a_gate - 1.0) * ka_ref[...])
    avec = -kk
    bvec = kk * a_gate

    akv_yk = {un: _mmb(m4[un][:, c:], prev("v", *un)) for un in units}

    ti = lax.broadcasted_iota(jnp.int32, (tb, tb), 0)
    si = lax.broadcasted_iota(jnp.int32, (tb, tb), 1)
    same_chunk = sum(((ti >= ck * c) & (ti < (ck + 1) * c) & (si >= ck * c)) for ck in range(nc)) > 0
    tri = jnp.where(same_chunk & (ti >= si), 1.0, 0.0).astype(BF16)
    l1, l2 = _split(logw)
    lgc = _dot(tri, l1) + _dot(tri, l2)
    lg_ends = [lgc[(ck + 1) * c - 1:(ck + 1) * c, :] for ck in range(nc)]
    lg_end = jnp.concatenate([jnp.broadcast_to(e, (c, dr)) for e in lg_ends], axis=0)
    ginv = jnp.exp(-lgc)
    gend = jnp.exp(lg_end - lgc)
    new = {"at": avec * jnp.exp(lgc - logw), "rt": r * jnp.exp(lgc), "bt": bvec * ginv,
           "kt": k2 * ginv, "v": v, "bh": bvec * gend, "kh": k2 * gend, "g": g}
    new_gend = jnp.exp(jnp.concatenate(lg_ends, axis=0))

    zpad = jnp.zeros((c, d), F32)
    w = {un: jnp.concatenate([m4[un][:c, :c], zpad, prev("at", *un), akv_yk[un][:c]], axis=1)
         for un in units}
    for lvl in range(6):
        rhs = w if lvl < 5 else {un: w[un][:, 2 * d:] for un in units}
        prod = {un: _mmb(w[un][:, :d], rhs[un]) for un in units}
        if lvl < 5:
            w = {un: jnp.concatenate([prod[un][:, :2 * d], prod[un][:, 2 * d:] + w[un][:, 2 * d:]],
                                     axis=1) for un in units}
        else:
            w = {un: prod[un] + w[un][:, 2 * d:] for un in units}

    new["bonus"] = headsum(r * k2 * rk_ref[...]) * v

    g_end = gend_scr[...]
    s = [s_scr[h] for h in heads]
    for ck in range(nc):
        su = [_mmb(jnp.concatenate([w[ck, h][:, :d], prev("rt", ck, h)], axis=0), s[h], nt=True)
              for h in heads]
        u = [su[h][:c] + w[ck, h][:, d:] for h in heads]
        rbu = [_mmb(m4[ck, h][c:, :c], u[h]) for h in heads]
        upd = [_mmb(jnp.concatenate([u[h], prev("v", ck, h)], axis=0).T,
                    jnp.concatenate([prev("bh", ck, h), prev("kh", ck, h)], axis=0))
               for h in heads]
        for h in heads:
            y_scr[rws[ck], sls[h]] = su[h][c:] + rbu[h] + akv_yk[ck, h][c:]
        s = [s[h] * g_end[ck:ck + 1, sls[h]] + upd[h] for h in heads]
    for h in heads:
        s_scr[h] = s[h]

    y = y_scr[...]
    inv_n = 1.0 / HEAD_DIM
    mean = headsum(y) * inv_n
    yc = y - mean
    var = headsum(yc * yc) * inv_n
    yn = yc * lax.rsqrt(var + GN_EPS) * lng_ref[...] + lnb_ref[...]
    o_ref[0] = ((yn + prep_scr[_PREP.index("bonus")]) * prep_scr[_PREP.index("g")]).astype(o_ref.dtype)

    for i, name in enumerate(_PREP):
        prep_scr[i] = new[name]
    gend_scr[...] = new_gend


def _rwkv(p3, mu_rkv, w0, w2p, a0, a2p, g2p, k_k, k_a, r_k, ln_g, ln_b, hsum, n_heads, lora_w):
    b, t, _ = p3.shape
    dr = n_heads * HEAD_DIM
    c = CHUNK * RWKV_CHUNKS
    nt = t // c
    lora_blk = (3 * dr) // lora_w
    row = lambda n: pl.BlockSpec((1, n), lambda i, j: (0, 0))
    full = lambda a: pl.BlockSpec(a.shape, lambda i, j: (0, 0))
    return pl.pallas_call(
        functools.partial(_rwkv_kernel, n_heads=n_heads),
        out_shape=jax.ShapeDtypeStruct((b, t, dr), BF16),
        grid=(b, nt + 1),
        in_specs=[pl.BlockSpec((1, c, 3 * dr), lambda i, j: (i, jnp.minimum(j, nt - 1), 0)),
                  pl.BlockSpec((1, c, lora_w), lambda i, j: (i, jnp.minimum(j, nt - 1), lora_blk)),
                  row(3 * dr), row(dr), full(w2p), row(dr), full(a2p), full(g2p),
                  row(dr), row(dr), row(dr), row(dr), row(dr), full(hsum)],
        out_specs=pl.BlockSpec((1, c, dr), lambda i, j: (i, jnp.maximum(j - 1, 0), 0)),
        scratch_shapes=[pltpu.VMEM((n_heads, HEAD_DIM, HEAD_DIM), F32),
                        pltpu.VMEM((1, 3 * dr), F32),
                        pltpu.VMEM((1, lora_w // 2), F32),
                        pltpu.VMEM((c, dr), F32),
                        pltpu.VMEM((len(_PREP), c, dr), F32),
                        pltpu.VMEM((RWKV_CHUNKS, dr), F32)],
        compiler_params=_cparams(2),
        name="rwkv",
    )(p3, p3, mu_rkv, w0, w2p, a0, a2p, g2p, k_k, k_a, r_k, ln_g, ln_b, hsum)


def _sbattn_kernel(q_ref, k_ref, v_ref, o_ref, *, tq, tk, nb, npairs):
    qi = pl.program_id(2)
    nsub = tq // tk
    first = lax.broadcasted_iota(jnp.int32, (tk, LANES), 1) < HEAD_DIM
    rr = lax.broadcasted_iota(jnp.int32, (2 * tk, 2 * tk), 0)
    cc = lax.broadcasted_iota(jnp.int32, (2 * tk, 2 * tk), 1)
    tri2 = jnp.where((rr >= cc) & ((rr < tk) == (cc < tk)), 1.0, 0.0).astype(BF16)

    def per_head_rows(blk):
        zero = jnp.zeros_like(blk)
        return jnp.concatenate([jnp.where(first, blk, zero), jnp.where(first, zero, blk)], axis=0)

    def add_rows(full, r0, delta):
        if r0 == 0:
            return full + delta
        return jnp.concatenate([full[:r0], full[r0:] + delta], axis=0)

    def blocks(starts, state, row0s, masks):
        jobs = [(p, u) for p in range(npairs) for u in range(len(starts))]
        lanes = [slice(p * LANES, (p + 1) * LANES) for p in range(npairs)]
        kcs = [per_head_rows(k_ref[0, pl.ds(starts[u], tk), lanes[p]]) for p, u in jobs]
        vcs = [per_head_rows(v_ref[0, pl.ds(starts[u], tk), lanes[p]]) for p, u in jobs]
        zs = [_dot_nt(q_ref[0, row0s[u]:, lanes[p]], kc) for (p, u), kc in zip(jobs, kcs)]
        sps = []
        for (p, u), z in zip(jobs, zs):
            sp = jnp.maximum(z, 0.0) + jnp.log(1.0 + jnp.exp2(-jnp.abs(z))) * LOG2E
            if masks[u] is not None:
                sp = jnp.where(masks[u], sp, 0.0)
            sps.append(sp.astype(BF16))
        css = [_dot(sp, tri2) for sp in sps]
        carries = [c for c, _ in state]
        pvs = []
        for (p, u), z, cs, vc in zip(jobs, zs, css, vcs):
            r0 = row0s[u]
            rows = tq - r0
            attn = jnp.exp2(jnp.minimum(z - cs, 0.0) - carries[p][r0:])
            if masks[u] is not None:
                attn = jnp.where(masks[u], attn, 0.0)
            pvs.append(_dot(attn.astype(BF16), vc))
            tot = jnp.concatenate([jnp.broadcast_to(cs[:, 0:1], (rows, tk)),
                                   jnp.broadcast_to(cs[:, tk:tk + 1], (rows, tk))], axis=1)
            carries[p] = add_rows(carries[p], r0, tot)
        accs = [a for _, a in state]
        for (p, u), pv in zip(jobs, pvs):
            accs[p] = add_rows(accs[p], row0s[u], pv)
        return tuple(zip(carries, accs))

    state = tuple((jnp.zeros((tq, 2 * tk), F32), jnp.zeros((tq, LANES), F32)) for _ in range(npairs))
    assert nsub == nb, "one group of diagonal blocks"
    subs = [nsub - 1 - u for u in range(nb)]
    starts = [pl.multiple_of(qi * tq + sub * tk, tk) for sub in subs]
    masks = []
    for sub in subs:
        rows = tq - sub * tk
        qpos = lax.broadcasted_iota(jnp.int32, (rows, 2 * tk), 0)
        col = lax.broadcasted_iota(jnp.int32, (rows, 2 * tk), 1)
        masks.append(jnp.where(col >= tk, col - tk, col) < qpos)
    base = jnp.maximum(qi * tq - nb * tk, 0)
    starts += [pl.multiple_of(base + (nb - 1 - u) * tk, tk) for u in range(nb)]
    has_earlier = jnp.broadcast_to(qi > 0, (tq, 2 * tk))
    state = blocks(starts, state, [sub * tk for sub in subs] + [0] * nb, masks + [has_earlier] * nb)

    def min_carry(st):
        m = st[0][0]
        for c_, _ in st[1:]:
            m = jnp.minimum(m, c_)
        return jnp.min(jnp.minimum(m[:, :tk], m[:, tk:]))

    n_steps = qi * (nsub // nb)

    def cond(loop):
        i, _, cmin = loop
        return (i < n_steps) & (cmin < SB_DEAD)

    def body(loop):
        i, st, _ = loop
        base = qi * tq - (i + 1) * (nb * tk)
        starts = [pl.multiple_of(base + (nb - 1 - u) * tk, tk) for u in range(nb)]
        st = blocks(starts, st, [0] * nb, [None] * nb)
        return i + 1, st, min_carry(st)

    _, state, _ = lax.while_loop(cond, body, (jnp.int32(1), state, min_carry(state)))
    o_ref[0] = jnp.concatenate([a for _, a in state], axis=1).astype(o_ref.dtype)


def _sbattn(qkv, tq=256, tk=128, nb=2, npairs=4):
    b, t, ds3 = qkv.shape
    ds = ds3 // 3
    w = npairs * LANES
    ng = ds // w
    return pl.pallas_call(
        functools.partial(_sbattn_kernel, tq=tq, tk=tk, nb=nb, npairs=npairs),
        out_shape=jax.ShapeDtypeStruct((b, t, ds), BF16),
        grid=(b, ng, t // tq),
        in_specs=[pl.BlockSpec((1, tq, w), lambda i, h, j: (i, j, h)),
                  pl.BlockSpec((1, t, w), lambda i, h, j: (i, 0, ng + h)),
                  pl.BlockSpec((1, t, w), lambda i, h, j: (i, 0, 2 * ng + h))],
        out_specs=pl.BlockSpec((1, tq, w), lambda i, h, j: (i, j, h)),
        compiler_params=_cparams(3),
        name="sbattn",
    )(qkv, qkv, qkv)


def _outproj_kernel(yr_ref, ys_ref, w_ref, x_ref, gt_ref, g2_ref, sc_ref, sh_ref, x1_ref, h2_ref):
    dr = yr_ref.shape[1]
    mix = _dot(yr_ref[...], w_ref[:dr, :]) + _dot(ys_ref[...], w_ref[dr:, :])
    x1 = x_ref[...] + gt_ref[0] * mix
    x1_ref[...] = x1
    ms = jnp.mean(x1 * x1, axis=-1, keepdims=True)
    y = x1 * lax.rsqrt(ms + RMS_EPS) * g2_ref[...]
    h2_ref[...] = (y * (1.0 + sc_ref[0]) + sh_ref[0]).astype(BF16)


def _outproj(yr, ys, w_out, x2, gt1, gain2, sc2, sh2, seq, tm=512):
    m, d = x2.shape
    dr = yr.shape[1]
    ds = ys.shape[1]
    per_seq = seq // tm
    mod = pl.BlockSpec((1, 1, d), lambda i: (i // per_seq, 0, 0))
    return pl.pallas_call(
        _outproj_kernel,
        out_shape=(jax.ShapeDtypeStruct((m, d), F32), jax.ShapeDtypeStruct((m, d), BF16)),
        grid=(m // tm,),
        in_specs=[pl.BlockSpec((tm, dr), lambda i: (i, 0)),
                  pl.BlockSpec((tm, ds), lambda i: (i, 0)),
                  pl.BlockSpec(w_out.shape, lambda i: (0, 0)),
                  pl.BlockSpec((tm, d), lambda i: (i, 0)),
                  mod,
                  pl.BlockSpec((1, d), lambda i: (0, 0)),
                  mod, mod],
        out_specs=(pl.BlockSpec((tm, d), lambda i: (i, 0)), pl.BlockSpec((tm, d), lambda i: (i, 0))),
        compiler_params=_cparams(1),
        name="outproj",
    )(yr, ys, w_out, x2, gt1, gain2, sc2, sh2)


def _ffn_kernel(h_ref, wg_ref, wu_ref, wd_ref, x_ref, gt_ref, o_ref, acc_scr):
    j = pl.program_id(1)

    @pl.when(j == 0)
    def _():
        acc_scr[...] = jnp.zeros_like(acc_scr)

    h = h_ref[...]
    gate = _dot(h, wg_ref[...])
    up = _dot(h, wu_ref[...])
    act = (gate * _sigmoid(gate) * up).astype(BF16)
    acc_scr[...] += _dot(act, wd_ref[...])

    @pl.when(j == pl.num_programs(1) - 1)
    def _():
        o_ref[...] = x_ref[...] + gt_ref[0] * acc_scr[...]


def _ffn(h2, w_gu, w_down, x1, gt2, seq, tm=512, tf=512):
    m, d = h2.shape
    dff = w_down.shape[0]
    nf = dff // tf
    per_seq = seq // tm
    return pl.pallas_call(
        _ffn_kernel,
        out_shape=jax.ShapeDtypeStruct((m, d), F32),
        grid=(m // tm, nf),
        in_specs=[pl.BlockSpec((tm, d), lambda i, j: (i, 0)),
                  pl.BlockSpec((d, tf), lambda i, j: (0, j)),
                  pl.BlockSpec((d, tf), lambda i, j: (0, j + nf)),
                  pl.BlockSpec((tf, d), lambda i, j: (j, 0)),
                  pl.BlockSpec((tm, d), lambda i, j: (i, 0)),
                  pl.BlockSpec((1, 1, d), lambda i, j: (i // per_seq, 0, 0))],
        out_specs=pl.BlockSpec((tm, d), lambda i, j: (i, 0)),
        scratch_shapes=[pltpu.VMEM((tm, d), F32)],
        compiler_params=_cparams(2),
        name="ffn",
    )(h2, w_gu, w_gu, w_down, x1, gt2)


def _pad_to(a, n, axis):
    pad = [(0, 0)] * a.ndim
    pad[axis] = (0, n - a.shape[axis])
    return jnp.pad(a, pad)


def _layer(x, c_pad, w_ada, b_ada, norm1_gain, norm2_gain, w_in, mu_rkv, mu_w, mu_a, mu_g, w0, w1,
           w2, a0, a1, a2, g1, g2, k_k, k_a, r_k, ln_x_gain, ln_x_bias, q_norm_gain, k_norm_gain,
           w_out, w_gate_up, w_down):
    b, t, d = x.shape
    dr = w0.shape[0]
    ds = d - dr
    n_rwkv = dr // HEAD_DIM
    n_sb = ds // HEAD_DIM
    row = lambda a: a.reshape(1, -1)

    mod = _ada(c_pad, w_ada, row(b_ada))[:b]
    sh1, sc1, gt1, sh2, sc2, gt2 = [m.reshape(b, 1, d) for m in jnp.split(mod, 6, axis=-1)]

    nw = -(-w1.shape[1] // LANES) * LANES
    na = -(-a1.shape[1] // LANES) * LANES
    ng = -(-g1.shape[1] // LANES) * LANES
    w_lora = _fold(w1, a1, g1, mu_w, mu_a, mu_g, (0, nw, nw + na), nw + na + ng)
    lora_w = w_lora.shape[1]
    assert dr == ds == lora_w, "column tiles of the input projection are one head group wide"

    hsum = jnp.kron(jnp.eye(MXU_N // HEAD_DIM, dtype=F32),
                    jnp.ones((HEAD_DIM, HEAD_DIM), F32)).astype(BF16)
    x2 = x.reshape(b * t, d)
    p, qkv = _inproj(x2, row(norm1_gain), sc1, sh1, w_in.astype(BF16), w_lora,
                     row(jnp.tile(q_norm_gain, n_sb)), row(jnp.tile(k_norm_gain, n_sb)), hsum, t,
                     3 * dr // lora_w)
    p3 = p.reshape(b, t, -1)
    y_rwkv = _rwkv(p3, row(mu_rkv), row(w0), _pad_to(w2, nw, 0).astype(BF16), row(a0),
                   _pad_to(a2, na, 0).astype(BF16), _pad_to(g2, ng, 0).astype(BF16),
                   row(k_k), row(k_a), row(r_k), row(ln_x_gain), row(ln_x_bias), hsum,
                   n_rwkv, lora_w)

    y_sb = _sbattn(qkv.reshape(b, t, -1))

    x1, h2 = _outproj(y_rwkv.reshape(b * t, dr), y_sb.reshape(b * t, ds), w_out.astype(BF16), x2,
                      gt1, row(norm2_gain), sc2, sh2, t)
    out = _ffn(h2, w_gate_up.astype(BF16), w_down.astype(BF16), x1, gt2, t)
    return out.reshape(b, t, d)


def kernel(x, c, w_ada, b_ada, norm1_gain, norm2_gain, w_in, mu_rkv, mu_w, mu_a, mu_g, w0, w1, w2,
           a0, a1, a2, g1, g2, k_k, k_a, r_k, ln_x_gain, ln_x_bias, q_norm_gain, k_norm_gain, w_out,
           w_gate_up, w_down):
    depth = w_ada.shape[0]
    c_pad = _pad_to(c, SUBLANES, 0)
    for l in range(depth):
        x = _layer(x, c_pad, w_ada[l], b_ada[l], norm1_gain[l], norm2_gain[l], w_in[l], mu_rkv[l],
                   mu_w[l], mu_a[l], mu_g[l], w0[l], w1[l], w2[l], a0[l], a1[l], a2[l], g1[l],
                   g2[l], k_k[l], k_a[l], r_k[l].reshape(-1), ln_x_gain[l], ln_x_bias[l],
                   q_norm_gain[l], k_norm_gain[l], w_out[l], w_gate_up[l], w_down[l])
    return x
```

```python
import functools
import math

import jax
import jax.numpy as jnp
from jax import lax
from jax.experimental import pallas as pl
from jax.experimental.pallas import tpu as pltpu

F32 = jnp.float32
BF16 = jnp.bfloat16

HEAD_DIM = 64
RMS_EPS = 1e-6
GN_EPS = 64e-5
L2_EPS = 1e-12
LOG2E = math.log2(math.e)
SB_DEAD = 150.0
LANES = 128
SUBLANES = 8
MXU_N = 256
CHUNK = 64
RWKV_CHUNKS = 2
VMEM_LIMIT = 48 * 1024 * 1024
VMEM_LIMIT_BIG = 56 * 1024 * 1024


def _cparams(n_axes, vmem_limit=VMEM_LIMIT):
    return pltpu.CompilerParams(dimension_semantics=("arbitrary",) * n_axes,
                                vmem_limit_bytes=vmem_limit)


def _sigmoid(x):
    return 1.0 / (1.0 + jnp.exp(-x))


def _softplus(x):
    return jnp.maximum(x, 0.0) + jnp.log(1.0 + jnp.exp(-jnp.abs(x)))


def _dot(x, y):
    return jnp.dot(x, y, preferred_element_type=F32)


def _dot_nt(x, y):
    return lax.dot_general(x, y, (((1,), (1,)), ((), ())), preferred_element_type=F32)


def _split(x):
    hi = x.astype(BF16)
    lo = (x - hi.astype(F32)).astype(BF16)
    return hi, lo


def _mmb(x, y, nt=False):
    return (_dot_nt if nt else _dot)(x.astype(BF16), y.astype(BF16))


def _headsum(t, hs):
    g = hs.shape[0]
    return jnp.concatenate([_dot(t[:, i:i + g].astype(BF16), hs) for i in range(0, t.shape[1], g)],
                           axis=1)


def _ada_kernel(c_ref, w_ref, b_ref, o_ref):
    c = c_ref[...]
    ca = c * _sigmoid(c)
    o_ref[...] = _dot(ca.astype(BF16), w_ref[...].astype(BF16)) + b_ref[...]


def _ada(c_pad, w_ada, b_ada, tn=1024):
    m, d = c_pad.shape
    n = w_ada.shape[1]
    return pl.pallas_call(
        _ada_kernel,
        out_shape=jax.ShapeDtypeStruct((m, n), F32),
        grid=(n // tn,),
        in_specs=[pl.BlockSpec((m, d), lambda j: (0, 0)),
                  pl.BlockSpec((d, tn), lambda j: (0, j)),
                  pl.BlockSpec((1, tn), lambda j: (0, j))],
        out_specs=pl.BlockSpec((m, tn), lambda j: (0, j)),
        compiler_params=_cparams(1),
        name="ada",
    )(c_pad, w_ada, b_ada)


def _fold_kernel(w1_ref, a1_ref, g1_ref, muw_ref, mua_ref, mug_ref, o_ref, *, offsets):
    o_ref[...] = jnp.zeros_like(o_ref)
    half = o_ref.shape[1] // 2
    for w_ref, mu_ref, off in zip((w1_ref, a1_ref, g1_ref), (muw_ref, mua_ref, mug_ref), offsets):
        w = w_ref[...]
        mu = mu_ref[...]
        n = w.shape[1]
        o_ref[:, off:off + n] = (w * (1.0 - mu)).astype(BF16)
        o_ref[:, half + off:half + off + n] = (w * mu).astype(BF16)


def _fold(w1, a1, g1, mu_w, mu_a, mu_g, offsets, half):
    d = w1.shape[0]
    col = lambda m: m.reshape(d, 1)
    return pl.pallas_call(
        functools.partial(_fold_kernel, offsets=offsets),
        out_shape=jax.ShapeDtypeStruct((d, 2 * half), BF16),
        compiler_params=pltpu.CompilerParams(vmem_limit_bytes=VMEM_LIMIT),
        name="fold",
    )(w1, a1, g1, col(mu_w), col(mu_a), col(mu_g))


INPROJ_SLICES = 4


def _inproj_kernel(x_ref, g_ref, sc_ref, sh_ref, w_ref, wl_ref, qg_ref, kg_ref, hs_ref, p_ref, s_ref,
                   h_scr, *, n_rkv):
    i = pl.program_id(0)
    j = pl.program_id(1)
    rows = x_ref.shape[0]

    def prepare():
        x = x_ref[...]
        ms = jnp.mean(x * x, axis=-1, keepdims=True)
        y = x * lax.rsqrt(ms + RMS_EPS) * g_ref[...]
        sl = jnp.clip(j - 1, 0, INPROJ_SLICES - 1)
        start = pl.multiple_of(sl * rows, rows)
        h_scr[i % 2, pl.ds(start, rows), :] = (y * (1.0 + sc_ref[0]) + sh_ref[0]).astype(BF16)

    def h_cur():
        return h_scr[(i + 1) % 2]

    def norm(t, gain):
        ms = _headsum(t * t, hs_ref[...]) * (1.0 / HEAD_DIM)
        return t * lax.rsqrt(ms + RMS_EPS) * gain

    @pl.when(i == 0)
    def _():
        prepare()

    @pl.when((i > 0) & (j < n_rkv))
    def _():
        p_ref[...] = _dot(h_cur(), w_ref[...])
        prepare()

    @pl.when((i > 0) & (j == n_rkv))
    def _():
        p_ref[...] = _dot(h_cur(), wl_ref[...])
        prepare()

    @pl.when((i > 0) & (j == n_rkv + 1))
    def _():
        q = norm(_dot(h_cur(), w_ref[...]), qg_ref[...])
        s_ref[...] = (q * (LOG2E / math.sqrt(HEAD_DIM))).astype(BF16)
        prepare()

    @pl.when((i > 0) & (j == n_rkv + 2))
    def _():
        s_ref[...] = norm(_dot(h_cur(), w_ref[...]), kg_ref[...]).astype(BF16)
        prepare()

    @pl.when((i > 0) & (j == n_rkv + 3))
    def _():
        s_ref[...] = _dot(h_cur(), w_ref[...]).astype(BF16)
        prepare()


def _inproj(x2, gain, sc, sh, w_in16, w_lora, qg, kg, hsum, seq, n_rkv, tm=1024):
    m, d = x2.shape
    tn = qg.shape[1]
    assert w_lora.shape[1] == tn and w_in16.shape[1] == (n_rkv + 3) * tn
    nrow = m // tm
    nt = n_rkv + 4
    rows = tm // INPROJ_SLICES
    per_seq = seq // tm
    const = lambda a: pl.BlockSpec(a.shape, lambda i, j: (0,) * a.ndim)
    batch = lambda i, j: (jnp.minimum(i, nrow - 1) // per_seq, 0, 0)
    prev_row = lambda i: jnp.maximum(i - 1, 0)
    return pl.pallas_call(
        functools.partial(_inproj_kernel, n_rkv=n_rkv),
        out_shape=(jax.ShapeDtypeStruct((m, (n_rkv + 1) * tn), F32),
                   jax.ShapeDtypeStruct((m, 3 * tn), BF16)),
        grid=(nrow + 1, nt),
        in_specs=[pl.BlockSpec((rows, d), lambda i, j: (jnp.minimum(
                      INPROJ_SLICES * i + jnp.clip(j - 1, 0, INPROJ_SLICES - 1),
                      INPROJ_SLICES * nrow - 1), 0)),
                  const(gain),
                  pl.BlockSpec((1, 1, d), batch),
                  pl.BlockSpec((1, 1, d), batch),
                  pl.BlockSpec((d, tn), lambda i, j: (0, jnp.where(j < n_rkv, j, jnp.maximum(j - 1, n_rkv - 1)))),
                  const(w_lora), const(qg), const(kg), const(hsum)],
        out_specs=(pl.BlockSpec((tm, tn), lambda i, j: (prev_row(i), jnp.where(i == 0, 0, jnp.minimum(j, n_rkv)))),
                   pl.BlockSpec((tm, tn), lambda i, j: (prev_row(i), jnp.where(i == 0, 0, jnp.maximum(j - n_rkv - 1, 0))))),
        scratch_shapes=[pltpu.VMEM((2, tm, d), BF16)],
        compiler_params=_cparams(2, VMEM_LIMIT_BIG),
        name="inproj",
    )(x2, gain, sc, sh, w_in16, w_lora, qg, kg, hsum)


def _shift_rows(cur, prev_row):
    rolled = pltpu.roll(cur, 1, axis=0)
    row = lax.broadcasted_iota(jnp.int32, cur.shape, 0)
    return jnp.where(row == 0, jnp.broadcast_to(prev_row, cur.shape), rolled)


_PREP16 = ("at", "rt", "bt", "kt", "v", "bh", "kh")
_PREP = ("bonus", "g")


def _rwkv_kernel(p_ref, l_ref, mu_ref, w0_ref, w2_ref, a0_ref, a2_ref, g2_ref, kk_ref, ka_ref,
                 rk_ref, lng_ref, lnb_ref, hs_ref, o_ref, s_scr, prevp_scr, prevl_scr, y_scr,
                 prep_scr, prep16_scr, gend_scr, *, n_heads):
    dr = n_heads * HEAD_DIM
    c = CHUNK
    nc = RWKV_CHUNKS
    d = HEAD_DIM
    heads = range(n_heads)
    units = [(ck, h) for ck in range(nc) for h in heads]
    sls = [slice(h * d, (h + 1) * d) for h in heads]
    rws = [slice(ck * c, (ck + 1) * c) for ck in range(nc)]

    @pl.when(pl.program_id(1) == 0)
    def _():
        s_scr[...] = jnp.zeros_like(s_scr)
        prevp_scr[...] = jnp.zeros_like(prevp_scr)
        prevl_scr[...] = jnp.zeros_like(prevl_scr)
        prep_scr[...] = jnp.zeros_like(prep_scr)
        prep16_scr[...] = jnp.zeros_like(prep16_scr)
        gend_scr[...] = jnp.zeros_like(gend_scr)

    headsum = functools.partial(_headsum, hs=hs_ref[...])

    def prev(name, ck, h):
        i = _PREP16.index(name)
        pair = prep16_scr[i, rws[ck], (h // 2) * LANES:(h // 2 + 1) * LANES]
        return pair[:, (h % 2) * d:(h % 2 + 1) * d]

    rows2 = lax.broadcasted_iota(jnp.int32, (2 * c, 2 * c), 0)
    cols2 = lax.broadcasted_iota(jnp.int32, (2 * c, 2 * c), 1)
    tt = jnp.where(rows2 >= c, rows2 - c, rows2)
    ii = jnp.where(cols2 >= c, cols2 - c, cols2)
    quad_mask = tt + jnp.where(rows2 >= c, 1, 0) > ii

    m4 = {}
    for un in units:
        ar = jnp.concatenate([prev("at", *un), prev("rt", *un)], axis=0)
        bk = jnp.concatenate([prev("bt", *un), prev("kt", *un)], axis=0)
        m4[un] = jnp.where(quad_mask, _mmb(ar, bk, nt=True), 0.0)

    p = p_ref[0]
    tb = nc * c
    pshift = _shift_rows(p, prevp_scr[...])
    prevp_scr[...] = p[tb - 1:tb, :]
    pm = p + (pshift - p) * mu_ref[...]
    r = pm[:, :dr]
    k = pm[:, dr:2 * dr]
    v = pm[:, 2 * dr:]

    l = l_ref[0]
    half = l.shape[1] // 2
    lb = l[:, half:]
    lin = l[:, :half] + _shift_rows(lb, prevl_scr[...])
    prevl_scr[...] = lb[tb - 1:tb, :]
    nw = w2_ref.shape[0]
    na = a2_ref.shape[0]
    lw = jnp.tanh(lin[:, :nw])
    la = lin[:, nw:nw + na]
    lg = _sigmoid(lin[:, nw + na:])
    w_pre = w0_ref[...] + _dot(lw.astype(BF16), w2_ref[...])
    w_log = -_softplus(-w_pre) - 0.5
    logw = -jnp.exp(w_log)
    a_gate = _sigmoid(a0_ref[...] + _dot(la.astype(BF16), a2_ref[...]))
    g = _dot(lg.astype(BF16), g2_ref[...])
    kk = k * kk_ref[...]
    kk = kk * lax.rsqrt(headsum(kk * kk) + L2_EPS)
    k2 = k * (1.0 + (a_gate - 1.0) * ka_ref[...])
    avec = -kk
    bvec = kk * a_gate

    akv_yk = {un: _mmb(m4[un][:, c:], prev("v", *un)) for un in units}

    ti = lax.broadcasted_iota(jnp.int32, (tb, tb), 0)
    si = lax.broadcasted_iota(jnp.int32, (tb, tb), 1)
    same_chunk = sum(((ti >= ck * c) & (ti < (ck + 1) * c) & (si >= ck * c)) for ck in range(nc)) > 0
    tri = jnp.where(same_chunk & (ti >= si), 1.0, 0.0).astype(BF16)
    l1, l2 = _split(logw)
    lgc = _dot(tri, l1) + _dot(tri, l2)
    lg_ends = [lgc[(ck + 1) * c - 1:(ck + 1) * c, :] for ck in range(nc)]
    lg_end = jnp.concatenate([jnp.broadcast_to(e, (c, dr)) for e in lg_ends], axis=0)
    ginv = jnp.exp(-lgc)
    gend = jnp.exp(lg_end - lgc)
    new = {"at": avec * jnp.exp(lgc - logw), "rt": r * jnp.exp(lgc), "bt": bvec * ginv,
           "kt": k2 * ginv, "v": v, "bh": bvec * gend, "kh": k2 * gend, "g": g}
    new_gend = jnp.exp(jnp.concatenate(lg_ends, axis=0))

    zpad = jnp.zeros((c, d), F32)
    w = {un: jnp.concatenate([m4[un][:c, :c], zpad, prev("at", *un).astype(F32), akv_yk[un][:c]], axis=1)
         for un in units}
    for lvl in range(6):
        rhs = w if lvl < 5 else {un: w[un][:, 2 * d:] for un in units}
        prod = {un: _mmb(w[un][:, :d], rhs[un]) for un in units}
        if lvl < 5:
            w = {un: jnp.concatenate([prod[un][:, :2 * d], prod[un][:, 2 * d:] + w[un][:, 2 * d:]],
                                     axis=1) for un in units}
        else:
            w = {un: prod[un] + w[un][:, 2 * d:] for un in units}

    new["bonus"] = headsum(r * k2 * rk_ref[...]) * v

    g_end = gend_scr[...]
    s = [s_scr[h] for h in heads]
    for ck in range(nc):
        su = [_mmb(jnp.concatenate([w[ck, h][:, :d], prev("rt", ck, h)], axis=0), s[h], nt=True)
              for h in heads]
        u = [su[h][:c] + w[ck, h][:, d:] for h in heads]
        rbu = [_mmb(m4[ck, h][c:, :c], u[h]) for h in heads]
        upd = [_mmb(jnp.concatenate([u[h], prev("v", ck, h)], axis=0).T,
                    jnp.concatenate([prev("bh", ck, h), prev("kh", ck, h)], axis=0))
               for h in heads]
        for h in heads:
            y_scr[rws[ck], sls[h]] = su[h][c:] + rbu[h] + akv_yk[ck, h][c:]
        s = [s[h] * g_end[ck:ck + 1, sls[h]] + upd[h] for h in heads]
    for h in heads:
        s_scr[h] = s[h]

    y = y_scr[...]
    inv_n = 1.0 / HEAD_DIM
    mean = headsum(y) * inv_n
    yc = y - mean
    var = headsum(yc * yc) * inv_n
    yn = yc * lax.rsqrt(var + GN_EPS) * lng_ref[...] + lnb_ref[...]
    o_ref[0] = ((yn + prep_scr[_PREP.index("bonus")]) * prep_scr[_PREP.index("g")]).astype(o_ref.dtype)

    for i, name in enumerate(_PREP):
        prep_scr[i] = new[name]
    for i, name in enumerate(_PREP16):
        prep16_scr[i] = new[name].astype(BF16)
    gend_scr[...] = new_gend


def _rwkv(p3, mu_rkv, w0, w2p, a0, a2p, g2p, k_k, k_a, r_k, ln_g, ln_b, hsum, n_heads, lora_w):
    b, t, _ = p3.shape
    dr = n_heads * HEAD_DIM
    c = CHUNK * RWKV_CHUNKS
    nt = t // c
    lora_blk = (3 * dr) // lora_w
    row = lambda n: pl.BlockSpec((1, n), lambda i, j: (0, 0))
    full = lambda a: pl.BlockSpec(a.shape, lambda i, j: (0, 0))
    return pl.pallas_call(
        functools.partial(_rwkv_kernel, n_heads=n_heads),
        out_shape=jax.ShapeDtypeStruct((b, t, dr), BF16),
        grid=(b, nt + 1),
        in_specs=[pl.BlockSpec((1, c, 3 * dr), lambda i, j: (i, jnp.minimum(j, nt - 1), 0)),
                  pl.BlockSpec((1, c, lora_w), lambda i, j: (i, jnp.minimum(j, nt - 1), lora_blk)),
                  row(3 * dr), row(dr), full(w2p), row(dr), full(a2p), full(g2p),
                  row(dr), row(dr), row(dr), row(dr), row(dr), full(hsum)],
        out_specs=pl.BlockSpec((1, c, dr), lambda i, j: (i, jnp.maximum(j - 1, 0), 0)),
        scratch_shapes=[pltpu.VMEM((n_heads, HEAD_DIM, HEAD_DIM), F32),
                        pltpu.VMEM((1, 3 * dr), F32),
                        pltpu.VMEM((1, lora_w // 2), F32),
                        pltpu.VMEM((c, dr), F32),
                        pltpu.VMEM((len(_PREP), c, dr), F32),
                        pltpu.VMEM((len(_PREP16), c, dr), BF16),
                        pltpu.VMEM((RWKV_CHUNKS, dr), F32)],
        compiler_params=_cparams(2),
        name="rwkv",
    )(p3, p3, mu_rkv, w0, w2p, a0, a2p, g2p, k_k, k_a, r_k, ln_g, ln_b, hsum)


def _sbattn_kernel(q_ref, k_ref, v_ref, o_ref, *, tq, tk, nb, npairs):
    qi = pl.program_id(2)
    nsub = tq // tk
    first = lax.broadcasted_iota(jnp.int32, (tk, LANES), 1) < HEAD_DIM
    rr = lax.broadcasted_iota(jnp.int32, (2 * tk, 2 * tk), 0)
    cc = lax.broadcasted_iota(jnp.int32, (2 * tk, 2 * tk), 1)
    tri2 = jnp.where((rr >= cc) & ((rr < tk) == (cc < tk)), 1.0, 0.0).astype(BF16)

    def per_head_rows(blk):
        zero = jnp.zeros_like(blk)
        return jnp.concatenate([jnp.where(first, blk, zero), jnp.where(first, zero, blk)], axis=0)

    def add_rows(full, r0, delta):
        if r0 == 0:
            return full + delta
        return jnp.concatenate([full[:r0], full[r0:] + delta], axis=0)

    def blocks(starts, state, row0s, masks):
        jobs = [(p, u) for p in range(npairs) for u in range(len(starts))]
        lanes = [slice(p * LANES, (p + 1) * LANES) for p in range(npairs)]
        kcs = [per_head_rows(k_ref[0, pl.ds(starts[u], tk), lanes[p]]) for p, u in jobs]
        vcs = [per_head_rows(v_ref[0, pl.ds(starts[u], tk), lanes[p]]) for p, u in jobs]
        zs = [_dot_nt(q_ref[0, row0s[u]:, lanes[p]], kc) for (p, u), kc in zip(jobs, kcs)]
        sps = []
        for (p, u), z in zip(jobs, zs):
            sp = jnp.maximum(z, 0.0) + jnp.log(1.0 + jnp.exp2(-jnp.abs(z))) * LOG2E
            if masks[u] is not None:
                sp = jnp.where(masks[u], sp, 0.0)
            sps.append(sp.astype(BF16))
        css = [_dot(sp, tri2) for sp in sps]
        carries = [c for c, _ in state]
        pvs = []
        for (p, u), z, cs, vc in zip(jobs, zs, css, vcs):
            r0 = row0s[u]
            rows = tq - r0
            attn = jnp.exp2(jnp.minimum(z - cs, 0.0) - carries[p][r0:])
            if masks[u] is not None:
                attn = jnp.where(masks[u], attn, 0.0)
            pvs.append(_dot(attn.astype(BF16), vc))
            tot = jnp.concatenate([jnp.broadcast_to(cs[:, 0:1], (rows, tk)),
                                   jnp.broadcast_to(cs[:, tk:tk + 1], (rows, tk))], axis=1)
            carries[p] = add_rows(carries[p], r0, tot)
        accs = [a for _, a in state]
        for (p, u), pv in zip(jobs, pvs):
            accs[p] = add_rows(accs[p], row0s[u], pv)
        return tuple(zip(carries, accs))

    state = tuple((jnp.zeros((tq, 2 * tk), F32), jnp.zeros((tq, LANES), F32)) for _ in range(npairs))
    assert nsub == nb, "one group of diagonal blocks"
    subs = [nsub - 1 - u for u in range(nb)]
    starts = [pl.multiple_of(qi * tq + sub * tk, tk) for sub in subs]
    masks = []
    for sub in subs:
        rows = tq - sub * tk
        qpos = lax.broadcasted_iota(jnp.int32, (rows, 2 * tk), 0)
        col = lax.broadcasted_iota(jnp.int32, (rows, 2 * tk), 1)
        masks.append(jnp.where(col >= tk, col - tk, col) < qpos)
    base = jnp.maximum(qi * tq - nb * tk, 0)
    starts += [pl.multiple_of(base + (nb - 1 - u) * tk, tk) for u in range(nb)]
    has_earlier = jnp.broadcast_to(qi > 0, (tq, 2 * tk))
    state = blocks(starts, state, [sub * tk for sub in subs] + [0] * nb, masks + [has_earlier] * nb)

    def min_carry(st):
        m = st[0][0]
        for c_, _ in st[1:]:
            m = jnp.minimum(m, c_)
        return jnp.min(jnp.minimum(m[:, :tk], m[:, tk:]))

    n_steps = qi * (nsub // nb)

    def cond(loop):
        i, _, cmin = loop
        return (i < n_steps) & (cmin < SB_DEAD)

    def body(loop):
        i, st, _ = loop
        base = qi * tq - (i + 1) * (nb * tk)
        starts = [pl.multiple_of(base + (nb - 1 - u) * tk, tk) for u in range(nb)]
        st = blocks(starts, st, [0] * nb, [None] * nb)
        return i + 1, st, min_carry(st)

    _, state, _ = lax.while_loop(cond, body, (jnp.int32(1), state, min_carry(state)))
    o_ref[0] = jnp.concatenate([a for _, a in state], axis=1).astype(o_ref.dtype)


def _sbattn(qkv, tq=256, tk=128, nb=2, npairs=4):
    b, t, ds3 = qkv.shape
    ds = ds3 // 3
    w = npairs * LANES
    ng = ds // w
    return pl.pallas_call(
        functools.partial(_sbattn_kernel, tq=tq, tk=tk, nb=nb, npairs=npairs),
        out_shape=jax.ShapeDtypeStruct((b, t, ds), BF16),
        grid=(b, ng, t // tq),
        in_specs=[pl.BlockSpec((1, tq, w), lambda i, h, j: (i, j, h)),
                  pl.BlockSpec((1, t, w), lambda i, h, j: (i, 0, ng + h)),
                  pl.BlockSpec((1, t, w), lambda i, h, j: (i, 0, 2 * ng + h))],
        out_specs=pl.BlockSpec((1, tq, w), lambda i, h, j: (i, j, h)),
        compiler_params=_cparams(3),
        name="sbattn",
    )(qkv, qkv, qkv)


def _outproj_kernel(yr_ref, ys_ref, w_ref, x_ref, gt_ref, g2_ref, sc_ref, sh_ref, x1_ref, h2_ref):
    dr = yr_ref.shape[1]
    mix = _dot(yr_ref[...], w_ref[:dr, :]) + _dot(ys_ref[...], w_ref[dr:, :])
    x1 = x_ref[...] + gt_ref[0] * mix
    x1_ref[...] = x1
    ms = jnp.mean(x1 * x1, axis=-1, keepdims=True)
    y = x1 * lax.rsqrt(ms + RMS_EPS) * g2_ref[...]
    h2_ref[...] = (y * (1.0 + sc_ref[0]) + sh_ref[0]).astype(BF16)


def _outproj(yr, ys, w_out, x2, gt1, gain2, sc2, sh2, seq, tm=512):
    m, d = x2.shape
    dr = yr.shape[1]
    ds = ys.shape[1]
    per_seq = seq // tm
    mod = pl.BlockSpec((1, 1, d), lambda i: (i // per_seq, 0, 0))
    return pl.pallas_call(
        _outproj_kernel,
        out_shape=(jax.ShapeDtypeStruct((m, d), F32), jax.ShapeDtypeStruct((m, d), BF16)),
        grid=(m // tm,),
        in_specs=[pl.BlockSpec((tm, dr), lambda i: (i, 0)),
                  pl.BlockSpec((tm, ds), lambda i: (i, 0)),
                  pl.BlockSpec(w_out.shape, lambda i: (0, 0)),
                  pl.BlockSpec((tm, d), lambda i: (i, 0)),
                  mod,
                  pl.BlockSpec((1, d), lambda i: (0, 0)),
                  mod, mod],
        out_specs=(pl.BlockSpec((tm, d), lambda i: (i, 0)), pl.BlockSpec((tm, d), lambda i: (i, 0))),
        compiler_params=_cparams(1),
        name="outproj",
    )(yr, ys, w_out, x2, gt1, gain2, sc2, sh2)


def _ffn_kernel(h_ref, wg_ref, wu_ref, wd_ref, x_ref, gt_ref, o_ref, acc_scr):
    j = pl.program_id(1)

    @pl.when(j == 0)
    def _():
        acc_scr[...] = jnp.zeros_like(acc_scr)

    h = h_ref[...]
    gate = _dot(h, wg_ref[...])
    up = _dot(h, wu_ref[...])
    act = (gate * _sigmoid(gate) * up).astype(BF16)
    acc_scr[...] += _dot(act, wd_ref[...])

    @pl.when(j == pl.num_programs(1) - 1)
    def _():
        o_ref[...] = x_ref[...] + gt_ref[0] * acc_scr[...]


def _ffn(h2, w_gu, w_down, x1, gt2, seq, tm=512, tf=512):
    m, d = h2.shape
    dff = w_down.shape[0]
    nf = dff // tf
    per_seq = seq // tm
    return pl.pallas_call(
        _ffn_kernel,
        out_shape=jax.ShapeDtypeStruct((m, d), F32),
        grid=(m // tm, nf),
        in_specs=[pl.BlockSpec((tm, d), lambda i, j: (i, 0)),
                  pl.BlockSpec((d, tf), lambda i, j: (0, j)),
                  pl.BlockSpec((d, tf), lambda i, j: (0, j + nf)),
                  pl.BlockSpec((tf, d), lambda i, j: (j, 0)),
                  pl.BlockSpec((tm, d), lambda i, j: (i, 0)),
                  pl.BlockSpec((1, 1, d), lambda i, j: (i // per_seq, 0, 0))],
        out_specs=pl.BlockSpec((tm, d), lambda i, j: (i, 0)),
        scratch_shapes=[pltpu.VMEM((tm, d), F32)],
        compiler_params=_cparams(2),
        name="ffn",
    )(h2, w_gu, w_gu, w_down, x1, gt2)


def _pad_to(a, n, axis):
    pad = [(0, 0)] * a.ndim
    pad[axis] = (0, n - a.shape[axis])
    return jnp.pad(a, pad)


def _layer(x, c_pad, w_ada, b_ada, norm1_gain, norm2_gain, w_in, mu_rkv, mu_w, mu_a, mu_g, w0, w1,
           w2, a0, a1, a2, g1, g2, k_k, k_a, r_k, ln_x_gain, ln_x_bias, q_norm_gain, k_norm_gain,
           w_out, w_gate_up, w_down):
    b, t, d = x.shape
    dr = w0.shape[0]
    ds = d - dr
    n_rwkv = dr // HEAD_DIM
    n_sb = ds // HEAD_DIM
    row = lambda a: a.reshape(1, -1)

    mod = _ada(c_pad, w_ada, row(b_ada))[:b]
    sh1, sc1, gt1, sh2, sc2, gt2 = [m.reshape(b, 1, d) for m in jnp.split(mod, 6, axis=-1)]

    nw = -(-w1.shape[1] // LANES) * LANES
    na = -(-a1.shape[1] // LANES) * LANES
    ng = -(-g1.shape[1] // LANES) * LANES
    w_lora = _fold(w1, a1, g1, mu_w, mu_a, mu_g, (0, nw, nw + na), nw + na + ng)
    lora_w = w_lora.shape[1]
    assert dr == ds == lora_w, "column tiles of the input projection are one head group wide"

    hsum = jnp.kron(jnp.eye(MXU_N // HEAD_DIM, dtype=F32),
                    jnp.ones((HEAD_DIM, HEAD_DIM), F32)).astype(BF16)
    x2 = x.reshape(b * t, d)
    p, qkv = _inproj(x2, row(norm1_gain), sc1, sh1, w_in.astype(BF16), w_lora,
                     row(jnp.tile(q_norm_gain, n_sb)), row(jnp.tile(k_norm_gain, n_sb)), hsum, t,
                     3 * dr // lora_w)
    p3 = p.reshape(b, t, -1)
    y_rwkv = _rwkv(p3, row(mu_rkv), row(w0), _pad_to(w2, nw, 0).astype(BF16), row(a0),
                   _pad_to(a2, na, 0).astype(BF16), _pad_to(g2, ng, 0).astype(BF16),
                   row(k_k), row(k_a), row(r_k), row(ln_x_gain), row(ln_x_bias), hsum,
                   n_rwkv, lora_w)

    y_sb = _sbattn(qkv.reshape(b, t, -1))

    x1, h2 = _outproj(y_rwkv.reshape(b * t, dr), y_sb.reshape(b * t, ds), w_out.astype(BF16), x2,
                      gt1, row(norm2_gain), sc2, sh2, t)
    out = _ffn(h2, w_gate_up.astype(BF16), w_down.astype(BF16), x1, gt2, t)
    return out.reshape(b, t, d)


def kernel(x, c, w_ada, b_ada, norm1_gain, norm2_gain, w_in, mu_rkv, mu_w, mu_a, mu_g, w0, w1, w2,
           a0, a1, a2, g1, g2, k_k, k_a, r_k, ln_x_gain, ln_x_bias, q_norm_gain, k_norm_gain, w_out,
           w_gate_up, w_down):
    depth = w_ada.shape[0]
    c_pad = _pad_to(c, SUBLANES, 0)
    for l in range(depth):
        x = _layer(x, c_pad, w_ada[l], b_ada[l], norm1_gain[l], norm2_gain[l], w_in[l], mu_rkv[l],
                   mu_w[l], mu_a[l], mu_g[l], w0[l], w1[l], w2[l], a0[l], a1[l], a2[l], g1[l],
                   g2[l], k_k[l], k_a[l], r_k[l].reshape(-1), ln_x_gain[l], ln_x_bias[l],
                   q_norm_gain[l], k_norm_gain[l], w_out[l], w_gate_up[l], w_down[l])
    return x
```

```python
import functools
import math

import jax
import jax.numpy as jnp
from jax import lax
from jax.experimental import pallas as pl
from jax.experimental.pallas import tpu as pltpu

F32 = jnp.float32
BF16 = jnp.bfloat16

HEAD_DIM = 64
RMS_EPS = 1e-6
GN_EPS = 64e-5
L2_EPS = 1e-12
LOG2E = math.log2(math.e)
SB_DEAD = 150.0
LANES = 128
SUBLANES = 8
MXU_N = 256
CHUNK = 64
RWKV_CHUNKS = 2
VMEM_LIMIT = 48 * 1024 * 1024
VMEM_LIMIT_BIG = 56 * 1024 * 1024


def _cparams(n_axes, vmem_limit=VMEM_LIMIT):
    return pltpu.CompilerParams(dimension_semantics=("arbitrary",) * n_axes,
                                vmem_limit_bytes=vmem_limit)


def _sigmoid(x):
    return 1.0 / (1.0 + jnp.exp(-x))


def _softplus(x):
    return jnp.maximum(x, 0.0) + jnp.log(1.0 + jnp.exp(-jnp.abs(x)))


def _dot(x, y):
    return jnp.dot(x, y, preferred_element_type=F32)


def _dot_nt(x, y):
    return lax.dot_general(x, y, (((1,), (1,)), ((), ())), preferred_element_type=F32)


def _split(x):
    hi = x.astype(BF16)
    lo = (x - hi.astype(F32)).astype(BF16)
    return hi, lo


def _mmb(x, y, nt=False):
    return (_dot_nt if nt else _dot)(x.astype(BF16), y.astype(BF16))


def _headsum(t, hs):
    g = hs.shape[0]
    return jnp.concatenate([_dot(t[:, i:i + g].astype(BF16), hs) for i in range(0, t.shape[1], g)],
                           axis=1)


def _ada_kernel(c_ref, w_ref, b_ref, o_ref):
    c = c_ref[...]
    ca = c * _sigmoid(c)
    o_ref[...] = _dot(ca.astype(BF16), w_ref[...].astype(BF16)) + b_ref[...]


def _ada(c_pad, w_ada, b_ada, tn=1024):
    m, d = c_pad.shape
    n = w_ada.shape[1]
    return pl.pallas_call(
        _ada_kernel,
        out_shape=jax.ShapeDtypeStruct((m, n), F32),
        grid=(n // tn,),
        in_specs=[pl.BlockSpec((m, d), lambda j: (0, 0)),
                  pl.BlockSpec((d, tn), lambda j: (0, j)),
                  pl.BlockSpec((1, tn), lambda j: (0, j))],
        out_specs=pl.BlockSpec((m, tn), lambda j: (0, j)),
        compiler_params=_cparams(1),
        name="ada",
    )(c_pad, w_ada, b_ada)


def _fold_kernel(w1_ref, a1_ref, g1_ref, muw_ref, mua_ref, mug_ref, o_ref, *, offsets):
    o_ref[...] = jnp.zeros_like(o_ref)
    half = o_ref.shape[1] // 2
    for w_ref, mu_ref, off in zip((w1_ref, a1_ref, g1_ref), (muw_ref, mua_ref, mug_ref), offsets):
        w = w_ref[...]
        mu = mu_ref[...]
        n = w.shape[1]
        o_ref[:, off:off + n] = (w * (1.0 - mu)).astype(BF16)
        o_ref[:, half + off:half + off + n] = (w * mu).astype(BF16)


def _fold(w1, a1, g1, mu_w, mu_a, mu_g, offsets, half):
    d = w1.shape[0]
    col = lambda m: m.reshape(d, 1)
    return pl.pallas_call(
        functools.partial(_fold_kernel, offsets=offsets),
        out_shape=jax.ShapeDtypeStruct((d, 2 * half), BF16),
        compiler_params=pltpu.CompilerParams(vmem_limit_bytes=VMEM_LIMIT),
        name="fold",
    )(w1, a1, g1, col(mu_w), col(mu_a), col(mu_g))


INPROJ_SLICES = 4
CAST_ROWS = 16


def _slab_rows(n_rows, n_steps):
    units = n_rows // CAST_ROWS
    assert units * CAST_ROWS == n_rows
    for k in range(1, units + 1):
        if units % k == 0 and units // k <= n_steps:
            return k * CAST_ROWS
    raise ValueError("no slab height fits")


def _inproj_kernel(x_ref, g_ref, sc_ref, sh_ref, w_ref, wl_ref, qg_ref, kg_ref, hs_ref, p_ref, s_ref,
                   h_scr, *, n_rkv):
    i = pl.program_id(0)
    j = pl.program_id(1)
    rows = x_ref.shape[0]

    def prepare():
        x = x_ref[...]
        ms = jnp.mean(x * x, axis=-1, keepdims=True)
        y = x * lax.rsqrt(ms + RMS_EPS) * g_ref[...]
        sl = jnp.clip(j - 1, 0, INPROJ_SLICES - 1)
        start = pl.multiple_of(sl * rows, rows)
        h_scr[i % 2, pl.ds(start, rows), :] = (y * (1.0 + sc_ref[0]) + sh_ref[0]).astype(BF16)

    def h_cur():
        return h_scr[(i + 1) % 2]

    def norm(t, gain):
        ms = _headsum(t * t, hs_ref[...]) * (1.0 / HEAD_DIM)
        return t * lax.rsqrt(ms + RMS_EPS) * gain

    @pl.when(i == 0)
    def _():
        prepare()

    @pl.when((i > 0) & (j < n_rkv))
    def _():
        p_ref[...] = _dot(h_cur(), w_ref[...])
        prepare()

    @pl.when((i > 0) & (j == n_rkv))
    def _():
        p_ref[...] = _dot(h_cur(), wl_ref[...])
        prepare()

    @pl.when((i > 0) & (j == n_rkv + 1))
    def _():
        q = norm(_dot(h_cur(), w_ref[...]), qg_ref[...])
        s_ref[...] = (q * (LOG2E / math.sqrt(HEAD_DIM))).astype(BF16)
        prepare()

    @pl.when((i > 0) & (j == n_rkv + 2))
    def _():
        s_ref[...] = norm(_dot(h_cur(), w_ref[...]), kg_ref[...]).astype(BF16)
        prepare()

    @pl.when((i > 0) & (j == n_rkv + 3))
    def _():
        s_ref[...] = _dot(h_cur(), w_ref[...]).astype(BF16)
        prepare()


def _inproj(x2, gain, sc, sh, w_in16, w_lora, qg, kg, hsum, seq, n_rkv, tm=1024):
    m, d = x2.shape
    tn = qg.shape[1]
    assert w_lora.shape[1] == tn and w_in16.shape[1] == (n_rkv + 3) * tn
    nrow = m // tm
    nt = n_rkv + 4
    rows = tm // INPROJ_SLICES
    per_seq = seq // tm
    const = lambda a: pl.BlockSpec(a.shape, lambda i, j: (0,) * a.ndim)
    batch = lambda i, j: (jnp.minimum(i, nrow - 1) // per_seq, 0, 0)
    prev_row = lambda i: jnp.maximum(i - 1, 0)
    return pl.pallas_call(
        functools.partial(_inproj_kernel, n_rkv=n_rkv),
        out_shape=(jax.ShapeDtypeStruct((m, (n_rkv + 1) * tn), F32),
                   jax.ShapeDtypeStruct((m, 3 * tn), BF16)),
        grid=(nrow + 1, nt),
        in_specs=[pl.BlockSpec((rows, d), lambda i, j: (jnp.minimum(
                      INPROJ_SLICES * i + jnp.clip(j - 1, 0, INPROJ_SLICES - 1),
                      INPROJ_SLICES * nrow - 1), 0)),
                  const(gain),
                  pl.BlockSpec((1, 1, d), batch),
                  pl.BlockSpec((1, 1, d), batch),
                  pl.BlockSpec((d, tn), lambda i, j: (0, jnp.where(j < n_rkv, j, jnp.maximum(j - 1, n_rkv - 1)))),
                  const(w_lora), const(qg), const(kg), const(hsum)],
        out_specs=(pl.BlockSpec((tm, tn), lambda i, j: (prev_row(i), jnp.where(i == 0, 0, jnp.minimum(j, n_rkv)))),
                   pl.BlockSpec((tm, tn), lambda i, j: (prev_row(i), jnp.where(i == 0, 0, jnp.maximum(j - n_rkv - 1, 0))))),
        scratch_shapes=[pltpu.VMEM((2, tm, d), BF16)],
        compiler_params=_cparams(2, VMEM_LIMIT_BIG),
        name="inproj",
    )(x2, gain, sc, sh, w_in16, w_lora, qg, kg, hsum)


def _shift_rows(cur, prev_row):
    rolled = pltpu.roll(cur, 1, axis=0)
    row = lax.broadcasted_iota(jnp.int32, cur.shape, 0)
    return jnp.where(row == 0, jnp.broadcast_to(prev_row, cur.shape), rolled)


_PREP = ("at", "rt", "bt", "kt", "v", "bh", "kh", "bonus", "g")


def _rwkv_kernel(p_ref, l_ref, mu_ref, w0_ref, w2_ref, a0_ref, a2_ref, g2_ref, kk_ref, ka_ref,
                 rk_ref, lng_ref, lnb_ref, hs_ref, cast_ref, o_ref, cast_out_ref, s_scr, prevp_scr,
                 prevl_scr, y_scr, prep_scr, gend_scr, *, n_heads):
    dr = n_heads * HEAD_DIM
    c = CHUNK
    nc = RWKV_CHUNKS
    d = HEAD_DIM
    heads = range(n_heads)
    cast_out_ref[...] = cast_ref[...].astype(BF16)
    units = [(ck, h) for ck in range(nc) for h in heads]
    sls = [slice(h * d, (h + 1) * d) for h in heads]
    rws = [slice(ck * c, (ck + 1) * c) for ck in range(nc)]

    @pl.when(pl.program_id(1) == 0)
    def _():
        s_scr[...] = jnp.zeros_like(s_scr)
        prevp_scr[...] = jnp.zeros_like(prevp_scr)
        prevl_scr[...] = jnp.zeros_like(prevl_scr)
        prep_scr[...] = jnp.zeros_like(prep_scr)
        gend_scr[...] = jnp.zeros_like(gend_scr)

    headsum = functools.partial(_headsum, hs=hs_ref[...])

    def prev(name, ck, h):
        i = _PREP.index(name)
        pair = prep_scr[i, rws[ck], (h // 2) * LANES:(h // 2 + 1) * LANES]
        return pair[:, (h % 2) * d:(h % 2 + 1) * d]

    rows2 = lax.broadcasted_iota(jnp.int32, (2 * c, 2 * c), 0)
    cols2 = lax.broadcasted_iota(jnp.int32, (2 * c, 2 * c), 1)
    tt = jnp.where(rows2 >= c, rows2 - c, rows2)
    ii = jnp.where(cols2 >= c, cols2 - c, cols2)
    quad_mask = tt + jnp.where(rows2 >= c, 1, 0) > ii

    m4 = {}
    for un in units:
        ar = jnp.concatenate([prev("at", *un), prev("rt", *un)], axis=0)
        bk = jnp.concatenate([prev("bt", *un), prev("kt", *un)], axis=0)
        m4[un] = jnp.where(quad_mask, _mmb(ar, bk, nt=True), 0.0)

    p = p_ref[0]
    tb = nc * c
    pshift = _shift_rows(p, prevp_scr[...])
    prevp_scr[...] = p[tb - 1:tb, :]
    pm = p + (pshift - p) * mu_ref[...]
    r = pm[:, :dr]
    k = pm[:, dr:2 * dr]
    v = pm[:, 2 * dr:]

    l = l_ref[0]
    half = l.shape[1] // 2
    lb = l[:, half:]
    lin = l[:, :half] + _shift_rows(lb, prevl_scr[...])
    prevl_scr[...] = lb[tb - 1:tb, :]
    nw = w2_ref.shape[0]
    na = a2_ref.shape[0]
    lw = jnp.tanh(lin[:, :nw])
    la = lin[:, nw:nw + na]
    lg = _sigmoid(lin[:, nw + na:])
    w_pre = w0_ref[...] + _dot(lw.astype(BF16), w2_ref[...])
    w_log = -_softplus(-w_pre) - 0.5
    logw = -jnp.exp(w_log)
    a_gate = _sigmoid(a0_ref[...] + _dot(la.astype(BF16), a2_ref[...]))
    g = _dot(lg.astype(BF16), g2_ref[...])
    kk = k * kk_ref[...]
    kk = kk * lax.rsqrt(headsum(kk * kk) + L2_EPS)
    k2 = k * (1.0 + (a_gate - 1.0) * ka_ref[...])
    avec = -kk
    bvec = kk * a_gate

    akv_yk = {un: _mmb(m4[un][:, c:], prev("v", *un)) for un in units}

    ti = lax.broadcasted_iota(jnp.int32, (tb, tb), 0)
    si = lax.broadcasted_iota(jnp.int32, (tb, tb), 1)
    same_chunk = sum(((ti >= ck * c) & (ti < (ck + 1) * c) & (si >= ck * c)) for ck in range(nc)) > 0
    tri = jnp.where(same_chunk & (ti >= si), 1.0, 0.0).astype(BF16)
    l1, l2 = _split(logw)
    lgc = _dot(tri, l1) + _dot(tri, l2)
    lg_ends = [lgc[(ck + 1) * c - 1:(ck + 1) * c, :] for ck in range(nc)]
    lg_end = jnp.concatenate([jnp.broadcast_to(e, (c, dr)) for e in lg_ends], axis=0)
    ginv = jnp.exp(-lgc)
    gend = jnp.exp(lg_end - lgc)
    new = {"at": avec * jnp.exp(lgc - logw), "rt": r * jnp.exp(lgc), "bt": bvec * ginv,
           "kt": k2 * ginv, "v": v, "bh": bvec * gend, "kh": k2 * gend, "g": g}
    new_gend = jnp.exp(jnp.concatenate(lg_ends, axis=0))

    zpad = jnp.zeros((c, d), F32)
    w = {un: jnp.concatenate([m4[un][:c, :c], zpad, prev("at", *un), akv_yk[un][:c]], axis=1)
         for un in units}
    for lvl in range(6):
        rhs = w if lvl < 5 else {un: w[un][:, 2 * d:] for un in units}
        prod = {un: _mmb(w[un][:, :d], rhs[un]) for un in units}
        if lvl < 5:
            w = {un: jnp.concatenate([prod[un][:, :2 * d], prod[un][:, 2 * d:] + w[un][:, 2 * d:]],
                                     axis=1) for un in units}
        else:
            w = {un: prod[un] + w[un][:, 2 * d:] for un in units}

    new["bonus"] = headsum(r * k2 * rk_ref[...]) * v

    g_end = gend_scr[...]
    s = [s_scr[h] for h in heads]
    for ck in range(nc):
        su = [_mmb(jnp.concatenate([w[ck, h][:, :d], prev("rt", ck, h)], axis=0), s[h], nt=True)
              for h in heads]
        u = [su[h][:c] + w[ck, h][:, d:] for h in heads]
        rbu = [_mmb(m4[ck, h][c:, :c], u[h]) for h in heads]
        upd = [_mmb(jnp.concatenate([u[h], prev("v", ck, h)], axis=0).T,
                    jnp.concatenate([prev("bh", ck, h), prev("kh", ck, h)], axis=0))
               for h in heads]
        for h in heads:
            y_scr[rws[ck], sls[h]] = su[h][c:] + rbu[h] + akv_yk[ck, h][c:]
        s = [s[h] * g_end[ck:ck + 1, sls[h]] + upd[h] for h in heads]
    for h in heads:
        s_scr[h] = s[h]

    y = y_scr[...]
    inv_n = 1.0 / HEAD_DIM
    mean = headsum(y) * inv_n
    yc = y - mean
    var = headsum(yc * yc) * inv_n
    yn = yc * lax.rsqrt(var + GN_EPS) * lng_ref[...] + lnb_ref[...]
    o_ref[0] = ((yn + prep_scr[_PREP.index("bonus")]) * prep_scr[_PREP.index("g")]).astype(o_ref.dtype)

    for i, name in enumerate(_PREP):
        prep_scr[i] = new[name]
    gend_scr[...] = new_gend


def _rwkv(p3, mu_rkv, w0, w2p, a0, a2p, g2p, k_k, k_a, r_k, ln_g, ln_b, hsum, n_heads, lora_w, w_cast):
    b, t, _ = p3.shape
    dr = n_heads * HEAD_DIM
    c = CHUNK * RWKV_CHUNKS
    nt = t // c
    lora_blk = (3 * dr) // lora_w
    row = lambda n: pl.BlockSpec((1, n), lambda i, j: (0, 0))
    full = lambda a: pl.BlockSpec(a.shape, lambda i, j: (0, 0))
    slab_rows = _slab_rows(w_cast.shape[0], b * (nt + 1))
    n_slab = w_cast.shape[0] // slab_rows
    slab = pl.BlockSpec((slab_rows, w_cast.shape[1]),
                        lambda i, j: (jnp.minimum(i * (nt + 1) + j, n_slab - 1), 0))
    return pl.pallas_call(
        functools.partial(_rwkv_kernel, n_heads=n_heads),
        out_shape=(jax.ShapeDtypeStruct((b, t, dr), BF16), jax.ShapeDtypeStruct(w_cast.shape, BF16)),
        grid=(b, nt + 1),
        in_specs=[pl.BlockSpec((1, c, 3 * dr), lambda i, j: (i, jnp.minimum(j, nt - 1), 0)),
                  pl.BlockSpec((1, c, lora_w), lambda i, j: (i, jnp.minimum(j, nt - 1), lora_blk)),
                  row(3 * dr), row(dr), full(w2p), row(dr), full(a2p), full(g2p),
                  row(dr), row(dr), row(dr), row(dr), row(dr), full(hsum), slab],
        out_specs=(pl.BlockSpec((1, c, dr), lambda i, j: (i, jnp.maximum(j - 1, 0), 0)), slab),
        scratch_shapes=[pltpu.VMEM((n_heads, HEAD_DIM, HEAD_DIM), F32),
                        pltpu.VMEM((1, 3 * dr), F32),
                        pltpu.VMEM((1, lora_w // 2), F32),
                        pltpu.VMEM((c, dr), F32),
                        pltpu.VMEM((len(_PREP), c, dr), F32),
                        pltpu.VMEM((RWKV_CHUNKS, dr), F32)],
        compiler_params=_cparams(2),
        name="rwkv",
    )(p3, p3, mu_rkv, w0, w2p, a0, a2p, g2p, k_k, k_a, r_k, ln_g, ln_b, hsum, w_cast)


def _sbattn_kernel(q_ref, k_ref, v_ref, ca_ref, cb_ref, o_ref, ca_out_ref, cb_out_ref, *, tq, tk, nb, npairs):
    qi = pl.program_id(2)
    nsub = tq // tk
    ca_out_ref[...] = ca_ref[...].astype(BF16)
    cb_out_ref[...] = cb_ref[...].astype(BF16)
    first = lax.broadcasted_iota(jnp.int32, (tk, LANES), 1) < HEAD_DIM
    rr = lax.broadcasted_iota(jnp.int32, (2 * tk, 2 * tk), 0)
    cc = lax.broadcasted_iota(jnp.int32, (2 * tk, 2 * tk), 1)
    tri2 = jnp.where((rr >= cc) & ((rr < tk) == (cc < tk)), 1.0, 0.0).astype(BF16)

    def per_head_rows(blk):
        zero = jnp.zeros_like(blk)
        return jnp.concatenate([jnp.where(first, blk, zero), jnp.where(first, zero, blk)], axis=0)

    def add_rows(full, r0, delta):
        if r0 == 0:
            return full + delta
        return jnp.concatenate([full[:r0], full[r0:] + delta], axis=0)

    def blocks(starts, state, row0s, masks):
        jobs = [(p, u) for p in range(npairs) for u in range(len(starts))]
        lanes = [slice(p * LANES, (p + 1) * LANES) for p in range(npairs)]
        kcs = [per_head_rows(k_ref[0, pl.ds(starts[u], tk), lanes[p]]) for p, u in jobs]
        vcs = [per_head_rows(v_ref[0, pl.ds(starts[u], tk), lanes[p]]) for p, u in jobs]
        zs = [_dot_nt(q_ref[0, row0s[u]:, lanes[p]], kc) for (p, u), kc in zip(jobs, kcs)]
        sps = []
        for (p, u), z in zip(jobs, zs):
            sp = jnp.maximum(z, 0.0) + jnp.log(1.0 + jnp.exp2(-jnp.abs(z))) * LOG2E
            if masks[u] is not None:
                sp = jnp.where(masks[u], sp, 0.0)
            sps.append(sp.astype(BF16))
        css = [_dot(sp, tri2) for sp in sps]
        carries = [c for c, _ in state]
        pvs = []
        for (p, u), z, cs, vc in zip(jobs, zs, css, vcs):
            r0 = row0s[u]
            rows = tq - r0
            attn = jnp.exp2(jnp.minimum(z - cs, 0.0) - carries[p][r0:])
            if masks[u] is not None:
                attn = jnp.where(masks[u], attn, 0.0)
            pvs.append(_dot(attn.astype(BF16), vc))
            tot = jnp.concatenate([jnp.broadcast_to(cs[:, 0:1], (rows, tk)),
                                   jnp.broadcast_to(cs[:, tk:tk + 1], (rows, tk))], axis=1)
            carries[p] = add_rows(carries[p], r0, tot)
        accs = [a for _, a in state]
        for (p, u), pv in zip(jobs, pvs):
            accs[p] = add_rows(accs[p], row0s[u], pv)
        return tuple(zip(carries, accs))

    state = tuple((jnp.zeros((tq, 2 * tk), F32), jnp.zeros((tq, LANES), F32)) for _ in range(npairs))
    assert nsub == nb, "one group of diagonal blocks"
    subs = [nsub - 1 - u for u in range(nb)]
    starts = [pl.multiple_of(qi * tq + sub * tk, tk) for sub in subs]
    masks = []
    for sub in subs:
        rows = tq - sub * tk
        qpos = lax.broadcasted_iota(jnp.int32, (rows, 2 * tk), 0)
        col = lax.broadcasted_iota(jnp.int32, (rows, 2 * tk), 1)
        masks.append(jnp.where(col >= tk, col - tk, col) < qpos)
    base = jnp.maximum(qi * tq - nb * tk, 0)
    starts += [pl.multiple_of(base + (nb - 1 - u) * tk, tk) for u in range(nb)]
    has_earlier = jnp.broadcast_to(qi > 0, (tq, 2 * tk))
    state = blocks(starts, state, [sub * tk for sub in subs] + [0] * nb, masks + [has_earlier] * nb)

    def min_carry(st):
        m = st[0][0]
        for c_, _ in st[1:]:
            m = jnp.minimum(m, c_)
        return jnp.min(jnp.minimum(m[:, :tk], m[:, tk:]))

    n_steps = qi * (nsub // nb)

    def cond(loop):
        i, _, cmin = loop
        return (i < n_steps) & (cmin < SB_DEAD)

    def body(loop):
        i, st, _ = loop
        base = qi * tq - (i + 1) * (nb * tk)
        starts = [pl.multiple_of(base + (nb - 1 - u) * tk, tk) for u in range(nb)]
        st = blocks(starts, st, [0] * nb, [None] * nb)
        return i + 1, st, min_carry(st)

    _, state, _ = lax.while_loop(cond, body, (jnp.int32(1), state, min_carry(state)))
    o_ref[0] = jnp.concatenate([a for _, a in state], axis=1).astype(o_ref.dtype)


def _sbattn(qkv, cast_a, cast_b, tq=256, tk=128, nb=2, npairs=4):
    b, t, ds3 = qkv.shape
    ds = ds3 // 3
    w = npairs * LANES
    ng = ds // w
    nq = t // tq
    n_steps = b * ng * nq

    def slab(a):
        slab_rows = _slab_rows(a.shape[0], n_steps)
        n_slab = a.shape[0] // slab_rows
        return pl.BlockSpec((slab_rows, a.shape[1]),
                            lambda i, h, j: (jnp.minimum((i * ng + h) * nq + j, n_slab - 1), 0))

    return pl.pallas_call(
        functools.partial(_sbattn_kernel, tq=tq, tk=tk, nb=nb, npairs=npairs),
        out_shape=(jax.ShapeDtypeStruct((b, t, ds), BF16),
                   jax.ShapeDtypeStruct(cast_a.shape, BF16), jax.ShapeDtypeStruct(cast_b.shape, BF16)),
        grid=(b, ng, nq),
        in_specs=[pl.BlockSpec((1, tq, w), lambda i, h, j: (i, j, h)),
                  pl.BlockSpec((1, t, w), lambda i, h, j: (i, 0, ng + h)),
                  pl.BlockSpec((1, t, w), lambda i, h, j: (i, 0, 2 * ng + h)),
                  slab(cast_a), slab(cast_b)],
        out_specs=(pl.BlockSpec((1, tq, w), lambda i, h, j: (i, j, h)), slab(cast_a), slab(cast_b)),
        compiler_params=_cparams(3),
        name="sbattn",
    )(qkv, qkv, qkv, cast_a, cast_b)


def _outproj_kernel(yr_ref, ys_ref, w_ref, x_ref, gt_ref, g2_ref, sc_ref, sh_ref, x1_ref, h2_ref):
    dr = yr_ref.shape[1]
    mix = _dot(yr_ref[...], w_ref[:dr, :]) + _dot(ys_ref[...], w_ref[dr:, :])
    x1 = x_ref[...] + gt_ref[0] * mix
    x1_ref[...] = x1
    ms = jnp.mean(x1 * x1, axis=-1, keepdims=True)
    y = x1 * lax.rsqrt(ms + RMS_EPS) * g2_ref[...]
    h2_ref[...] = (y * (1.0 + sc_ref[0]) + sh_ref[0]).astype(BF16)


def _outproj(yr, ys, w_out, x2, gt1, gain2, sc2, sh2, seq, tm=512):
    m, d = x2.shape
    dr = yr.shape[1]
    ds = ys.shape[1]
    per_seq = seq // tm
    mod = pl.BlockSpec((1, 1, d), lambda i: (i // per_seq, 0, 0))
    return pl.pallas_call(
        _outproj_kernel,
        out_shape=(jax.ShapeDtypeStruct((m, d), F32), jax.ShapeDtypeStruct((m, d), BF16)),
        grid=(m // tm,),
        in_specs=[pl.BlockSpec((tm, dr), lambda i: (i, 0)),
                  pl.BlockSpec((tm, ds), lambda i: (i, 0)),
                  pl.BlockSpec(w_out.shape, lambda i: (0, 0)),
                  pl.BlockSpec((tm, d), lambda i: (i, 0)),
                  mod,
                  pl.BlockSpec((1, d), lambda i: (0, 0)),
                  mod, mod],
        out_specs=(pl.BlockSpec((tm, d), lambda i: (i, 0)), pl.BlockSpec((tm, d), lambda i: (i, 0))),
        compiler_params=_cparams(1),
        name="outproj",
    )(yr, ys, w_out, x2, gt1, gain2, sc2, sh2)


def _ffn_kernel(h_ref, wg_ref, wu_ref, wd_ref, x_ref, gt_ref, o_ref, acc_scr):
    j = pl.program_id(1)

    @pl.when(j == 0)
    def _():
        acc_scr[...] = jnp.zeros_like(acc_scr)

    h = h_ref[...]
    gate = _dot(h, wg_ref[...])
    up = _dot(h, wu_ref[...])
    act = (gate * _sigmoid(gate) * up).astype(BF16)
    acc_scr[...] += _dot(act, wd_ref[...])

    @pl.when(j == pl.num_programs(1) - 1)
    def _():
        o_ref[...] = x_ref[...] + gt_ref[0] * acc_scr[...]


def _ffn(h2, w_gu, w_down, x1, gt2, seq, tm=512, tf=512):
    m, d = h2.shape
    dff = w_down.shape[0]
    nf = dff // tf
    per_seq = seq // tm
    return pl.pallas_call(
        _ffn_kernel,
        out_shape=jax.ShapeDtypeStruct((m, d), F32),
        grid=(m // tm, nf),
        in_specs=[pl.BlockSpec((tm, d), lambda i, j: (i, 0)),
                  pl.BlockSpec((d, tf), lambda i, j: (0, j)),
                  pl.BlockSpec((d, tf), lambda i, j: (0, j + nf)),
                  pl.BlockSpec((tf, d), lambda i, j: (j, 0)),
                  pl.BlockSpec((tm, d), lambda i, j: (i, 0)),
                  pl.BlockSpec((1, 1, d), lambda i, j: (i // per_seq, 0, 0))],
        out_specs=pl.BlockSpec((tm, d), lambda i, j: (i, 0)),
        scratch_shapes=[pltpu.VMEM((tm, d), F32)],
        compiler_params=_cparams(2),
        name="ffn",
    )(h2, w_gu, w_gu, w_down, x1, gt2)


def _pad_to(a, n, axis):
    pad = [(0, 0)] * a.ndim
    pad[axis] = (0, n - a.shape[axis])
    return jnp.pad(a, pad)


def _layer(x, c_pad, w_ada, b_ada, norm1_gain, norm2_gain, w_in, mu_rkv, mu_w, mu_a, mu_g, w0, w1,
           w2, a0, a1, a2, g1, g2, k_k, k_a, r_k, ln_x_gain, ln_x_bias, q_norm_gain, k_norm_gain,
           w_out, w_gate_up, w_down):
    b, t, d = x.shape
    dr = w0.shape[0]
    ds = d - dr
    n_rwkv = dr // HEAD_DIM
    n_sb = ds // HEAD_DIM
    row = lambda a: a.reshape(1, -1)

    mod = _ada(c_pad, w_ada, row(b_ada))[:b]
    sh1, sc1, gt1, sh2, sc2, gt2 = [m.reshape(b, 1, d) for m in jnp.split(mod, 6, axis=-1)]

    nw = -(-w1.shape[1] // LANES) * LANES
    na = -(-a1.shape[1] // LANES) * LANES
    ng = -(-g1.shape[1] // LANES) * LANES
    w_lora = _fold(w1, a1, g1, mu_w, mu_a, mu_g, (0, nw, nw + na), nw + na + ng)
    lora_w = w_lora.shape[1]
    assert dr == ds == lora_w, "column tiles of the input projection are one head group wide"

    hsum = jnp.kron(jnp.eye(MXU_N // HEAD_DIM, dtype=F32),
                    jnp.ones((HEAD_DIM, HEAD_DIM), F32)).astype(BF16)
    x2 = x.reshape(b * t, d)
    p, qkv = _inproj(x2, row(norm1_gain), sc1, sh1, w_in.astype(BF16), w_lora,
                     row(jnp.tile(q_norm_gain, n_sb)), row(jnp.tile(k_norm_gain, n_sb)), hsum, t,
                     3 * dr // lora_w)
    p3 = p.reshape(b, t, -1)
    y_rwkv, w_gu16 = _rwkv(p3, row(mu_rkv), row(w0), _pad_to(w2, nw, 0).astype(BF16), row(a0),
                           _pad_to(a2, na, 0).astype(BF16), _pad_to(g2, ng, 0).astype(BF16),
                           row(k_k), row(k_a), row(r_k), row(ln_x_gain), row(ln_x_bias), hsum,
                           n_rwkv, lora_w, w_gate_up)

    y_sb, w_down16, w_out16 = _sbattn(qkv.reshape(b, t, -1), w_down.reshape(d, -1), w_out)

    x1, h2 = _outproj(y_rwkv.reshape(b * t, dr), y_sb.reshape(b * t, ds), w_out16, x2,
                      gt1, row(norm2_gain), sc2, sh2, t)
    out = _ffn(h2, w_gu16, w_down16.reshape(w_down.shape), x1, gt2, t)
    return out.reshape(b, t, d)


def kernel(x, c, w_ada, b_ada, norm1_gain, norm2_gain, w_in, mu_rkv, mu_w, mu_a, mu_g, w0, w1, w2,
           a0, a1, a2, g1, g2, k_k, k_a, r_k, ln_x_gain, ln_x_bias, q_norm_gain, k_norm_gain, w_out,
           w_gate_up, w_down):
    depth = w_ada.shape[0]
    c_pad = _pad_to(c, SUBLANES, 0)
    for l in range(depth):
        x = _layer(x, c_pad, w_ada[l], b_ada[l], norm1_gain[l], norm2_gain[l], w_in[l], mu_rkv[l],
                   mu_w[l], mu_a[l], mu_g[l], w0[l], w1[l], w2[l], a0[l], a1[l], a2[l], g1[l],
                   g2[l], k_k[l], k_a[l], r_k[l].reshape(-1), ln_x_gain[l], ln_x_bias[l],
                   q_norm_gain[l], k_norm_gain[l], w_out[l], w_gate_up[l], w_down[l])
    return x
```

```python
import functools
import math

import jax
import jax.numpy as jnp
from jax import lax
from jax.experimental import pallas as pl
from jax.experimental.pallas import tpu as pltpu

F32 = jnp.float32
BF16 = jnp.bfloat16

HEAD_DIM = 64
RMS_EPS = 1e-6
GN_EPS = 64e-5
L2_EPS = 1e-12
LOG2E = math.log2(math.e)
SB_DEAD = 150.0
LANES = 128
SUBLANES = 8
MXU_N = 256
CHUNK = 64
RWKV_CHUNKS = 2
VMEM_LIMIT = 48 * 1024 * 1024
VMEM_LIMIT_BIG = 56 * 1024 * 1024


def _cparams(n_axes, vmem_limit=VMEM_LIMIT):
    return pltpu.CompilerParams(dimension_semantics=("arbitrary",) * n_axes,
                                vmem_limit_bytes=vmem_limit)


def _sigmoid(x):
    return 1.0 / (1.0 + jnp.exp(-x))


def _softplus(x):
    return jnp.maximum(x, 0.0) + jnp.log(1.0 + jnp.exp(-jnp.abs(x)))


def _dot(x, y):
    return jnp.dot(x, y, preferred_element_type=F32)


def _dot_nt(x, y):
    return lax.dot_general(x, y, (((1,), (1,)), ((), ())), preferred_element_type=F32)


def _split(x):
    hi = x.astype(BF16)
    lo = (x - hi.astype(F32)).astype(BF16)
    return hi, lo


def _mmb(x, y, nt=False):
    return (_dot_nt if nt else _dot)(x.astype(BF16), y.astype(BF16))


def _headsum(t, hs):
    g = hs.shape[0]
    return jnp.concatenate([_dot(t[:, i:i + g].astype(BF16), hs) for i in range(0, t.shape[1], g)],
                           axis=1)


def _ada_kernel(c_ref, w_ref, b_ref, o_ref):
    c = c_ref[...]
    ca = c * _sigmoid(c)
    o_ref[...] = _dot(ca.astype(BF16), w_ref[...].astype(BF16)) + b_ref[...]


def _ada(c_pad, w_ada, b_ada, tn=1024):
    m, d = c_pad.shape
    n = w_ada.shape[1]
    return pl.pallas_call(
        _ada_kernel,
        out_shape=jax.ShapeDtypeStruct((m, n), F32),
        grid=(n // tn,),
        in_specs=[pl.BlockSpec((m, d), lambda j: (0, 0)),
                  pl.BlockSpec((d, tn), lambda j: (0, j)),
                  pl.BlockSpec((1, tn), lambda j: (0, j))],
        out_specs=pl.BlockSpec((m, tn), lambda j: (0, j)),
        compiler_params=_cparams(1),
        name="ada",
    )(c_pad, w_ada, b_ada)


def _fold_kernel(w1_ref, a1_ref, g1_ref, muw_ref, mua_ref, mug_ref, o_ref, *, offsets):
    o_ref[...] = jnp.zeros_like(o_ref)
    half = o_ref.shape[1] // 2
    for w_ref, mu_ref, off in zip((w1_ref, a1_ref, g1_ref), (muw_ref, mua_ref, mug_ref), offsets):
        w = w_ref[...]
        mu = mu_ref[...]
        n = w.shape[1]
        o_ref[:, off:off + n] = (w * (1.0 - mu)).astype(BF16)
        o_ref[:, half + off:half + off + n] = (w * mu).astype(BF16)


def _fold(w1, a1, g1, mu_w, mu_a, mu_g, offsets, half):
    d = w1.shape[0]
    col = lambda m: m.reshape(d, 1)
    return pl.pallas_call(
        functools.partial(_fold_kernel, offsets=offsets),
        out_shape=jax.ShapeDtypeStruct((d, 2 * half), BF16),
        compiler_params=pltpu.CompilerParams(vmem_limit_bytes=VMEM_LIMIT),
        name="fold",
    )(w1, a1, g1, col(mu_w), col(mu_a), col(mu_g))


INPROJ_SLICES = 4
CAST_ROWS = 16


def _slab_rows(n_rows, n_steps):
    units = n_rows // CAST_ROWS
    assert units * CAST_ROWS == n_rows
    for k in range(1, units + 1):
        if units % k == 0 and units // k <= n_steps:
            return k * CAST_ROWS
    raise ValueError("no slab height fits")


def _inproj_kernel(x_ref, g_ref, sc_ref, sh_ref, w_ref, wl_ref, qg_ref, kg_ref, hs_ref, p_ref, s_ref,
                   h_scr, *, n_rkv):
    i = pl.program_id(0)
    j = pl.program_id(1)
    rows = x_ref.shape[0]

    def prepare():
        x = x_ref[...]
        ms = jnp.mean(x * x, axis=-1, keepdims=True)
        y = x * lax.rsqrt(ms + RMS_EPS) * g_ref[...]
        sl = jnp.clip(j - 1, 0, INPROJ_SLICES - 1)
        start = pl.multiple_of(sl * rows, rows)
        h_scr[i % 2, pl.ds(start, rows), :] = (y * (1.0 + sc_ref[0]) + sh_ref[0]).astype(BF16)

    def h_cur():
        return h_scr[(i + 1) % 2]

    def norm(t, gain):
        ms = _headsum(t * t, hs_ref[...]) * (1.0 / HEAD_DIM)
        return t * lax.rsqrt(ms + RMS_EPS) * gain

    @pl.when(i == 0)
    def _():
        prepare()

    @pl.when((i > 0) & (j < n_rkv))
    def _():
        p_ref[...] = _dot(h_cur(), w_ref[...])
        prepare()

    @pl.when((i > 0) & (j == n_rkv))
    def _():
        p_ref[...] = _dot(h_cur(), wl_ref[...])
        prepare()

    @pl.when((i > 0) & (j == n_rkv + 1))
    def _():
        q = norm(_dot(h_cur(), w_ref[...]), qg_ref[...])
        s_ref[...] = (q * (LOG2E / math.sqrt(HEAD_DIM))).astype(BF16)
        prepare()

    @pl.when((i > 0) & (j == n_rkv + 2))
    def _():
        s_ref[...] = norm(_dot(h_cur(), w_ref[...]), kg_ref[...]).astype(BF16)
        prepare()

    @pl.when((i > 0) & (j == n_rkv + 3))
    def _():
        s_ref[...] = _dot(h_cur(), w_ref[...]).astype(BF16)
        prepare()


def _inproj(x2, gain, sc, sh, w_in16, w_lora, qg, kg, hsum, seq, n_rkv, tm=1024):
    m, d = x2.shape
    tn = qg.shape[1]
    assert w_lora.shape[1] == tn and w_in16.shape[1] == (n_rkv + 3) * tn
    nrow = m // tm
    nt = n_rkv + 4
    rows = tm // INPROJ_SLICES
    per_seq = seq // tm
    const = lambda a: pl.BlockSpec(a.shape, lambda i, j: (0,) * a.ndim)
    batch = lambda i, j: (jnp.minimum(i, nrow - 1) // per_seq, 0, 0)
    prev_row = lambda i: jnp.maximum(i - 1, 0)
    return pl.pallas_call(
        functools.partial(_inproj_kernel, n_rkv=n_rkv),
        out_shape=(jax.ShapeDtypeStruct((m, (n_rkv + 1) * tn), F32),
                   jax.ShapeDtypeStruct((m, 3 * tn), BF16)),
        grid=(nrow + 1, nt),
        in_specs=[pl.BlockSpec((rows, d), lambda i, j: (jnp.minimum(
                      INPROJ_SLICES * i + jnp.clip(j - 1, 0, INPROJ_SLICES - 1),
                      INPROJ_SLICES * nrow - 1), 0)),
                  const(gain),
                  pl.BlockSpec((1, 1, d), batch),
                  pl.BlockSpec((1, 1, d), batch),
                  pl.BlockSpec((d, tn), lambda i, j: (0, jnp.where(j < n_rkv, j, jnp.maximum(j - 1, n_rkv - 1)))),
                  const(w_lora), const(qg), const(kg), const(hsum)],
        out_specs=(pl.BlockSpec((tm, tn), lambda i, j: (prev_row(i), jnp.where(i == 0, 0, jnp.minimum(j, n_rkv)))),
                   pl.BlockSpec((tm, tn), lambda i, j: (prev_row(i), jnp.where(i == 0, 0, jnp.maximum(j - n_rkv - 1, 0))))),
        scratch_shapes=[pltpu.VMEM((2, tm, d), BF16)],
        compiler_params=_cparams(2, VMEM_LIMIT_BIG),
        name="inproj",
    )(x2, gain, sc, sh, w_in16, w_lora, qg, kg, hsum)


def _shift_rows(cur, prev_row):
    rolled = pltpu.roll(cur, 1, axis=0)
    row = lax.broadcasted_iota(jnp.int32, cur.shape, 0)
    return jnp.where(row == 0, jnp.broadcast_to(prev_row, cur.shape), rolled)


_PREP = ("at", "rt", "bt", "kt", "v", "bh", "kh", "bonus", "g")


def _rwkv_kernel(p_ref, l_ref, mu_ref, w0_ref, w2_ref, a0_ref, a2_ref, g2_ref, kk_ref, ka_ref,
                 rk_ref, lng_ref, lnb_ref, hs_ref, cast_ref, o_ref, cast_out_ref, s_scr, prevp_scr,
                 prevl_scr, y_scr, prep_scr, gend_scr, *, n_heads):
    dr = n_heads * HEAD_DIM
    c = CHUNK
    nc = RWKV_CHUNKS
    d = HEAD_DIM
    heads = range(n_heads)
    cast_out_ref[...] = cast_ref[0].astype(BF16)
    units = [(ck, h) for ck in range(nc) for h in heads]
    sls = [slice(h * d, (h + 1) * d) for h in heads]
    rws = [slice(ck * c, (ck + 1) * c) for ck in range(nc)]

    @pl.when(pl.program_id(1) == 0)
    def _():
        s_scr[...] = jnp.zeros_like(s_scr)
        prevp_scr[...] = jnp.zeros_like(prevp_scr)
        prevl_scr[...] = jnp.zeros_like(prevl_scr)
        prep_scr[...] = jnp.zeros_like(prep_scr)
        gend_scr[...] = jnp.zeros_like(gend_scr)

    headsum = functools.partial(_headsum, hs=hs_ref[...])

    def prev(name, ck, h):
        i = _PREP.index(name)
        pair = prep_scr[i, rws[ck], (h // 2) * LANES:(h // 2 + 1) * LANES]
        return pair[:, (h % 2) * d:(h % 2 + 1) * d]

    rows2 = lax.broadcasted_iota(jnp.int32, (2 * c, 2 * c), 0)
    cols2 = lax.broadcasted_iota(jnp.int32, (2 * c, 2 * c), 1)
    tt = jnp.where(rows2 >= c, rows2 - c, rows2)
    ii = jnp.where(cols2 >= c, cols2 - c, cols2)
    quad_mask = tt + jnp.where(rows2 >= c, 1, 0) > ii

    m4 = {}
    for un in units:
        ar = jnp.concatenate([prev("at", *un), prev("rt", *un)], axis=0)
        bk = jnp.concatenate([prev("bt", *un), prev("kt", *un)], axis=0)
        m4[un] = jnp.where(quad_mask, _mmb(ar, bk, nt=True), 0.0)

    p = p_ref[0]
    tb = nc * c
    pshift = _shift_rows(p, prevp_scr[...])
    prevp_scr[...] = p[tb - 1:tb, :]
    pm = p + (pshift - p) * mu_ref[...]
    r = pm[:, :dr]
    k = pm[:, dr:2 * dr]
    v = pm[:, 2 * dr:]

    l = l_ref[0]
    half = l.shape[1] // 2
    lb = l[:, half:]
    lin = l[:, :half] + _shift_rows(lb, prevl_scr[...])
    prevl_scr[...] = lb[tb - 1:tb, :]
    nw = w2_ref.shape[0]
    na = a2_ref.shape[0]
    lw = jnp.tanh(lin[:, :nw])
    la = lin[:, nw:nw + na]
    lg = _sigmoid(lin[:, nw + na:])
    w_pre = w0_ref[...] + _dot(lw.astype(BF16), w2_ref[...])
    w_log = -_softplus(-w_pre) - 0.5
    logw = -jnp.exp(w_log)
    a_gate = _sigmoid(a0_ref[...] + _dot(la.astype(BF16), a2_ref[...]))
    g = _dot(lg.astype(BF16), g2_ref[...])
    kk = k * kk_ref[...]
    kk = kk * lax.rsqrt(headsum(kk * kk) + L2_EPS)
    k2 = k * (1.0 + (a_gate - 1.0) * ka_ref[...])
    avec = -kk
    bvec = kk * a_gate

    akv_yk = {un: _mmb(m4[un][:, c:], prev("v", *un)) for un in units}

    ti = lax.broadcasted_iota(jnp.int32, (tb, tb), 0)
    si = lax.broadcasted_iota(jnp.int32, (tb, tb), 1)
    same_chunk = sum(((ti >= ck * c) & (ti < (ck + 1) * c) & (si >= ck * c)) for ck in range(nc)) > 0
    tri = jnp.where(same_chunk & (ti >= si), 1.0, 0.0).astype(BF16)
    l1, l2 = _split(logw)
    lgc = _dot(tri, l1) + _dot(tri, l2)
    lg_ends = [lgc[(ck + 1) * c - 1:(ck + 1) * c, :] for ck in range(nc)]
    lg_end = jnp.concatenate([jnp.broadcast_to(e, (c, dr)) for e in lg_ends], axis=0)
    ginv = jnp.exp(-lgc)
    gend = jnp.exp(lg_end - lgc)
    new = {"at": avec * jnp.exp(lgc - logw), "rt": r * jnp.exp(lgc), "bt": bvec * ginv,
           "kt": k2 * ginv, "v": v, "bh": bvec * gend, "kh": k2 * gend, "g": g}
    new_gend = jnp.exp(jnp.concatenate(lg_ends, axis=0))

    zpad = jnp.zeros((c, d), F32)
    w = {un: jnp.concatenate([m4[un][:c, :c], zpad, prev("at", *un), akv_yk[un][:c]], axis=1)
         for un in units}
    for lvl in range(6):
        rhs = w if lvl < 5 else {un: w[un][:, 2 * d:] for un in units}
        prod = {un: _mmb(w[un][:, :d], rhs[un]) for un in units}
        if lvl < 5:
            w = {un: jnp.concatenate([prod[un][:, :2 * d], prod[un][:, 2 * d:] + w[un][:, 2 * d:]],
                                     axis=1) for un in units}
        else:
            w = {un: prod[un] + w[un][:, 2 * d:] for un in units}

    new["bonus"] = headsum(r * k2 * rk_ref[...]) * v

    g_end = gend_scr[...]
    s = [s_scr[h] for h in heads]
    for ck in range(nc):
        su = [_mmb(jnp.concatenate([w[ck, h][:, :d], prev("rt", ck, h)], axis=0), s[h], nt=True)
              for h in heads]
        u = [su[h][:c] + w[ck, h][:, d:] for h in heads]
        rbu = [_mmb(m4[ck, h][c:, :c], u[h]) for h in heads]
        upd = [_mmb(jnp.concatenate([u[h], prev("v", ck, h)], axis=0).T,
                    jnp.concatenate([prev("bh", ck, h), prev("kh", ck, h)], axis=0))
               for h in heads]
        for h in heads:
            y_scr[rws[ck], sls[h]] = su[h][c:] + rbu[h] + akv_yk[ck, h][c:]
        s = [s[h] * g_end[ck:ck + 1, sls[h]] + upd[h] for h in heads]
    for h in heads:
        s_scr[h] = s[h]

    y = y_scr[...]
    inv_n = 1.0 / HEAD_DIM
    mean = headsum(y) * inv_n
    yc = y - mean
    var = headsum(yc * yc) * inv_n
    yn = yc * lax.rsqrt(var + GN_EPS) * lng_ref[...] + lnb_ref[...]
    o_ref[0] = ((yn + prep_scr[_PREP.index("bonus")]) * prep_scr[_PREP.index("g")]).astype(o_ref.dtype)

    for i, name in enumerate(_PREP):
        prep_scr[i] = new[name]
    gend_scr[...] = new_gend


def _rwkv(p3, mu_rkv, w0, w2p, a0, a2p, g2p, k_k, k_a, r_k, ln_g, ln_b, hsum, n_heads, lora_w, w_cast,
          layer):
    b, t, _ = p3.shape
    dr = n_heads * HEAD_DIM
    c = CHUNK * RWKV_CHUNKS
    nt = t // c
    lora_blk = (3 * dr) // lora_w
    row = lambda n: pl.BlockSpec((1, n), lambda i, j: (0, 0))
    full = lambda a: pl.BlockSpec(a.shape, lambda i, j: (0, 0))
    _, cast_r, cast_c = w_cast.shape
    slab_rows = _slab_rows(cast_r, b * (nt + 1))
    n_slab = cast_r // slab_rows
    slab_of = lambda i, j: jnp.minimum(i * (nt + 1) + j, n_slab - 1)
    slab_in = pl.BlockSpec((1, slab_rows, cast_c), lambda i, j: (layer, slab_of(i, j), 0))
    slab_out = pl.BlockSpec((slab_rows, cast_c), lambda i, j: (slab_of(i, j), 0))
    return pl.pallas_call(
        functools.partial(_rwkv_kernel, n_heads=n_heads),
        out_shape=(jax.ShapeDtypeStruct((b, t, dr), BF16), jax.ShapeDtypeStruct((cast_r, cast_c), BF16)),
        grid=(b, nt + 1),
        in_specs=[pl.BlockSpec((1, c, 3 * dr), lambda i, j: (i, jnp.minimum(j, nt - 1), 0)),
                  pl.BlockSpec((1, c, lora_w), lambda i, j: (i, jnp.minimum(j, nt - 1), lora_blk)),
                  row(3 * dr), row(dr), full(w2p), row(dr), full(a2p), full(g2p),
                  row(dr), row(dr), row(dr), row(dr), row(dr), full(hsum), slab_in],
        out_specs=(pl.BlockSpec((1, c, dr), lambda i, j: (i, jnp.maximum(j - 1, 0), 0)), slab_out),
        scratch_shapes=[pltpu.VMEM((n_heads, HEAD_DIM, HEAD_DIM), F32),
                        pltpu.VMEM((1, 3 * dr), F32),
                        pltpu.VMEM((1, lora_w // 2), F32),
                        pltpu.VMEM((c, dr), F32),
                        pltpu.VMEM((len(_PREP), c, dr), F32),
                        pltpu.VMEM((RWKV_CHUNKS, dr), F32)],
        compiler_params=_cparams(2),
        name="rwkv",
    )(p3, p3, mu_rkv, w0, w2p, a0, a2p, g2p, k_k, k_a, r_k, ln_g, ln_b, hsum, w_cast)


def _sbattn_kernel(q_ref, k_ref, v_ref, ca_ref, cb_ref, o_ref, ca_out_ref, cb_out_ref, *, tq, tk, nb, npairs):
    qi = pl.program_id(2)
    nsub = tq // tk
    ca_out_ref[...] = ca_ref[0].astype(BF16)
    cb_out_ref[...] = cb_ref[0].astype(BF16)
    first = lax.broadcasted_iota(jnp.int32, (tk, LANES), 1) < HEAD_DIM
    rr = lax.broadcasted_iota(jnp.int32, (2 * tk, 2 * tk), 0)
    cc = lax.broadcasted_iota(jnp.int32, (2 * tk, 2 * tk), 1)
    tri2 = jnp.where((rr >= cc) & ((rr < tk) == (cc < tk)), 1.0, 0.0).astype(BF16)

    def per_head_rows(blk):
        zero = jnp.zeros_like(blk)
        return jnp.concatenate([jnp.where(first, blk, zero), jnp.where(first, zero, blk)], axis=0)

    def add_rows(full, r0, delta):
        if r0 == 0:
            return full + delta
        return jnp.concatenate([full[:r0], full[r0:] + delta], axis=0)

    def blocks(starts, state, row0s, masks):
        jobs = [(p, u) for p in range(npairs) for u in range(len(starts))]
        lanes = [slice(p * LANES, (p + 1) * LANES) for p in range(npairs)]
        kcs = [per_head_rows(k_ref[0, pl.ds(starts[u], tk), lanes[p]]) for p, u in jobs]
        vcs = [per_head_rows(v_ref[0, pl.ds(starts[u], tk), lanes[p]]) for p, u in jobs]
        zs = [_dot_nt(q_ref[0, row0s[u]:, lanes[p]], kc) for (p, u), kc in zip(jobs, kcs)]
        sps = []
        for (p, u), z in zip(jobs, zs):
            sp = jnp.maximum(z, 0.0) + jnp.log(1.0 + jnp.exp2(-jnp.abs(z))) * LOG2E
            if masks[u] is not None:
                sp = jnp.where(masks[u], sp, 0.0)
            sps.append(sp.astype(BF16))
        css = [_dot(sp, tri2) for sp in sps]
        carries = [c for c, _ in state]
        pvs = []
        for (p, u), z, cs, vc in zip(jobs, zs, css, vcs):
            r0 = row0s[u]
            rows = tq - r0
            attn = jnp.exp2(jnp.minimum(z - cs, 0.0) - carries[p][r0:])
            if masks[u] is not None:
                attn = jnp.where(masks[u], attn, 0.0)
            pvs.append(_dot(attn.astype(BF16), vc))
            tot = jnp.concatenate([jnp.broadcast_to(cs[:, 0:1], (rows, tk)),
                                   jnp.broadcast_to(cs[:, tk:tk + 1], (rows, tk))], axis=1)
            carries[p] = add_rows(carries[p], r0, tot)
        accs = [a for _, a in state]
        for (p, u), pv in zip(jobs, pvs):
            accs[p] = add_rows(accs[p], row0s[u], pv)
        return tuple(zip(carries, accs))

    state = tuple((jnp.zeros((tq, 2 * tk), F32), jnp.zeros((tq, LANES), F32)) for _ in range(npairs))
    assert nsub == nb, "one group of diagonal blocks"
    subs = [nsub - 1 - u for u in range(nb)]
    starts = [pl.multiple_of(qi * tq + sub * tk, tk) for sub in subs]
    masks = []
    for sub in subs:
        rows = tq - sub * tk
        qpos = lax.broadcasted_iota(jnp.int32, (rows, 2 * tk), 0)
        col = lax.broadcasted_iota(jnp.int32, (rows, 2 * tk), 1)
        masks.append(jnp.where(col >= tk, col - tk, col) < qpos)
    base = jnp.maximum(qi * tq - nb * tk, 0)
    starts += [pl.multiple_of(base + (nb - 1 - u) * tk, tk) for u in range(nb)]
    has_earlier = jnp.broadcast_to(qi > 0, (tq, 2 * tk))
    state = blocks(starts, state, [sub * tk for sub in subs] + [0] * nb, masks + [has_earlier] * nb)

    def min_carry(st):
        m = st[0][0]
        for c_, _ in st[1:]:
            m = jnp.minimum(m, c_)
        return jnp.min(jnp.minimum(m[:, :tk], m[:, tk:]))

    n_steps = qi * (nsub // nb)

    def cond(loop):
        i, _, cmin = loop
        return (i < n_steps) & (cmin < SB_DEAD)

    def body(loop):
        i, st, _ = loop
        base = qi * tq - (i + 1) * (nb * tk)
        starts = [pl.multiple_of(base + (nb - 1 - u) * tk, tk) for u in range(nb)]
        st = blocks(starts, st, [0] * nb, [None] * nb)
        return i + 1, st, min_carry(st)

    _, state, _ = lax.while_loop(cond, body, (jnp.int32(1), state, min_carry(state)))
    o_ref[0] = jnp.concatenate([a for _, a in state], axis=1).astype(o_ref.dtype)


def _sbattn(qkv, cast_a, cast_b, layer, tq=256, tk=128, nb=2, npairs=4):
    b, t, ds3 = qkv.shape
    ds = ds3 // 3
    w = npairs * LANES
    ng = ds // w
    nq = t // tq
    n_steps = b * ng * nq

    def slabs(a):
        _, rows, cols = a.shape
        slab_rows = _slab_rows(rows, n_steps)
        n_slab = rows // slab_rows
        slab_of = lambda i, h, j: jnp.minimum((i * ng + h) * nq + j, n_slab - 1)
        return (pl.BlockSpec((1, slab_rows, cols), lambda i, h, j: (layer, slab_of(i, h, j), 0)),
                pl.BlockSpec((slab_rows, cols), lambda i, h, j: (slab_of(i, h, j), 0)))

    a_in, a_out = slabs(cast_a)
    b_in, b_out = slabs(cast_b)

    return pl.pallas_call(
        functools.partial(_sbattn_kernel, tq=tq, tk=tk, nb=nb, npairs=npairs),
        out_shape=(jax.ShapeDtypeStruct((b, t, ds), BF16),
                   jax.ShapeDtypeStruct(cast_a.shape[1:], BF16),
                   jax.ShapeDtypeStruct(cast_b.shape[1:], BF16)),
        grid=(b, ng, nq),
        in_specs=[pl.BlockSpec((1, tq, w), lambda i, h, j: (i, j, h)),
                  pl.BlockSpec((1, t, w), lambda i, h, j: (i, 0, ng + h)),
                  pl.BlockSpec((1, t, w), lambda i, h, j: (i, 0, 2 * ng + h)),
                  a_in, b_in],
        out_specs=(pl.BlockSpec((1, tq, w), lambda i, h, j: (i, j, h)), a_out, b_out),
        compiler_params=_cparams(3),
        name="sbattn",
    )(qkv, qkv, qkv, cast_a, cast_b)


def _outproj_kernel(yr_ref, ys_ref, w_ref, x_ref, gt_ref, g2_ref, sc_ref, sh_ref, x1_ref, h2_ref):
    dr = yr_ref.shape[1]
    mix = _dot(yr_ref[...], w_ref[:dr, :]) + _dot(ys_ref[...], w_ref[dr:, :])
    x1 = x_ref[...] + gt_ref[0] * mix
    x1_ref[...] = x1
    ms = jnp.mean(x1 * x1, axis=-1, keepdims=True)
    y = x1 * lax.rsqrt(ms + RMS_EPS) * g2_ref[...]
    h2_ref[...] = (y * (1.0 + sc_ref[0]) + sh_ref[0]).astype(BF16)


def _outproj(yr, ys, w_out, x2, gt1, gain2, sc2, sh2, seq, tm=512):
    m, d = x2.shape
    dr = yr.shape[1]
    ds = ys.shape[1]
    per_seq = seq // tm
    mod = pl.BlockSpec((1, 1, d), lambda i: (i // per_seq, 0, 0))
    return pl.pallas_call(
        _outproj_kernel,
        out_shape=(jax.ShapeDtypeStruct((m, d), F32), jax.ShapeDtypeStruct((m, d), BF16)),
        grid=(m // tm,),
        in_specs=[pl.BlockSpec((tm, dr), lambda i: (i, 0)),
                  pl.BlockSpec((tm, ds), lambda i: (i, 0)),
                  pl.BlockSpec(w_out.shape, lambda i: (0, 0)),
                  pl.BlockSpec((tm, d), lambda i: (i, 0)),
                  mod,
                  pl.BlockSpec((1, d), lambda i: (0, 0)),
                  mod, mod],
        out_specs=(pl.BlockSpec((tm, d), lambda i: (i, 0)), pl.BlockSpec((tm, d), lambda i: (i, 0))),
        compiler_params=_cparams(1),
        name="outproj",
    )(yr, ys, w_out, x2, gt1, gain2, sc2, sh2)


def _ffn_kernel(h_ref, wg_ref, wu_ref, wd_ref, x_ref, gt_ref, o_ref, acc_scr):
    j = pl.program_id(1)

    @pl.when(j == 0)
    def _():
        acc_scr[...] = jnp.zeros_like(acc_scr)

    h = h_ref[...]
    gate = _dot(h, wg_ref[...])
    up = _dot(h, wu_ref[...])
    act = (gate * _sigmoid(gate) * up).astype(BF16)
    acc_scr[...] += _dot(act, wd_ref[...])

    @pl.when(j == pl.num_programs(1) - 1)
    def _():
        o_ref[...] = x_ref[...] + gt_ref[0] * acc_scr[...]


def _ffn(h2, w_gu, w_down, x1, gt2, seq, tm=512, tf=512):
    m, d = h2.shape
    dff = w_down.shape[0]
    nf = dff // tf
    per_seq = seq // tm
    return pl.pallas_call(
        _ffn_kernel,
        out_shape=jax.ShapeDtypeStruct((m, d), F32),
        grid=(m // tm, nf),
        in_specs=[pl.BlockSpec((tm, d), lambda i, j: (i, 0)),
                  pl.BlockSpec((d, tf), lambda i, j: (0, j)),
                  pl.BlockSpec((d, tf), lambda i, j: (0, j + nf)),
                  pl.BlockSpec((tf, d), lambda i, j: (j, 0)),
                  pl.BlockSpec((tm, d), lambda i, j: (i, 0)),
                  pl.BlockSpec((1, 1, d), lambda i, j: (i // per_seq, 0, 0))],
        out_specs=pl.BlockSpec((tm, d), lambda i, j: (i, 0)),
        scratch_shapes=[pltpu.VMEM((tm, d), F32)],
        compiler_params=_cparams(2),
        name="ffn",
    )(h2, w_gu, w_gu, w_down, x1, gt2)


def _pad_to(a, n, axis):
    pad = [(0, 0)] * a.ndim
    pad[axis] = (0, n - a.shape[axis])
    return jnp.pad(a, pad)


def _layer(x, c_pad, w_ada, b_ada, norm1_gain, norm2_gain, w_in, mu_rkv, mu_w, mu_a, mu_g, w0, w1,
           w2, a0, a1, a2, g1, g2, k_k, k_a, r_k, ln_x_gain, ln_x_bias, q_norm_gain, k_norm_gain,
           w_out, w_gate_up, w_down, layer):
    b, t, d = x.shape
    dr = w0.shape[0]
    ds = d - dr
    n_rwkv = dr // HEAD_DIM
    n_sb = ds // HEAD_DIM
    row = lambda a: a.reshape(1, -1)

    mod = _ada(c_pad, w_ada, row(b_ada))[:b]
    sh1, sc1, gt1, sh2, sc2, gt2 = [m.reshape(b, 1, d) for m in jnp.split(mod, 6, axis=-1)]

    nw = -(-w1.shape[1] // LANES) * LANES
    na = -(-a1.shape[1] // LANES) * LANES
    ng = -(-g1.shape[1] // LANES) * LANES
    w_lora = _fold(w1, a1, g1, mu_w, mu_a, mu_g, (0, nw, nw + na), nw + na + ng)
    lora_w = w_lora.shape[1]
    assert dr == ds == lora_w, "column tiles of the input projection are one head group wide"

    hsum = jnp.kron(jnp.eye(MXU_N // HEAD_DIM, dtype=F32),
                    jnp.ones((HEAD_DIM, HEAD_DIM), F32)).astype(BF16)
    x2 = x.reshape(b * t, d)
    p, qkv = _inproj(x2, row(norm1_gain), sc1, sh1, w_in.astype(BF16), w_lora,
                     row(jnp.tile(q_norm_gain, n_sb)), row(jnp.tile(k_norm_gain, n_sb)), hsum, t,
                     3 * dr // lora_w)
    p3 = p.reshape(b, t, -1)
    y_rwkv, w_gu16 = _rwkv(p3, row(mu_rkv), row(w0), _pad_to(w2, nw, 0).astype(BF16), row(a0),
                           _pad_to(a2, na, 0).astype(BF16), _pad_to(g2, ng, 0).astype(BF16),
                           row(k_k), row(k_a), row(r_k), row(ln_x_gain), row(ln_x_bias), hsum,
                           n_rwkv, lora_w, w_gate_up, layer)

    y_sb, w_down16, w_out16 = _sbattn(qkv.reshape(b, t, -1), w_down, w_out, layer)

    x1, h2 = _outproj(y_rwkv.reshape(b * t, dr), y_sb.reshape(b * t, ds), w_out16, x2,
                      gt1, row(norm2_gain), sc2, sh2, t)
    out = _ffn(h2, w_gu16, w_down16, x1, gt2, t)
    return out.reshape(b, t, d)


def kernel(x, c, w_ada, b_ada, norm1_gain, norm2_gain, w_in, mu_rkv, mu_w, mu_a, mu_g, w0, w1, w2,
           a0, a1, a2, g1, g2, k_k, k_a, r_k, ln_x_gain, ln_x_bias, q_norm_gain, k_norm_gain, w_out,
           w_gate_up, w_down):
    depth = w_ada.shape[0]
    c_pad = _pad_to(c, SUBLANES, 0)
    for l in range(depth):
        x = _layer(x, c_pad, w_ada[l], b_ada[l], norm1_gain[l], norm2_gain[l], w_in[l], mu_rkv[l],
                   mu_w[l], mu_a[l], mu_g[l], w0[l], w1[l], w2[l], a0[l], a1[l], a2[l], g1[l],
                   g2[l], k_k[l], k_a[l], r_k[l].reshape(-1), ln_x_gain[l], ln_x_bias[l],
                   q_norm_gain[l], k_norm_gain[l], w_out, w_gate_up, w_down, l)
    return x
```

```python
import functools
import math

import jax
import jax.numpy as jnp
from jax import lax
from jax.experimental import pallas as pl
from jax.experimental.pallas import tpu as pltpu

F32 = jnp.float32
BF16 = jnp.bfloat16

HEAD_DIM = 64
RMS_EPS = 1e-6
GN_EPS = 64e-5
L2_EPS = 1e-12
LOG2E = math.log2(math.e)
SB_DEAD = 150.0
LANES = 128
SUBLANES = 8
MXU_N = 256
CHUNK = 64
RWKV_CHUNKS = 2
VMEM_LIMIT = 48 * 1024 * 1024
VMEM_LIMIT_BIG = 56 * 1024 * 1024


def _cparams(n_axes, vmem_limit=VMEM_LIMIT):
    return pltpu.CompilerParams(dimension_semantics=("arbitrary",) * n_axes,
                                vmem_limit_bytes=vmem_limit)


def _sigmoid(x):
    return 1.0 / (1.0 + jnp.exp(-x))


def _softplus(x):
    return jnp.maximum(x, 0.0) + jnp.log(1.0 + jnp.exp(-jnp.abs(x)))


def _dot(x, y):
    return jnp.dot(x, y, preferred_element_type=F32)


def _dot_nt(x, y):
    return lax.dot_general(x, y, (((1,), (1,)), ((), ())), preferred_element_type=F32)


def _split(x):
    hi = x.astype(BF16)
    lo = (x - hi.astype(F32)).astype(BF16)
    return hi, lo


def _mmb(x, y, nt=False):
    return (_dot_nt if nt else _dot)(x.astype(BF16), y.astype(BF16))


def _headsum(t, hs):
    g = hs.shape[0]
    return jnp.concatenate([_dot(t[:, i:i + g].astype(BF16), hs) for i in range(0, t.shape[1], g)],
                           axis=1)


def _ada_kernel(c_ref, w_ref, b_ref, o_ref):
    c = c_ref[...]
    ca = c * _sigmoid(c)
    o_ref[...] = _dot(ca.astype(BF16), w_ref[...].astype(BF16)) + b_ref[...]


def _ada(c_pad, w_ada, b_ada, tn=1024):
    m, d = c_pad.shape
    n = w_ada.shape[1]
    return pl.pallas_call(
        _ada_kernel,
        out_shape=jax.ShapeDtypeStruct((m, n), F32),
        grid=(n // tn,),
        in_specs=[pl.BlockSpec((m, d), lambda j: (0, 0)),
                  pl.BlockSpec((d, tn), lambda j: (0, j)),
                  pl.BlockSpec((1, tn), lambda j: (0, j))],
        out_specs=pl.BlockSpec((m, tn), lambda j: (0, j)),
        compiler_params=_cparams(1),
        name="ada",
    )(c_pad, w_ada, b_ada)


def _fold_kernel(w1_ref, a1_ref, g1_ref, muw_ref, mua_ref, mug_ref, o_ref, *, offsets):
    o_ref[...] = jnp.zeros_like(o_ref)
    half = o_ref.shape[1] // 2
    for w_ref, mu_ref, off in zip((w1_ref, a1_ref, g1_ref), (muw_ref, mua_ref, mug_ref), offsets):
        w = w_ref[...]
        mu = mu_ref[...]
        n = w.shape[1]
        o_ref[:, off:off + n] = (w * (1.0 - mu)).astype(BF16)
        o_ref[:, half + off:half + off + n] = (w * mu).astype(BF16)


def _fold(w1, a1, g1, mu_w, mu_a, mu_g, offsets, half):
    d = w1.shape[0]
    col = lambda m: m.reshape(d, 1)
    return pl.pallas_call(
        functools.partial(_fold_kernel, offsets=offsets),
        out_shape=jax.ShapeDtypeStruct((d, 2 * half), BF16),
        compiler_params=pltpu.CompilerParams(vmem_limit_bytes=VMEM_LIMIT),
        name="fold",
    )(w1, a1, g1, col(mu_w), col(mu_a), col(mu_g))


INPROJ_SLICES = 4
CAST_ROWS = 16


def _slab_rows(n_rows, n_steps):
    units = n_rows // CAST_ROWS
    assert units * CAST_ROWS == n_rows
    for k in range(1, units + 1):
        if units % k == 0 and units // k <= n_steps:
            return k * CAST_ROWS
    raise ValueError("no slab height fits")


def _inproj_kernel(x_ref, g_ref, sc_ref, sh_ref, w_ref, wl_ref, qg_ref, kg_ref, hs_ref, p_ref, s_ref,
                   h_scr, *, n_rkv):
    i = pl.program_id(0)
    j = pl.program_id(1)
    rows = x_ref.shape[0]

    def prepare():
        x = x_ref[...]
        ms = jnp.mean(x * x, axis=-1, keepdims=True)
        y = x * lax.rsqrt(ms + RMS_EPS) * g_ref[...]
        sl = jnp.clip(j - 1, 0, INPROJ_SLICES - 1)
        start = pl.multiple_of(sl * rows, rows)
        h_scr[i % 2, pl.ds(start, rows), :] = (y * (1.0 + sc_ref[0]) + sh_ref[0]).astype(BF16)

    def h_cur():
        return h_scr[(i + 1) % 2]

    def norm(t, gain):
        ms = _headsum(t * t, hs_ref[...]) * (1.0 / HEAD_DIM)
        return t * lax.rsqrt(ms + RMS_EPS) * gain

    @pl.when(i == 0)
    def _():
        prepare()

    @pl.when((i > 0) & (j < n_rkv))
    def _():
        p_ref[...] = _dot(h_cur(), w_ref[0].astype(BF16))
        prepare()

    @pl.when((i > 0) & (j == n_rkv))
    def _():
        p_ref[...] = _dot(h_cur(), wl_ref[...])
        prepare()

    @pl.when((i > 0) & (j == n_rkv + 1))
    def _():
        q = norm(_dot(h_cur(), w_ref[0].astype(BF16)), qg_ref[...])
        s_ref[...] = (q * (LOG2E / math.sqrt(HEAD_DIM))).astype(BF16)
        prepare()

    @pl.when((i > 0) & (j == n_rkv + 2))
    def _():
        s_ref[...] = norm(_dot(h_cur(), w_ref[0].astype(BF16)), kg_ref[...]).astype(BF16)
        prepare()

    @pl.when((i > 0) & (j == n_rkv + 3))
    def _():
        s_ref[...] = _dot(h_cur(), w_ref[0].astype(BF16)).astype(BF16)
        prepare()


def _inproj(x2, gain, sc, sh, w_in, layer, w_lora, qg, kg, hsum, seq, n_rkv, tm=1024):
    m, d = x2.shape
    tn = qg.shape[1]
    assert w_lora.shape[1] == tn and w_in.shape[2] == (n_rkv + 3) * tn
    nrow = m // tm
    nt = n_rkv + 4
    rows = tm // INPROJ_SLICES
    per_seq = seq // tm
    const = lambda a: pl.BlockSpec(a.shape, lambda i, j: (0,) * a.ndim)
    batch = lambda i, j: (jnp.minimum(i, nrow - 1) // per_seq, 0, 0)
    prev_row = lambda i: jnp.maximum(i - 1, 0)
    return pl.pallas_call(
        functools.partial(_inproj_kernel, n_rkv=n_rkv),
        out_shape=(jax.ShapeDtypeStruct((m, (n_rkv + 1) * tn), F32),
                   jax.ShapeDtypeStruct((m, 3 * tn), BF16)),
        grid=(nrow + 1, nt),
        in_specs=[pl.BlockSpec((rows, d), lambda i, j: (jnp.minimum(
                      INPROJ_SLICES * i + jnp.clip(j - 1, 0, INPROJ_SLICES - 1),
                      INPROJ_SLICES * nrow - 1), 0)),
                  const(gain),
                  pl.BlockSpec((1, 1, d), batch),
                  pl.BlockSpec((1, 1, d), batch),
                  pl.BlockSpec((1, d, tn), lambda i, j: (layer, 0, jnp.where(j < n_rkv, j, jnp.maximum(j - 1, n_rkv - 1)))),
                  const(w_lora), const(qg), const(kg), const(hsum)],
        out_specs=(pl.BlockSpec((tm, tn), lambda i, j: (prev_row(i), jnp.where(i == 0, 0, jnp.minimum(j, n_rkv)))),
                   pl.BlockSpec((tm, tn), lambda i, j: (prev_row(i), jnp.where(i == 0, 0, jnp.maximum(j - n_rkv - 1, 0))))),
        scratch_shapes=[pltpu.VMEM((2, tm, d), BF16)],
        compiler_params=_cparams(2, VMEM_LIMIT_BIG),
        name="inproj",
    )(x2, gain, sc, sh, w_in, w_lora, qg, kg, hsum)


def _shift_rows(cur, prev_row):
    rolled = pltpu.roll(cur, 1, axis=0)
    row = lax.broadcasted_iota(jnp.int32, cur.shape, 0)
    return jnp.where(row == 0, jnp.broadcast_to(prev_row, cur.shape), rolled)


_PREP = ("at", "rt", "bt", "kt", "v", "bh", "kh", "bonus", "g")


def _rwkv_kernel(p_ref, l_ref, mu_ref, w0_ref, w2_ref, a0_ref, a2_ref, g2_ref, kk_ref, ka_ref,
                 rk_ref, lng_ref, lnb_ref, hs_ref, cast_ref, o_ref, cast_out_ref, s_scr, prevp_scr,
                 prevl_scr, y_scr, prep_scr, gend_scr, *, n_heads):
    dr = n_heads * HEAD_DIM
    c = CHUNK
    nc = RWKV_CHUNKS
    d = HEAD_DIM
    heads = range(n_heads)
    cast_out_ref[...] = cast_ref[0].astype(BF16)
    units = [(ck, h) for ck in range(nc) for h in heads]
    sls = [slice(h * d, (h + 1) * d) for h in heads]
    rws = [slice(ck * c, (ck + 1) * c) for ck in range(nc)]

    @pl.when(pl.program_id(1) == 0)
    def _():
        s_scr[...] = jnp.zeros_like(s_scr)
        prevp_scr[...] = jnp.zeros_like(prevp_scr)
        prevl_scr[...] = jnp.zeros_like(prevl_scr)
        prep_scr[...] = jnp.zeros_like(prep_scr)
        gend_scr[...] = jnp.zeros_like(gend_scr)

    headsum = functools.partial(_headsum, hs=hs_ref[...])

    def prev(name, ck, h):
        i = _PREP.index(name)
        pair = prep_scr[i, rws[ck], (h // 2) * LANES:(h // 2 + 1) * LANES]
        return pair[:, (h % 2) * d:(h % 2 + 1) * d]

    rows2 = lax.broadcasted_iota(jnp.int32, (2 * c, 2 * c), 0)
    cols2 = lax.broadcasted_iota(jnp.int32, (2 * c, 2 * c), 1)
    tt = jnp.where(rows2 >= c, rows2 - c, rows2)
    ii = jnp.where(cols2 >= c, cols2 - c, cols2)
    quad_mask = tt + jnp.where(rows2 >= c, 1, 0) > ii

    m4 = {}
    for un in units:
        ar = jnp.concatenate([prev("at", *un), prev("rt", *un)], axis=0)
        bk = jnp.concatenate([prev("bt", *un), prev("kt", *un)], axis=0)
        m4[un] = jnp.where(quad_mask, _mmb(ar, bk, nt=True), 0.0)

    p = p_ref[0]
    tb = nc * c
    pshift = _shift_rows(p, prevp_scr[...])
    prevp_scr[...] = p[tb - 1:tb, :]
    pm = p + (pshift - p) * mu_ref[...]
    r = pm[:, :dr]
    k = pm[:, dr:2 * dr]
    v = pm[:, 2 * dr:]

    l = l_ref[0]
    half = l.shape[1] // 2
    lb = l[:, half:]
    lin = l[:, :half] + _shift_rows(lb, prevl_scr[...])
    prevl_scr[...] = lb[tb - 1:tb, :]
    nw = w2_ref.shape[0]
    na = a2_ref.shape[0]
    lw = jnp.tanh(lin[:, :nw])
    la = lin[:, nw:nw + na]
    lg = _sigmoid(lin[:, nw + na:])
    w_pre = w0_ref[...] + _dot(lw.astype(BF16), w2_ref[...])
    w_log = -_softplus(-w_pre) - 0.5
    logw = -jnp.exp(w_log)
    a_gate = _sigmoid(a0_ref[...] + _dot(la.astype(BF16), a2_ref[...]))
    g = _dot(lg.astype(BF16), g2_ref[...])
    kk = k * kk_ref[...]
    kk = kk * lax.rsqrt(headsum(kk * kk) + L2_EPS)
    k2 = k * (1.0 + (a_gate - 1.0) * ka_ref[...])
    avec = -kk
    bvec = kk * a_gate

    akv_yk = {un: _mmb(m4[un][:, c:], prev("v", *un)) for un in units}

    ti = lax.broadcasted_iota(jnp.int32, (tb, tb), 0)
    si = lax.broadcasted_iota(jnp.int32, (tb, tb), 1)
    same_chunk = sum(((ti >= ck * c) & (ti < (ck + 1) * c) & (si >= ck * c)) for ck in range(nc)) > 0
    tri = jnp.where(same_chunk & (ti >= si), 1.0, 0.0).astype(BF16)
    l1, l2 = _split(logw)
    lgc = _dot(tri, l1) + _dot(tri, l2)
    lg_ends = [lgc[(ck + 1) * c - 1:(ck + 1) * c, :] for ck in range(nc)]
    lg_end = jnp.concatenate([jnp.broadcast_to(e, (c, dr)) for e in lg_ends], axis=0)
    ginv = jnp.exp(-lgc)
    gend = jnp.exp(lg_end - lgc)
    new = {"at": avec * jnp.exp(lgc - logw), "rt": r * jnp.exp(lgc), "bt": bvec * ginv,
           "kt": k2 * ginv, "v": v, "bh": bvec * gend, "kh": k2 * gend, "g": g}
    new_gend = jnp.exp(jnp.concatenate(lg_ends, axis=0))

    zpad = jnp.zeros((c, d), F32)
    w = {un: jnp.concatenate([m4[un][:c, :c], zpad, prev("at", *un), akv_yk[un][:c]], axis=1)
         for un in units}
    for lvl in range(6):
        rhs = w if lvl < 5 else {un: w[un][:, 2 * d:] for un in units}
        prod = {un: _mmb(w[un][:, :d], rhs[un]) for un in units}
        if lvl < 5:
            w = {un: jnp.concatenate([prod[un][:, :2 * d], prod[un][:, 2 * d:] + w[un][:, 2 * d:]],
                                     axis=1) for un in units}
        else:
            w = {un: prod[un] + w[un][:, 2 * d:] for un in units}

    new["bonus"] = headsum(r * k2 * rk_ref[...]) * v

    g_end = gend_scr[...]
    s = [s_scr[h] for h in heads]
    for ck in range(nc):
        su = [_mmb(jnp.concatenate([w[ck, h][:, :d], prev("rt", ck, h)], axis=0), s[h], nt=True)
              for h in heads]
        u = [su[h][:c] + w[ck, h][:, d:] for h in heads]
        rbu = [_mmb(m4[ck, h][c:, :c], u[h]) for h in heads]
        upd = [_mmb(jnp.concatenate([u[h], prev("v", ck, h)], axis=0).T,
                    jnp.concatenate([prev("bh", ck, h), prev("kh", ck, h)], axis=0))
               for h in heads]
        for h in heads:
            y_scr[rws[ck], sls[h]] = su[h][c:] + rbu[h] + akv_yk[ck, h][c:]
        s = [s[h] * g_end[ck:ck + 1, sls[h]] + upd[h] for h in heads]
    for h in heads:
        s_scr[h] = s[h]

    y = y_scr[...]
    inv_n = 1.0 / HEAD_DIM
    mean = headsum(y) * inv_n
    yc = y - mean
    var = headsum(yc * yc) * inv_n
    yn = yc * lax.rsqrt(var + GN_EPS) * lng_ref[...] + lnb_ref[...]
    o_ref[0] = ((yn + prep_scr[_PREP.index("bonus")]) * prep_scr[_PREP.index("g")]).astype(o_ref.dtype)

    for i, name in enumerate(_PREP):
        prep_scr[i] = new[name]
    gend_scr[...] = new_gend


def _rwkv(p3, mu_rkv, w0, w2p, a0, a2p, g2p, k_k, k_a, r_k, ln_g, ln_b, hsum, n_heads, lora_w, w_cast,
          layer):
    b, t, _ = p3.shape
    dr = n_heads * HEAD_DIM
    c = CHUNK * RWKV_CHUNKS
    nt = t // c
    lora_blk = (3 * dr) // lora_w
    row = lambda n: pl.BlockSpec((1, n), lambda i, j: (0, 0))
    full = lambda a: pl.BlockSpec(a.shape, lambda i, j: (0, 0))
    _, cast_r, cast_c = w_cast.shape
    slab_rows = _slab_rows(cast_r, b * (nt + 1))
    n_slab = cast_r // slab_rows
    slab_of = lambda i, j: jnp.minimum(i * (nt + 1) + j, n_slab - 1)
    slab_in = pl.BlockSpec((1, slab_rows, cast_c), lambda i, j: (layer, slab_of(i, j), 0))
    slab_out = pl.BlockSpec((slab_rows, cast_c), lambda i, j: (slab_of(i, j), 0))
    return pl.pallas_call(
        functools.partial(_rwkv_kernel, n_heads=n_heads),
        out_shape=(jax.ShapeDtypeStruct((b, t, dr), BF16), jax.ShapeDtypeStruct((cast_r, cast_c), BF16)),
        grid=(b, nt + 1),
        in_specs=[pl.BlockSpec((1, c, 3 * dr), lambda i, j: (i, jnp.minimum(j, nt - 1), 0)),
                  pl.BlockSpec((1, c, lora_w), lambda i, j: (i, jnp.minimum(j, nt - 1), lora_blk)),
                  row(3 * dr), row(dr), full(w2p), row(dr), full(a2p), full(g2p),
                  row(dr), row(dr), row(dr), row(dr), row(dr), full(hsum), slab_in],
        out_specs=(pl.BlockSpec((1, c, dr), lambda i, j: (i, jnp.maximum(j - 1, 0), 0)), slab_out),
        scratch_shapes=[pltpu.VMEM((n_heads, HEAD_DIM, HEAD_DIM), F32),
                        pltpu.VMEM((1, 3 * dr), F32),
                        pltpu.VMEM((1, lora_w // 2), F32),
                        pltpu.VMEM((c, dr), F32),
                        pltpu.VMEM((len(_PREP), c, dr), F32),
                        pltpu.VMEM((RWKV_CHUNKS, dr), F32)],
        compiler_params=_cparams(2),
        name="rwkv",
    )(p3, p3, mu_rkv, w0, w2p, a0, a2p, g2p, k_k, k_a, r_k, ln_g, ln_b, hsum, w_cast)


def _sbattn_kernel(q_ref, k_ref, v_ref, ca_ref, cb_ref, o_ref, ca_out_ref, cb_out_ref, *, tq, tk, nb, npairs):
    qi = pl.program_id(2)
    nsub = tq // tk
    ca_out_ref[...] = ca_ref[0].astype(BF16)
    cb_out_ref[...] = cb_ref[0].astype(BF16)
    first = lax.broadcasted_iota(jnp.int32, (tk, LANES), 1) < HEAD_DIM
    rr = lax.broadcasted_iota(jnp.int32, (2 * tk, 2 * tk), 0)
    cc = lax.broadcasted_iota(jnp.int32, (2 * tk, 2 * tk), 1)
    tri2 = jnp.where((rr >= cc) & ((rr < tk) == (cc < tk)), 1.0, 0.0).astype(BF16)

    def per_head_rows(blk):
        zero = jnp.zeros_like(blk)
        return jnp.concatenate([jnp.where(first, blk, zero), jnp.where(first, zero, blk)], axis=0)

    def add_rows(full, r0, delta):
        if r0 == 0:
            return full + delta
        return jnp.concatenate([full[:r0], full[r0:] + delta], axis=0)

    def blocks(starts, state, row0s, masks):
        jobs = [(p, u) for p in range(npairs) for u in range(len(starts))]
        lanes = [slice(p * LANES, (p + 1) * LANES) for p in range(npairs)]
        kcs = [per_head_rows(k_ref[0, pl.ds(starts[u], tk), lanes[p]]) for p, u in jobs]
        vcs = [per_head_rows(v_ref[0, pl.ds(starts[u], tk), lanes[p]]) for p, u in jobs]
        zs = [_dot_nt(q_ref[0, row0s[u]:, lanes[p]], kc) for (p, u), kc in zip(jobs, kcs)]
        sps = []
        for (p, u), z in zip(jobs, zs):
            sp = jnp.maximum(z, 0.0) + jnp.log(1.0 + jnp.exp2(-jnp.abs(z))) * LOG2E
            if masks[u] is not None:
                sp = jnp.where(masks[u], sp, 0.0)
            sps.append(sp.astype(BF16))
        css = [_dot(sp, tri2) for sp in sps]
        carries = [c for c, _ in state]
        pvs = []
        for (p, u), z, cs, vc in zip(jobs, zs, css, vcs):
            r0 = row0s[u]
            rows = tq - r0
            attn = jnp.exp2(jnp.minimum(z - cs, 0.0) - carries[p][r0:])
            if masks[u] is not None:
                attn = jnp.where(masks[u], attn, 0.0)
            pvs.append(_dot(attn.astype(BF16), vc))
            tot = jnp.concatenate([jnp.broadcast_to(cs[:, 0:1], (rows, tk)),
                                   jnp.broadcast_to(cs[:, tk:tk + 1], (rows, tk))], axis=1)
            carries[p] = add_rows(carries[p], r0, tot)
        accs = [a for _, a in state]
        for (p, u), pv in zip(jobs, pvs):
            accs[p] = add_rows(accs[p], row0s[u], pv)
        return tuple(zip(carries, accs))

    state = tuple((jnp.zeros((tq, 2 * tk), F32), jnp.zeros((tq, LANES), F32)) for _ in range(npairs))
    assert nsub == nb, "one group of diagonal blocks"
    subs = [nsub - 1 - u for u in range(nb)]
    starts = [pl.multiple_of(qi * tq + sub * tk, tk) for sub in subs]
    masks = []
    for sub in subs:
        rows = tq - sub * tk
        qpos = lax.broadcasted_iota(jnp.int32, (rows, 2 * tk), 0)
        col = lax.broadcasted_iota(jnp.int32, (rows, 2 * tk), 1)
        masks.append(jnp.where(col >= tk, col - tk, col) < qpos)
    base = jnp.maximum(qi * tq - nb * tk, 0)
    starts += [pl.multiple_of(base + (nb - 1 - u) * tk, tk) for u in range(nb)]
    has_earlier = jnp.broadcast_to(qi > 0, (tq, 2 * tk))
    state = blocks(starts, state, [sub * tk for sub in subs] + [0] * nb, masks + [has_earlier] * nb)

    def min_carry(st):
        m = st[0][0]
        for c_, _ in st[1:]:
            m = jnp.minimum(m, c_)
        return jnp.min(jnp.minimum(m[:, :tk], m[:, tk:]))

    n_steps = qi * (nsub // nb)

    def cond(loop):
        i, _, cmin = loop
        return (i < n_steps) & (cmin < SB_DEAD)

    def body(loop):
        i, st, _ = loop
        base = qi * tq - (i + 1) * (nb * tk)
        starts = [pl.multiple_of(base + (nb - 1 - u) * tk, tk) for u in range(nb)]
        st = blocks(starts, st, [0] * nb, [None] * nb)
        return i + 1, st, min_carry(st)

    _, state, _ = lax.while_loop(cond, body, (jnp.int32(1), state, min_carry(state)))
    o_ref[0] = jnp.concatenate([a for _, a in state], axis=1).astype(o_ref.dtype)


def _sbattn(qkv, cast_a, cast_b, layer, tq=256, tk=128, nb=2, npairs=4):
    b, t, ds3 = qkv.shape
    ds = ds3 // 3
    w = npairs * LANES
    ng = ds // w
    nq = t // tq
    n_steps = b * ng * nq

    def slabs(a):
        _, rows, cols = a.shape
        slab_rows = _slab_rows(rows, n_steps)
        n_slab = rows // slab_rows
        slab_of = lambda i, h, j: jnp.minimum((i * ng + h) * nq + j, n_slab - 1)
        return (pl.BlockSpec((1, slab_rows, cols), lambda i, h, j: (layer, slab_of(i, h, j), 0)),
                pl.BlockSpec((slab_rows, cols), lambda i, h, j: (slab_of(i, h, j), 0)))

    a_in, a_out = slabs(cast_a)
    b_in, b_out = slabs(cast_b)

    return pl.pallas_call(
        functools.partial(_sbattn_kernel, tq=tq, tk=tk, nb=nb, npairs=npairs),
        out_shape=(jax.ShapeDtypeStruct((b, t, ds), BF16),
                   jax.ShapeDtypeStruct(cast_a.shape[1:], BF16),
                   jax.ShapeDtypeStruct(cast_b.shape[1:], BF16)),
        grid=(b, ng, nq),
        in_specs=[pl.BlockSpec((1, tq, w), lambda i, h, j: (i, j, h)),
                  pl.BlockSpec((1, t, w), lambda i, h, j: (i, 0, ng + h)),
                  pl.BlockSpec((1, t, w), lambda i, h, j: (i, 0, 2 * ng + h)),
                  a_in, b_in],
        out_specs=(pl.BlockSpec((1, tq, w), lambda i, h, j: (i, j, h)), a_out, b_out),
        compiler_params=_cparams(3),
        name="sbattn",
    )(qkv, qkv, qkv, cast_a, cast_b)


def _outproj_kernel(yr_ref, ys_ref, w_ref, x_ref, gt_ref, g2_ref, sc_ref, sh_ref, x1_ref, h2_ref):
    dr = yr_ref.shape[1]
    mix = _dot(yr_ref[...], w_ref[:dr, :]) + _dot(ys_ref[...], w_ref[dr:, :])
    x1 = x_ref[...] + gt_ref[0] * mix
    x1_ref[...] = x1
    ms = jnp.mean(x1 * x1, axis=-1, keepdims=True)
    y = x1 * lax.rsqrt(ms + RMS_EPS) * g2_ref[...]
    h2_ref[...] = (y * (1.0 + sc_ref[0]) + sh_ref[0]).astype(BF16)


def _outproj(yr, ys, w_out, x2, gt1, gain2, sc2, sh2, seq, tm=512):
    m, d = x2.shape
    dr = yr.shape[1]
    ds = ys.shape[1]
    per_seq = seq // tm
    mod = pl.BlockSpec((1, 1, d), lambda i: (i // per_seq, 0, 0))
    return pl.pallas_call(
        _outproj_kernel,
        out_shape=(jax.ShapeDtypeStruct((m, d), F32), jax.ShapeDtypeStruct((m, d), BF16)),
        grid=(m // tm,),
        in_specs=[pl.BlockSpec((tm, dr), lambda i: (i, 0)),
                  pl.BlockSpec((tm, ds), lambda i: (i, 0)),
                  pl.BlockSpec(w_out.shape, lambda i: (0, 0)),
                  pl.BlockSpec((tm, d), lambda i: (i, 0)),
                  mod,
                  pl.BlockSpec((1, d), lambda i: (0, 0)),
                  mod, mod],
        out_specs=(pl.BlockSpec((tm, d), lambda i: (i, 0)), pl.BlockSpec((tm, d), lambda i: (i, 0))),
        compiler_params=_cparams(1),
        name="outproj",
    )(yr, ys, w_out, x2, gt1, gain2, sc2, sh2)


def _ffn_kernel(h_ref, wg_ref, wu_ref, wd_ref, x_ref, gt_ref, o_ref, acc_scr):
    j = pl.program_id(1)

    @pl.when(j == 0)
    def _():
        acc_scr[...] = jnp.zeros_like(acc_scr)

    h = h_ref[...]
    gate = _dot(h, wg_ref[...])
    up = _dot(h, wu_ref[...])
    act = (gate * _sigmoid(gate) * up).astype(BF16)
    acc_scr[...] += _dot(act, wd_ref[...])

    @pl.when(j == pl.num_programs(1) - 1)
    def _():
        o_ref[...] = x_ref[...] + gt_ref[0] * acc_scr[...]


def _ffn(h2, w_gu, w_down, x1, gt2, seq, tm=512, tf=512):
    m, d = h2.shape
    dff = w_down.shape[0]
    nf = dff // tf
    per_seq = seq // tm
    return pl.pallas_call(
        _ffn_kernel,
        out_shape=jax.ShapeDtypeStruct((m, d), F32),
        grid=(m // tm, nf),
        in_specs=[pl.BlockSpec((tm, d), lambda i, j: (i, 0)),
                  pl.BlockSpec((d, tf), lambda i, j: (0, j)),
                  pl.BlockSpec((d, tf), lambda i, j: (0, j + nf)),
                  pl.BlockSpec((tf, d), lambda i, j: (j, 0)),
                  pl.BlockSpec((tm, d), lambda i, j: (i, 0)),
                  pl.BlockSpec((1, 1, d), lambda i, j: (i // per_seq, 0, 0))],
        out_specs=pl.BlockSpec((tm, d), lambda i, j: (i, 0)),
        scratch_shapes=[pltpu.VMEM((tm, d), F32)],
        compiler_params=_cparams(2),
        name="ffn",
    )(h2, w_gu, w_gu, w_down, x1, gt2)


def _pad_to(a, n, axis):
    pad = [(0, 0)] * a.ndim
    pad[axis] = (0, n - a.shape[axis])
    return jnp.pad(a, pad)


def _layer(x, c_pad, w_ada, b_ada, norm1_gain, norm2_gain, w_in, mu_rkv, mu_w, mu_a, mu_g, w0, w1,
           w2, a0, a1, a2, g1, g2, k_k, k_a, r_k, ln_x_gain, ln_x_bias, q_norm_gain, k_norm_gain,
           w_out, w_gate_up, w_down, layer):
    b, t, d = x.shape
    dr = w0.shape[0]
    ds = d - dr
    n_rwkv = dr // HEAD_DIM
    n_sb = ds // HEAD_DIM
    row = lambda a: a.reshape(1, -1)

    mod = _ada(c_pad, w_ada, row(b_ada))[:b]
    sh1, sc1, gt1, sh2, sc2, gt2 = [m.reshape(b, 1, d) for m in jnp.split(mod, 6, axis=-1)]

    nw = -(-w1.shape[1] // LANES) * LANES
    na = -(-a1.shape[1] // LANES) * LANES
    ng = -(-g1.shape[1] // LANES) * LANES
    w_lora = _fold(w1, a1, g1, mu_w, mu_a, mu_g, (0, nw, nw + na), nw + na + ng)
    lora_w = w_lora.shape[1]
    assert dr == ds == lora_w, "column tiles of the input projection are one head group wide"

    hsum = jnp.kron(jnp.eye(MXU_N // HEAD_DIM, dtype=F32),
                    jnp.ones((HEAD_DIM, HEAD_DIM), F32)).astype(BF16)
    x2 = x.reshape(b * t, d)
    p, qkv = _inproj(x2, row(norm1_gain), sc1, sh1, w_in, layer, w_lora,
                     row(jnp.tile(q_norm_gain, n_sb)), row(jnp.tile(k_norm_gain, n_sb)), hsum, t,
                     3 * dr // lora_w)
    p3 = p.reshape(b, t, -1)
    y_rwkv, w_gu16 = _rwkv(p3, row(mu_rkv), row(w0), _pad_to(w2, nw, 0).astype(BF16), row(a0),
                           _pad_to(a2, na, 0).astype(BF16), _pad_to(g2, ng, 0).astype(BF16),
                           row(k_k), row(k_a), row(r_k), row(ln_x_gain), row(ln_x_bias), hsum,
                           n_rwkv, lora_w, w_gate_up, layer)

    y_sb, w_down16, w_out16 = _sbattn(qkv.reshape(b, t, -1), w_down, w_out, layer)

    x1, h2 = _outproj(y_rwkv.reshape(b * t, dr), y_sb.reshape(b * t, ds), w_out16, x2,
                      gt1, row(norm2_gain), sc2, sh2, t)
    out = _ffn(h2, w_gu16, w_down16, x1, gt2, t)
    return out.reshape(b, t, d)


def kernel(x, c, w_ada, b_ada, norm1_gain, norm2_gain, w_in, mu_rkv, mu_w, mu_a, mu_g, w0, w1, w2,
           a0, a1, a2, g1, g2, k_k, k_a, r_k, ln_x_gain, ln_x_bias, q_norm_gain, k_norm_gain, w_out,
           w_gate_up, w_down):
    depth = w_ada.shape[0]
    c_pad = _pad_to(c, SUBLANES, 0)
    for l in range(depth):
        x = _layer(x, c_pad, w_ada[l], b_ada[l], norm1_gain[l], norm2_gain[l], w_in, mu_rkv[l],
                   mu_w[l], mu_a[l], mu_g[l], w0[l], w1[l], w2[l], a0[l], a1[l], a2[l], g1[l],
                   g2[l], k_k[l], k_a[l], r_k[l].reshape(-1), ln_x_gain[l], ln_x_bias[l],
                   q_norm_gain[l], k_norm_gain[l], w_out, w_gate_up, w_down, l)
    return x
```
